```python
import jax, jax.numpy as jnp
from jax import lax
import numpy as np

D_MODEL = 4096
BATCH = 8
SEQ = 2048
DEPTH = 1
DEC_BATCH = 16
DEC_SEQ = 32
PAST_LEN = 2048

CHUNK = 64
HEAD_DIM = 128
N_HEADS_A = 16
N_HEADS_B = 16
D_A = N_HEADS_A * HEAD_DIM
D_B = N_HEADS_B * HEAD_DIM
D_MIX = D_A + D_B
LEFT_CHUNKS = 8
BAND = (LEFT_CHUNKS + 1) * CHUNK
MAX_REL = 128
N_REL = 2 * MAX_REL + 1
Q_BLOCK = 128
D_FF = -(-8 * D_MODEL // (3 * 256)) * 256
EPS = 1e-6
SCALE = HEAD_DIM ** -0.5
SPLITS = (D_A, 2 * D_A, 3 * D_A, 3 * D_A + D_B, 3 * D_A + 2 * D_B, 3 * D_A + 3 * D_B)
N_PROJ = 3 * D_A + 3 * D_B + N_HEADS_A

kernel_name = 'hybrid_fox_chunkband_adaln_encoder_step'


def rms_norm(x, g):
    xf = x.astype(jnp.float32)
    y = xf * lax.rsqrt(jnp.mean(xf * xf, axis=-1, keepdims=True) + EPS)
    return (y * g.astype(jnp.float32)).astype(x.dtype)


def attend(q, k, v, bias, mask):
    s = jnp.einsum('bqhd,bkhd->bhqk', q, k, preferred_element_type=jnp.float32) * SCALE
    s = jnp.where(mask, s + bias, -1e30)
    p = jax.nn.softmax(s, axis=-1)
    return jnp.einsum('bhqk,bkhd->bqhd', p.astype(v.dtype), v)


def fox_attention(q, k, v, logf):
    B, Tk, H, Dh = k.shape
    Tq = q.shape[1]
    cum_k = jnp.cumsum(logf.astype(jnp.float32), axis=1).transpose(0, 2, 1)
    cum_q = cum_k[:, :, Tk - Tq:]
    qb = Q_BLOCK if Tq % Q_BLOCK == 0 else Tq
    nb = Tq // qb
    q_blocks = q.reshape(B, nb, qb, H, Dh).transpose(1, 0, 2, 3, 4)
    cq_blocks = cum_q.reshape(B, H, nb, qb).transpose(2, 0, 1, 3)
    pos_blocks = (Tk - Tq + jnp.arange(Tq)).reshape(nb, qb)
    kpos = jnp.arange(Tk)

    def one_block(args):
        qblk, cq, qpos = args
        bias = cq[:, :, :, None] - cum_k[:, :, None, :]
        mask = kpos[None, :] <= qpos[:, None]
        return attend(qblk, k, v, bias, mask)

    out = lax.map(one_block, (q_blocks, cq_blocks, pos_blocks))
    return out.transpose(1, 0, 2, 3, 4).reshape(B, Tq, H, Dh)


def rel_bias(rel_table, q_pos, k_pos):
    d = jnp.clip(q_pos[:, None] - k_pos[None, :], -MAX_REL, MAX_REL) + MAX_REL
    return rel_table[:, d]


def chunk_band_prompt(q, k, v, rel_table):
    B, T, H, Dh = q.shape
    nc = T // CHUNK
    left = LEFT_CHUNKS * CHUNK
    pad = ((0, 0), (left, 0), (0, 0), (0, 0))
    kp = jnp.pad(k, pad)
    vp = jnp.pad(v, pad)
    bias = rel_bias(rel_table, left + jnp.arange(CHUNK), jnp.arange(BAND))[None]
    q_chunks = q.reshape(B, nc, CHUNK, H, Dh).transpose(1, 0, 2, 3, 4)

    def one_chunk(args):
        n, qc = args
        start = n * CHUNK
        kb = lax.dynamic_slice_in_dim(kp, start, BAND, axis=1)
        vb = lax.dynamic_slice_in_dim(vp, start, BAND, axis=1)
        mask = (start - left + jnp.arange(BAND) >= 0)[None, :]
        return attend(qc, kb, vb, bias, mask)

    out = lax.map(one_chunk, (jnp.arange(nc), q_chunks))
    return out.transpose(1, 0, 2, 3, 4).reshape(B, T, H, Dh)


def chunk_band_sample(q, k, v, cache_k, cache_v, rel_table):
    keep = cache_k.shape[1]
    Tq = q.shape[1]
    kc = jnp.concatenate([cache_k, k.astype(cache_k.dtype)], axis=1)
    vc = jnp.concatenate([cache_v, v.astype(cache_v.dtype)], axis=1)
    bias = rel_bias(rel_table, keep + jnp.arange(Tq), jnp.arange(keep + Tq))[None]
    mask = jnp.ones((1, 1), dtype=bool)
    out = attend(q, kc, vc, bias, mask)
    return out, kc[:, -keep:], vc[:, -keep:]


def encoder_layer(x, c, cache, w_ada, b_ada, g_attn, g_ffn, w_in, b_f, g_q_a, g_k_a,
                  g_q_b, g_k_b, rel_table, w_o, w_gu, w_down):
    B, T, _ = x.shape
    mod = jnp.einsum('bd,dn->bn', jax.nn.silu(c), w_ada) + b_ada
    shift_a, scale_a, gate_a, shift_f, scale_f, gate_f = jnp.split(mod[:, None, :], 6, axis=-1)

    h = rms_norm(x, g_attn) * (1 + scale_a) + shift_a
    proj = jnp.einsum('btd,dn->btn', h, w_in)
    q_a, k_a, v_a, q_b, k_b, v_b, f_a = jnp.split(proj, SPLITS, axis=-1)
    heads_a = lambda t: t.reshape(B, T, N_HEADS_A, HEAD_DIM)
    heads_b = lambda t: t.reshape(B, T, N_HEADS_B, HEAD_DIM)
    q_a = rms_norm(heads_a(q_a), g_q_a)
    k_a = rms_norm(heads_a(k_a), g_k_a)
    v_a = heads_a(v_a)
    logf_a = jax.nn.log_sigmoid((f_a + b_f).astype(jnp.float32)).astype(x.dtype)
    q_b = rms_norm(heads_b(q_b), g_q_b)
    k_b = rms_norm(heads_b(k_b), g_k_b)
    v_b = heads_b(v_b)

    if cache is None:
        o_a = fox_attention(q_a, k_a, v_a, logf_a)
        o_b = chunk_band_prompt(q_b, k_b, v_b, rel_table)
        keep = min(LEFT_CHUNKS * CHUNK, T)
        new_k_b, new_v_b = k_b[:, -keep:], v_b[:, -keep:]
    else:
        ck_a, cv_a, clogf_a, ck_b, cv_b = cache
        o_a = fox_attention(q_a,
                            jnp.concatenate([ck_a, k_a.astype(ck_a.dtype)], axis=1),
                            jnp.concatenate([cv_a, v_a.astype(cv_a.dtype)], axis=1),
                            jnp.concatenate([clogf_a, logf_a.astype(clogf_a.dtype)], axis=1))
        o_b, new_k_b, new_v_b = chunk_band_sample(q_b, k_b, v_b, ck_b, cv_b, rel_table)

    mix = jnp.concatenate([o_a.reshape(B, T, D_A), o_b.reshape(B, T, D_B)], axis=-1)
    x = x + gate_a * jnp.einsum('btm,md->btd', mix, w_o)

    h2 = rms_norm(x, g_ffn) * (1 + scale_f) + shift_f
    g, u = jnp.split(jnp.einsum('btd,df->btf', h2, w_gu), 2, axis=-1)
    x = x + gate_f * jnp.einsum('btf,fd->btd', jax.nn.silu(g) * u, w_down)
    return x, k_a, v_a, logf_a, new_k_b, new_v_b


def setup_inputs(seed: int = 0) -> dict:
    key = jax.random.key(seed)
    ks = jax.random.split(key, 24)
    f32 = jnp.float32
    keep_b = min(LEFT_CHUNKS * CHUNK, PAST_LEN)

    def nrm(k, shape, s=1.0):
        return s * jax.random.normal(k, shape, f32)

    return {
        'x_prompt': nrm(ks[0], (BATCH, SEQ, D_MODEL)),
        'x_sample': nrm(ks[1], (DEC_BATCH, DEC_SEQ, D_MODEL)),
        'cache_k_a': nrm(ks[2], (DEPTH, DEC_BATCH, PAST_LEN, N_HEADS_A, HEAD_DIM)),
        'cache_v_a': nrm(ks[3], (DEPTH, DEC_BATCH, PAST_LEN, N_HEADS_A, HEAD_DIM)),
        'cache_logf_a': jax.nn.log_sigmoid(2.0 + nrm(ks[4], (DEPTH, DEC_BATCH, PAST_LEN, N_HEADS_A))),
        'cache_k_b': nrm(ks[5], (DEPTH, DEC_BATCH, keep_b, N_HEADS_B, HEAD_DIM)),
        'cache_v_b': nrm(ks[6], (DEPTH, DEC_BATCH, keep_b, N_HEADS_B, HEAD_DIM)),
        'c_prompt': nrm(ks[7], (BATCH, D_MODEL)),
        'c_sample': nrm(ks[8], (DEC_BATCH, D_MODEL)),
        'w_ada': nrm(ks[9], (DEPTH, D_MODEL, 6 * D_MODEL), D_MODEL ** -0.5),
        'b_ada': nrm(ks[10], (DEPTH, 6 * D_MODEL), 0.02),
        'g_attn': 1.0 + nrm(ks[11], (DEPTH, D_MODEL), 0.02),
        'g_ffn': 1.0 + nrm(ks[12], (DEPTH, D_MODEL), 0.02),
        'w_in': nrm(ks[13], (DEPTH, D_MODEL, N_PROJ), D_MODEL ** -0.5),
        'b_f': 2.0 + nrm(ks[14], (DEPTH, N_HEADS_A), 0.5),
        'g_q_a': 1.0 + nrm(ks[15], (DEPTH, HEAD_DIM), 0.02),
        'g_k_a': 1.0 + nrm(ks[16], (DEPTH, HEAD_DIM), 0.02),
        'g_q_b': 1.0 + nrm(ks[17], (DEPTH, HEAD_DIM), 0.02),
        'g_k_b': 1.0 + nrm(ks[18], (DEPTH, HEAD_DIM), 0.02),
        'rel_table': nrm(ks[19], (DEPTH, N_HEADS_B, N_REL), 0.5),
        'w_o': nrm(ks[20], (DEPTH, D_MIX, D_MODEL), D_MIX ** -0.5),
        'w_gu': nrm(ks[21], (DEPTH, D_MODEL, 2 * D_FF), D_MODEL ** -0.5),
        'w_down': nrm(ks[22], (DEPTH, D_FF, D_MODEL), D_FF ** -0.5),
    }


def reference(x_prompt, x_sample, cache_k_a, cache_v_a, cache_logf_a, cache_k_b, cache_v_b,
              c_prompt, c_sample, w_ada, b_ada, g_attn, g_ffn, w_in, b_f, g_q_a, g_k_a,
              g_q_b, g_k_b, rel_table, w_o, w_gu, w_down):
    y_prompt, y_sample = x_prompt, x_sample
    kap, vap, lap, kbp, vbp = [], [], [], [], []
    kas, vas, las, kbs, vbs = [], [], [], [], []
    for l in range(DEPTH):
        w = (w_ada[l], b_ada[l], g_attn[l], g_ffn[l], w_in[l], b_f[l], g_q_a[l], g_k_a[l],
             g_q_b[l], g_k_b[l], rel_table[l], w_o[l], w_gu[l], w_down[l])
        y_prompt, ka, va, la, kb, vb = encoder_layer(y_prompt, c_prompt, None, *w)
        kap.append(ka); vap.append(va); lap.append(la); kbp.append(kb); vbp.append(vb)
        cache = (cache_k_a[l], cache_v_a[l], cache_logf_a[l], cache_k_b[l], cache_v_b[l])
        y_sample, ka, va, la, kb, vb = encoder_layer(y_sample, c_sample, cache, *w)
        kas.append(ka); vas.append(va); las.append(la); kbs.append(kb); vbs.append(vb)
    new_k_a_prompt, new_v_a_prompt, new_logf_a_prompt = jnp.stack(kap), jnp.stack(vap), jnp.stack(lap)
    new_k_b_prompt, new_v_b_prompt = jnp.stack(kbp), jnp.stack(vbp)
    new_k_a_sample, new_v_a_sample, new_logf_a_sample = jnp.stack(kas), jnp.stack(vas), jnp.stack(las)
    new_k_b_sample, new_v_b_sample = jnp.stack(kbs), jnp.stack(vbs)
    return (y_prompt, y_sample,
            new_k_a_prompt, new_v_a_prompt, new_logf_a_prompt, new_k_b_prompt, new_v_b_prompt,
            new_k_a_sample, new_v_a_sample, new_logf_a_sample, new_k_b_sample, new_v_b_sample)
```

```python
import functools

import jax
import jax.numpy as jnp
from jax import lax
from jax.experimental import pallas as pl
from jax.experimental.pallas import tpu as pltpu

CHUNK = 64
LEFT_CHUNKS = 8
LEFT = LEFT_CHUNKS * CHUNK
MAX_REL = 128
EPS = 1e-6
NEG = -1e30

LANES = 128
BAND_TQ = 256
BAND_WIN = LEFT + BAND_TQ
BAND_EXT = BAND_WIN + LEFT
BAND_ROLL = 2048
CUM_BLK = 256
VMEM_LIMIT_BYTES = 56 * 1024 * 1024

F32 = jnp.float32
BF16 = jnp.bfloat16


def _params(*semantics):
    return pltpu.CompilerParams(dimension_semantics=semantics, vmem_limit_bytes=VMEM_LIMIT_BYTES)


def _tile(n, want):
    if n <= want:
        return n
    t = (want // LANES) * LANES
    while t >= LANES:
        if n % t == 0:
            return t
        t -= LANES
    raise ValueError(f"no lane-aligned tile for {n}")


def _ada_kernel(c_ref, w_ref, b_ref, o_ref):
    c = c_ref[...]
    a = (c * jax.nn.sigmoid(c)).astype(BF16)
    o_ref[...] = jnp.dot(a, w_ref[...].astype(BF16), preferred_element_type=F32) + b_ref[...]


def _ada(c, w_ada, b_ada):
    m, d = c.shape
    n = w_ada.shape[1]
    tn = _tile(n, 512)
    return pl.pallas_call(
        _ada_kernel,
        grid=(n // tn,),
        in_specs=[pl.BlockSpec((m, d), lambda j: (0, 0)),
                  pl.BlockSpec((d, tn), lambda j: (0, j)),
                  pl.BlockSpec((1, tn), lambda j: (0, j))],
        out_specs=pl.BlockSpec((m, tn), lambda j: (0, j)),
        out_shape=jax.ShapeDtypeStruct((m, n), F32),
        compiler_params=_params("arbitrary"),
        name="ada_mod",
    )(c, w_ada, b_ada.reshape(1, n))


def _modnorm_kernel(x_ref, sc_ref, sh_ref, g_ref, o_ref):
    x = x_ref[...]
    ms = jnp.mean(x * x, axis=-1, keepdims=True)
    y = x * lax.rsqrt(ms + EPS) * g_ref[...]
    o_ref[...] = (y * (1.0 + sc_ref[...]) + sh_ref[...]).astype(o_ref.dtype)


def _modnorm(x, scale, shift, g, nb, tr):
    b, t, d = x.shape
    bs = pl.BlockSpec((nb, 1, d), lambda i, j: (i, 0, 0))
    return pl.pallas_call(
        _modnorm_kernel,
        grid=(b // nb, t // tr),
        in_specs=[pl.BlockSpec((nb, tr, d), lambda i, j: (i, j, 0)), bs, bs,
                  pl.BlockSpec((1, 1, d), lambda i, j: (0, 0, 0))],
        out_specs=pl.BlockSpec((nb, tr, d), lambda i, j: (i, j, 0)),
        out_shape=jax.ShapeDtypeStruct((b, t, d), BF16),
        compiler_params=_params("arbitrary", "arbitrary"),
        name="modnorm",
    )(x, scale.reshape(b, 1, d), shift.reshape(b, 1, d), g.reshape(1, 1, d))


def _proj_kernel(a_ref, w_ref, g_ref, *o_refs, do_rms, dh):
    acc = jnp.dot(a_ref[...], w_ref[...], preferred_element_type=F32)
    if not do_rms:
        for o_ref in o_refs:
            o_ref[...] = acc.astype(o_ref.dtype)
        return
    g = g_ref[...]
    for hh in range(acc.shape[1] // dh):
        blk = acc[:, hh * dh:(hh + 1) * dh]
        ms = jnp.mean(blk * blk, axis=-1, keepdims=True)
        y = blk * lax.rsqrt(ms + EPS) * g
        for o_ref in o_refs:
            o_ref[:, hh * dh:(hh + 1) * dh] = y.astype(o_ref.dtype)


def _proj(a, w, col0, n, g, out_dtypes, *, do_rms, dh, tm, tn):
    m, k = a.shape
    tm = min(tm, m)
    tn = _tile(n, tn)
    j0 = col0 // tn
    assert col0 % tn == 0 and m % tm == 0
    out_spec = pl.BlockSpec((tm, tn), lambda i, j: (i, j))
    outs = pl.pallas_call(
        functools.partial(_proj_kernel, do_rms=do_rms, dh=dh),
        grid=(m // tm, n // tn),
        in_specs=[pl.BlockSpec((tm, k), lambda i, j: (i, 0)),
                  pl.BlockSpec((k, tn), lambda i, j: (0, j0 + j)),
                  pl.BlockSpec((1, dh), lambda i, j: (0, 0))],
        out_specs=[out_spec] * len(out_dtypes),
        out_shape=[jax.ShapeDtypeStruct((m, n), dt) for dt in out_dtypes],
        compiler_params=_params("arbitrary", "arbitrary"),
        name="proj",
    )(a, w, g.reshape(1, dh).astype(F32))
    return outs


def _gate_kernel(a_ref, w_ref, b_ref, o_ref):
    z = jnp.dot(a_ref[...], w_ref[...], preferred_element_type=F32)
    z = z[:, :o_ref.shape[1]] + b_ref[...]
    o_ref[...] = jnp.minimum(z, 0.0) - jnp.log1p(jnp.exp(-jnp.abs(z)))


def _forget_gate(a, w_f, b_f, tm):
    m, k = a.shape
    nh = b_f.shape[0]
    tm = min(tm, m)
    return pl.pallas_call(
        _gate_kernel,
        grid=(m // tm,),
        in_specs=[pl.BlockSpec((tm, k), lambda i: (i, 0)),
                  pl.BlockSpec((k, LANES), lambda i: (0, 0)),
                  pl.BlockSpec((1, nh), lambda i: (0, 0))],
        out_specs=pl.BlockSpec((tm, nh), lambda i: (i, 0)),
        out_shape=jax.ShapeDtypeStruct((m, nh), F32),
        compiler_params=_params("arbitrary"),
        name="forget_gate",
    )(a, w_f, b_f.reshape(1, nh).astype(F32))


def _cumsum_kernel(x_ref, o_ref):
    h, tp = x_ref.shape[1], x_ref.shape[2]
    r = lax.broadcasted_iota(jnp.int32, (CUM_BLK, CUM_BLK), 0)
    c = lax.broadcasted_iota(jnp.int32, (CUM_BLK, CUM_BLK), 1)
    tri = (r <= c).astype(BF16)
    carry = jnp.zeros((h, 1), F32)
    for blk in range(tp // CUM_BLK):
        x = x_ref[0, :, blk * CUM_BLK:(blk + 1) * CUM_BLK]
        hi = x.astype(BF16)
        r1 = x - hi.astype(F32)
        mid = r1.astype(BF16)
        lo = (r1 - mid.astype(F32)).astype(BF16)
        cs = (jnp.dot(hi, tri, preferred_element_type=F32)
              + jnp.dot(mid, tri, preferred_element_type=F32)
              + jnp.dot(lo, tri, preferred_element_type=F32)) + carry
        o_ref[0, :, blk * CUM_BLK:(blk + 1) * CUM_BLK] = cs
        carry = cs[:, CUM_BLK - 1:CUM_BLK]


def _cumsum_t(logf_t):
    b, h, tp = logf_t.shape
    return pl.pallas_call(
        _cumsum_kernel,
        grid=(b,),
        in_specs=[pl.BlockSpec((1, h, tp), lambda i: (i, 0, 0))],
        out_specs=pl.BlockSpec((1, h, tp), lambda i: (i, 0, 0)),
        out_shape=jax.ShapeDtypeStruct((b, h, tp), F32),
        compiler_params=_params("arbitrary"),
        name="logf_cumsum",
    )(logf_t)


def _head_column(cc, h):
    lane = lax.broadcasted_iota(jnp.int32, cc.shape, 1)
    return jnp.sum(jnp.where(lane == h, cc, 0.0), axis=1, keepdims=True)


def _qk(q, k):
    return lax.dot_general(q, k, (((1,), (1,)), ((), ())), preferred_element_type=F32)


def _fox_kernel(q_ref, k_ref, v_ref, cc_ref, cr_ref, o_ref, *, tq):
    h = pl.program_id(1)
    i = pl.program_id(2)
    q = q_ref[0]
    cq = _head_column(cc_ref[0], h)
    dh = q.shape[1]

    def scores(kb):
        off = pl.multiple_of(kb * tq, tq)
        k = k_ref[0, pl.ds(off, tq), :]
        v = v_ref[0, pl.ds(off, tq), :]
        s = _qk(q, k) + (cq - cr_ref[0, 0, kb])
        return s, v

    def update(carry, s, v):
        m, l, acc = carry
        m_new = jnp.maximum(m, jnp.max(s, axis=1, keepdims=True))
        alpha = jnp.exp(m - m_new)
        p = jnp.exp(s - m_new)
        l = alpha * l + jnp.sum(p, axis=1, keepdims=True)
        acc = alpha * acc + jnp.dot(p.astype(BF16), v, preferred_element_type=F32)
        return m_new, l, acc

    def body(kb, carry):
        s, v = scores(kb)
        return update(carry, s, v)

    init = (jnp.full((tq, 1), NEG, F32), jnp.zeros((tq, 1), F32), jnp.zeros((tq, dh), F32))
    carry = lax.fori_loop(0, i, body, init)
    s, v = scores(i)
    r = lax.broadcasted_iota(jnp.int32, s.shape, 0)
    c = lax.broadcasted_iota(jnp.int32, s.shape, 1)
    s = jnp.where(c <= r, s, NEG)
    _, l, acc = update(carry, s, v)
    o_ref[0] = (acc / l).astype(o_ref.dtype)


def _fox_prompt(q, k, v, cum_col, cum_row, nh, dh, tq):
    b, t, _ = q.shape
    nq = t // tq
    return pl.pallas_call(
        functools.partial(_fox_kernel, tq=tq),
        grid=(b, nh, nq),
        in_specs=[pl.BlockSpec((1, tq, dh), lambda bb, h, i: (bb, i, h)),
                  pl.BlockSpec((1, t, dh), lambda bb, h, i: (bb, 0, h)),
                  pl.BlockSpec((1, t, dh), lambda bb, h, i: (bb, 0, h)),
                  pl.BlockSpec((1, tq, nh), lambda bb, h, i: (bb, i, 0)),
                  pl.BlockSpec((1, 1, nq, 1, tq), lambda bb, h, i: (bb, h, 0, 0, 0))],
        out_specs=pl.BlockSpec((1, tq, dh), lambda bb, h, i: (bb, i, h)),
        out_shape=jax.ShapeDtypeStruct((b, t, nh * dh), BF16),
        compiler_params=_params("arbitrary", "arbitrary", "arbitrary"),
        name="fox_prompt",
    )(q, k, v, cum_col, cum_row)


def _fox_s_kernel(q_ref, kc_ref, vc_ref, kn_ref, vn_ref, cc_ref, cr_ref, o_ref, *, past, s_len):
    h = pl.program_id(1)
    q = q_ref[0]
    kc = kc_ref[0].astype(BF16)
    vc = vc_ref[0].astype(BF16)
    cr = cr_ref[0, 0]
    cq = _head_column(cc_ref[0][:s_len], h)
    s1 = _qk(q, kc) + (cq - cr[:, :past])
    s2 = _qk(q, kn_ref[0]) + (cq - cr[:, past:past + s_len])
    r = lax.broadcasted_iota(jnp.int32, s2.shape, 0)
    c = lax.broadcasted_iota(jnp.int32, s2.shape, 1)
    s2 = jnp.where(c <= r, s2, NEG)
    m = jnp.maximum(jnp.max(s1, axis=1, keepdims=True), jnp.max(s2, axis=1, keepdims=True))
    p1 = jnp.exp(s1 - m)
    p2 = jnp.exp(s2 - m)
    l = jnp.sum(p1, axis=1, keepdims=True) + jnp.sum(p2, axis=1, keepdims=True)
    o = (jnp.dot(p1.astype(BF16), vc, preferred_element_type=F32)
         + jnp.dot(p2.astype(BF16), vn_ref[0], preferred_element_type=F32))
    o_ref[0] = (o / l).astype(o_ref.dtype)


def _fox_sample(q, kc, vc, kn, vn, cum_col, cum_row, nh, dh):
    b, s_len, _ = q.shape
    past = kc.shape[1]
    tp = cum_col.shape[1]
    assert past % CUM_BLK == 0 and tp - past == CUM_BLK
    new = pl.BlockSpec((1, s_len, dh), lambda bb, h: (bb, 0, h))
    cache = pl.BlockSpec((1, past, dh), lambda bb, h: (bb, 0, h))
    return pl.pallas_call(
        functools.partial(_fox_s_kernel, past=past, s_len=s_len),
        grid=(b, nh),
        in_specs=[new, cache, cache, new, new,
                  pl.BlockSpec((1, CUM_BLK, nh), lambda bb, h: (bb, past // CUM_BLK, 0)),
                  pl.BlockSpec((1, 1, 1, tp), lambda bb, h: (bb, h, 0, 0))],
        out_specs=new,
        out_shape=jax.ShapeDtypeStruct((b, s_len, nh * dh), BF16),
        compiler_params=_params("arbitrary", "arbitrary"),
        name="fox_sample",
    )(q, kc, vc, kn, vn, cum_col, cum_row)


def _band_bias_kernel(gf_ref, o_ref):
    f8 = jnp.broadcast_to(gf_ref[0], (8, BAND_ROLL))
    sub = lax.broadcasted_iota(jnp.int32, (8, BAND_ROLL), 0)
    base = f8
    for bb in range(1, 8):
        base = jnp.where(sub == bb, pltpu.roll(f8, bb, axis=1), base)
    row8 = lax.broadcasted_iota(jnp.int32, (8, BAND_EXT), 0)
    col = lax.broadcasted_iota(jnp.int32, (8, BAND_EXT), 1)
    key_chunk = col // CHUNK - LEFT_CHUNKS
    for a in range(BAND_TQ // 8):
        rows = base if a == 0 else pltpu.roll(base, 8 * a, axis=1)
        rows = rows[:, :BAND_EXT]
        cd = (row8 + 8 * a) // CHUNK - key_chunk
        rows = jnp.where((cd >= 0) & (cd <= LEFT_CHUNKS), rows, NEG)
        for w in range(3):
            off = LEFT - w * BAND_TQ
            o_ref[0, w, 8 * a:8 * a + 8, :] = rows[:, off:off + BAND_WIN]


def _band_bias(rel_table):
    nh = rel_table.shape[0]
    u = jnp.arange(BAND_ROLL)
    v = jnp.where(u < BAND_ROLL - BAND_TQ, u, u - BAND_ROLL)
    idx = jnp.clip(LEFT - v, -MAX_REL, MAX_REL) + MAX_REL
    gf = jnp.take(rel_table.astype(F32), idx, axis=1).reshape(nh, 1, BAND_ROLL)
    return pl.pallas_call(
        _band_bias_kernel,
        grid=(nh,),
        in_specs=[pl.BlockSpec((1, 1, BAND_ROLL), lambda h: (h, 0, 0))],
        out_specs=pl.BlockSpec((1, 3, BAND_TQ, BAND_WIN), lambda h: (h, 0, 0, 0)),
        out_shape=jax.ShapeDtypeStruct((nh, 3, BAND_TQ, BAND_WIN), F32),
        compiler_params=_params("arbitrary"),
        name="band_bias",
    )(gf)


def _softmax_pv(parts):
    m = functools.reduce(jnp.maximum, [jnp.max(s, axis=1, keepdims=True) for s, _ in parts])
    ps = [jnp.exp(s - m) for s, _ in parts]
    l = functools.reduce(jnp.add, [jnp.sum(p, axis=1, keepdims=True) for p in ps])
    o = functools.reduce(jnp.add, [jnp.dot(p.astype(BF16), v, preferred_element_type=F32)
                                   for p, (_, v) in zip(ps, parts)])
    return o / l


def _band_kernel(q_ref, k_ref, v_ref, e_ref, o_ref):
    g = pl.program_id(2)
    ws = pl.multiple_of(jnp.maximum(g * BAND_TQ - LEFT, 0), BAND_TQ)
    w = jnp.minimum(g, LEFT // BAND_TQ)
    k = k_ref[0, pl.ds(ws, BAND_WIN), :]
    v = v_ref[0, pl.ds(ws, BAND_WIN), :]
    s = _qk(q_ref[0], k) + e_ref[0, w]
    o_ref[0] = _softmax_pv([(s, v)]).astype(o_ref.dtype)


def _band_prompt(q, k, v, bias, nh, dh):
    b, t, _ = q.shape
    assert t % BAND_TQ == 0 and t >= BAND_WIN
    return pl.pallas_call(
        _band_kernel,
        grid=(nh, b, t // BAND_TQ),
        in_specs=[pl.BlockSpec((1, BAND_TQ, dh), lambda h, bb, g: (bb, g, h)),
                  pl.BlockSpec((1, t, dh), lambda h, bb, g: (bb, 0, h)),
                  pl.BlockSpec((1, t, dh), lambda h, bb, g: (bb, 0, h)),
                  pl.BlockSpec((1, 3, BAND_TQ, BAND_WIN), lambda h, bb, g: (h, 0, 0, 0))],
        out_specs=pl.BlockSpec((1, BAND_TQ, dh), lambda h, bb, g: (bb, g, h)),
        out_shape=jax.ShapeDtypeStruct((b, t, nh * dh), BF16),
        compiler_params=_params("arbitrary", "arbitrary", "arbitrary"),
        name="band_prompt",
    )(q, k, v, bias)


def _band_s_kernel(q_ref, kc_ref, vc_ref, kn_ref, vn_ref, e_ref, o_ref, *, keep, s_len):
    q = q_ref[0]
    e = e_ref[0, 0]
    s1 = _qk(q, kc_ref[0].astype(BF16)) + e[:s_len, :keep]
    s2 = _qk(q, kn_ref[0]) + e[:s_len, keep:keep + s_len]
    o = _softmax_pv([(s1, vc_ref[0].astype(BF16)), (s2, vn_ref[0])])
    o_ref[0] = o.astype(o_ref.dtype)


def _band_sample(q, kc, vc, kn, vn, bias, nh, dh):
    b, s_len, _ = q.shape
    keep = kc.shape[1]
    assert keep == LEFT and s_len <= CHUNK
    new = pl.BlockSpec((1, s_len, dh), lambda bb, h: (bb, 0, h))
    cache = pl.BlockSpec((1, keep, dh), lambda bb, h: (bb, 0, h))
    return pl.pallas_call(
        functools.partial(_band_s_kernel, keep=keep, s_len=s_len),
        grid=(b, nh),
        in_specs=[new, cache, cache, new, new,
                  pl.BlockSpec((1, 1, BAND_TQ, BAND_WIN), lambda bb, h: (h, LEFT // BAND_TQ, 0, 0))],
        out_specs=new,
        out_shape=jax.ShapeDtypeStruct((b, s_len, nh * dh), BF16),
        compiler_params=_params("arbitrary", "arbitrary"),
        name="band_sample",
    )(q, kc, vc, kn, vn, bias)


def _oproj_kernel(oa_ref, ob_ref, wa_ref, wb_ref, x_ref, gate_ref, o_ref):
    acc = (jnp.dot(oa_ref[...], wa_ref[...], preferred_element_type=F32)
           + jnp.dot(ob_ref[...], wb_ref[...], preferred_element_type=F32))
    o_ref[...] = x_ref[...] + gate_ref[...] * acc


def _gate_spec(gate, m, tm, tn):
    if gate.ndim == 2:
        return pl.BlockSpec((tm, tn), lambda i, j: (i, j))
    tiles_per_seq = m // gate.shape[0] // tm
    return pl.BlockSpec((None, 1, tn), lambda i, j: (i // tiles_per_seq, 0, j))


def _oproj(oa, ob, w_o, x, gate, tm, tn):
    m, da = oa.shape
    db = ob.shape[1]
    d = w_o.shape[1]
    tm = min(tm, m)
    tn = _tile(d, tn)
    assert da == db
    return pl.pallas_call(
        _oproj_kernel,
        grid=(m // tm, d // tn),
        in_specs=[pl.BlockSpec((tm, da), lambda i, j: (i, 0)),
                  pl.BlockSpec((tm, db), lambda i, j: (i, 0)),
                  pl.BlockSpec((da, tn), lambda i, j: (0, j)),
                  pl.BlockSpec((db, tn), lambda i, j: (1, j)),
                  pl.BlockSpec((tm, tn), lambda i, j: (i, j)),
                  _gate_spec(gate, m, tm, tn)],
        out_specs=pl.BlockSpec((tm, tn), lambda i, j: (i, j)),
        out_shape=jax.ShapeDtypeStruct((m, d), F32),
        compiler_params=_params("arbitrary", "arbitrary"),
        name="out_proj",
    )(oa, ob, w_o, w_o, x, gate)


def _ffn_up_kernel(a_ref, wg_ref, wu_ref, o_ref):
    a = a_ref[...]
    g = jnp.dot(a, wg_ref[...], preferred_element_type=F32)
    u = jnp.dot(a, wu_ref[...], preferred_element_type=F32)
    o_ref[...] = (g * jax.nn.sigmoid(g) * u).astype(o_ref.dtype)


def _ffn_up(a, w_gu, tm, tn):
    m, k = a.shape
    f = w_gu.shape[1] // 2
    tm = min(tm, m)
    tn = _tile(f, tn)
    nj = f // tn
    return pl.pallas_call(
        _ffn_up_kernel,
        grid=(m // tm, nj),
        in_specs=[pl.BlockSpec((tm, k), lambda i, j: (i, 0)),
                  pl.BlockSpec((k, tn), lambda i, j: (0, j)),
                  pl.BlockSpec((k, tn), lambda i, j: (0, nj + j))],
        out_specs=pl.BlockSpec((tm, tn), lambda i, j: (i, j)),
        out_shape=jax.ShapeDtypeStruct((m, f), BF16),
        compiler_params=_params("arbitrary", "arbitrary"),
        name="ffn_up",
    )(a, w_gu, w_gu)


def _ffn_down_kernel(a_ref, w_ref, x_ref, gate_ref, o_ref):
    acc = jnp.dot(a_ref[...], w_ref[...], preferred_element_type=F32)
    o_ref[...] = x_ref[...] + gate_ref[...] * acc


def _ffn_down(a, w_down, x, gate, tm, tn):
    m, f = a.shape
    d = w_down.shape[1]
    tm = min(tm, m)
    tn = _tile(d, tn)
    return pl.pallas_call(
        _ffn_down_kernel,
        grid=(m // tm, d // tn),
        in_specs=[pl.BlockSpec((tm, f), lambda i, j: (i, 0)),
                  pl.BlockSpec((f, tn), lambda i, j: (0, j)),
                  pl.BlockSpec((tm, tn), lambda i, j: (i, j)),
                  _gate_spec(gate, m, tm, tn)],
        out_specs=pl.BlockSpec((tm, tn), lambda i, j: (i, j)),
        out_shape=jax.ShapeDtypeStruct((m, d), F32),
        compiler_params=_params("arbitrary", "arbitrary"),
        name="ffn_down",
    )(a, w_down, x, gate)


def _layer(x, mod, cache, w, bias, *, nb, tr):
    b, t, d = x.shape
    m = b * t
    shift_a, scale_a, gate_a, shift_f, scale_f, gate_f = mod
    dh = w["g_q_a"].shape[0]
    nh_a = w["b_f"].shape[0]
    d_a = nh_a * dh
    d_b = (w["w_qkv"].shape[1] - 3 * d_a) // 3
    nh_b = d_b // dh
    scale = dh ** -0.5
    per_row = cache is not None
    tm = 1024

    def gates(gt):
        if per_row:
            return jnp.broadcast_to(gt[:, None, :], (b, t, d)).reshape(m, d)
        return gt.reshape(b, 1, d)

    h = _modnorm(x, scale_a, shift_a, w["g_attn"], nb, tr).reshape(m, d)
    pj = functools.partial(_proj, h, w["w_qkv"], dh=dh, tm=tm, tn=1024)
    (q_a,) = pj(0, d_a, w["g_q_a"] * scale, [BF16], do_rms=True)
    k_a32, k_a = pj(d_a, d_a, w["g_k_a"], [F32, BF16], do_rms=True)
    v_a32, v_a = pj(2 * d_a, d_a, w["g_k_a"], [F32, BF16], do_rms=False)
    (q_b,) = pj(3 * d_a, d_b, w["g_q_b"] * scale, [BF16], do_rms=True)
    k_b32, k_b = pj(3 * d_a + d_b, d_b, w["g_k_b"], [F32, BF16], do_rms=True)
    v_b32, v_b = pj(3 * d_a + 2 * d_b, d_b, w["g_k_b"], [F32, BF16], do_rms=False)
    logf = _forget_gate(h, w["w_f"], w["b_f"], tm).reshape(b, t, nh_a)

    r3 = lambda z: z.reshape(b, t, -1)
    if cache is None:
        cum_t = _cumsum_t(logf.transpose(0, 2, 1))
        tq = 256
        o_a = _fox_prompt(r3(q_a), r3(k_a), r3(v_a), cum_t.transpose(0, 2, 1),
                          cum_t.reshape(b, nh_a, t // tq, 1, tq), nh_a, dh, tq)
        o_b = _band_prompt(r3(q_b), r3(k_b), r3(v_b), bias, nh_b, dh)
        keep = min(LEFT, t)
        new_k_b, new_v_b = r3(k_b32)[:, t - keep:], r3(v_b32)[:, t - keep:]
    else:
        ck_a, cv_a, clogf_a, ck_b, cv_b = cache
        past = ck_a.shape[1]
        tp = past + CUM_BLK
        lf_all = jnp.concatenate([clogf_a.astype(F32), logf, jnp.zeros((b, tp - past - t, nh_a), F32)], axis=1)
        cum_t = _cumsum_t(lf_all.transpose(0, 2, 1))
        o_a = _fox_sample(r3(q_a), ck_a.reshape(b, past, d_a), cv_a.reshape(b, past, d_a), r3(k_a), r3(v_a),
                          cum_t.transpose(0, 2, 1), cum_t.reshape(b, nh_a, 1, tp), nh_a, dh)
        keep = ck_b.shape[1]
        o_b = _band_sample(r3(q_b), ck_b.reshape(b, keep, d_b), cv_b.reshape(b, keep, d_b), r3(k_b), r3(v_b),
                           bias, nh_b, dh)
        new_k_b = jnp.concatenate([ck_b.reshape(b, keep, d_b), r3(k_b32).astype(ck_b.dtype)], axis=1)[:, -keep:]
        new_v_b = jnp.concatenate([cv_b.reshape(b, keep, d_b), r3(v_b32).astype(cv_b.dtype)], axis=1)[:, -keep:]

    x1 = _oproj(o_a.reshape(m, d_a), o_b.reshape(m, d_b), w["w_o"], x.reshape(m, d), gates(gate_a), tm, 1024)
    h2 = _modnorm(x1.reshape(b, t, d), scale_f, shift_f, w["g_ffn"], nb, tr).reshape(m, d)
    act = _ffn_up(h2, w["w_gu"], tm, 256)
    y = _ffn_down(act, w["w_down"], x1, gates(gate_f), 512, 512)
    return (y.reshape(b, t, d), k_a32.reshape(b, t, nh_a, dh), v_a32.reshape(b, t, nh_a, dh), logf,
            new_k_b.reshape(b, -1, nh_b, dh), new_v_b.reshape(b, -1, nh_b, dh))


def kernel(x_prompt, x_sample, cache_k_a, cache_v_a, cache_logf_a, cache_k_b, cache_v_b, c_prompt, c_sample,
           w_ada, b_ada, g_attn, g_ffn, w_in, b_f, g_q_a, g_k_a, g_q_b, g_k_b, rel_table, w_o, w_gu, w_down):
    depth = w_ada.shape[0]
    d = x_prompt.shape[-1]
    nb_p, nb_s = c_prompt.shape[0], c_sample.shape[0]
    nh_a = b_f.shape[1]
    n_qkv = w_in.shape[2] - nh_a
    y_p, y_s = x_prompt, x_sample
    outs_p, outs_s = [], []
    for l in range(depth):
        w_f = jnp.pad(w_in[l][:, n_qkv:], ((0, 0), (0, LANES - nh_a))).astype(BF16)
        w = dict(w_qkv=w_in[l][:, :n_qkv].astype(BF16), w_f=w_f, b_f=b_f[l], g_attn=g_attn[l], g_ffn=g_ffn[l],
                 g_q_a=g_q_a[l], g_k_a=g_k_a[l], g_q_b=g_q_b[l], g_k_b=g_k_b[l],
                 w_o=w_o[l].astype(BF16), w_gu=w_gu[l].astype(BF16), w_down=w_down[l].astype(BF16))
        c_all = jnp.concatenate([c_prompt, c_sample], axis=0)
        pad = (-c_all.shape[0]) % 16
        mod = _ada(jnp.pad(c_all, ((0, pad), (0, 0))), w_ada[l], b_ada[l])
        mod_p = [mod[:nb_p, i * d:(i + 1) * d] for i in range(6)]
        mod_s = [mod[nb_p:nb_p + nb_s, i * d:(i + 1) * d] for i in range(6)]
        bias = _band_bias(rel_table[l])
        y_p, *rest_p = _layer(y_p, mod_p, None, w, bias, nb=1, tr=min(512, x_prompt.shape[1]))
        cache = (cache_k_a[l], cache_v_a[l], cache_logf_a[l], cache_k_b[l], cache_v_b[l])
        y_s, *rest_s = _layer(y_s, mod_s, cache, w, bias, nb=nb_s, tr=x_sample.shape[1])
        outs_p.append(rest_p)
        outs_s.append(rest_s)
    stack = lambda outs, i: jnp.stack([o[i] for o in outs])
    return (y_p, y_s, *[stack(outs_p, i) for i in range(5)], *[stack(outs_s, i) for i in range(5)])
```

```python
import functools

import jax
import jax.numpy as jnp
from jax import lax
from jax.experimental import pallas as pl
from jax.experimental.pallas import tpu as pltpu

CHUNK = 64
LEFT_CHUNKS = 8
LEFT = LEFT_CHUNKS * CHUNK
MAX_REL = 128
EPS = 1e-6
NEG = -1e30

LANES = 128
BAND_TQ = 256
BAND_WIN = LEFT + BAND_TQ
BAND_EXT = BAND_WIN + LEFT
BAND_ROLL = 2048
CUM_BLK = 256
VMEM_LIMIT_BYTES = 56 * 1024 * 1024

F32 = jnp.float32
BF16 = jnp.bfloat16


def _params(*semantics):
    return pltpu.CompilerParams(dimension_semantics=semantics, vmem_limit_bytes=VMEM_LIMIT_BYTES)


def _tile(n, want):
    if n <= want:
        return n
    t = (want // LANES) * LANES
    while t >= LANES:
        if n % t == 0:
            return t
        t -= LANES
    raise ValueError(f"no lane-aligned tile for {n}")


def _ada_kernel(c_ref, w_ref, b_ref, o_ref):
    c = c_ref[...]
    a = (c * jax.nn.sigmoid(c)).astype(BF16)
    o_ref[...] = jnp.dot(a, w_ref[...].astype(BF16), preferred_element_type=F32) + b_ref[...]


def _ada(c, w_ada, b_ada):
    m, d = c.shape
    n = w_ada.shape[1]
    tn = _tile(n, 512)
    return pl.pallas_call(
        _ada_kernel,
        grid=(n // tn,),
        in_specs=[pl.BlockSpec((m, d), lambda j: (0, 0)),
                  pl.BlockSpec((d, tn), lambda j: (0, j)),
                  pl.BlockSpec((1, tn), lambda j: (0, j))],
        out_specs=pl.BlockSpec((m, tn), lambda j: (0, j)),
        out_shape=jax.ShapeDtypeStruct((m, n), F32),
        compiler_params=_params("arbitrary"),
        name="ada_mod",
    )(c, w_ada, b_ada.reshape(1, n))


def _modnorm_kernel(x_ref, sc_ref, sh_ref, g_ref, o_ref):
    x = x_ref[...]
    ms = jnp.mean(x * x, axis=-1, keepdims=True)
    y = x * lax.rsqrt(ms + EPS) * g_ref[...]
    o_ref[...] = (y * (1.0 + sc_ref[...]) + sh_ref[...]).astype(o_ref.dtype)


def _modnorm(x, scale, shift, g, nb, tr):
    b, t, d = x.shape
    bs = pl.BlockSpec((nb, 1, d), lambda i, j: (i, 0, 0))
    return pl.pallas_call(
        _modnorm_kernel,
        grid=(b // nb, t // tr),
        in_specs=[pl.BlockSpec((nb, tr, d), lambda i, j: (i, j, 0)), bs, bs,
                  pl.BlockSpec((1, 1, d), lambda i, j: (0, 0, 0))],
        out_specs=pl.BlockSpec((nb, tr, d), lambda i, j: (i, j, 0)),
        out_shape=jax.ShapeDtypeStruct((b, t, d), BF16),
        compiler_params=_params("arbitrary", "arbitrary"),
        name="modnorm",
    )(x, scale.reshape(b, 1, d), shift.reshape(b, 1, d), g.reshape(1, 1, d))


def _proj_kernel(a_ref, w_ref, g_ref, *o_refs, do_rms, dh, nh, rows32, tiles_per_seq):
    acc = jnp.dot(a_ref[...], w_ref[...], preferred_element_type=F32)
    o16_ref = o_refs[-1]
    tm, tn = acc.shape
    hpt = tn // dh
    g = g_ref[...]
    ys = []
    for hh in range(hpt):
        y = acc[:, hh * dh:(hh + 1) * dh]
        if do_rms:
            ms = jnp.mean(y * y, axis=-1, keepdims=True)
            y = y * lax.rsqrt(ms + EPS) * g
        o16_ref[:, hh * dh:(hh + 1) * dh] = y.astype(o16_ref.dtype)
        ys.append(y)
    if rows32:
        o32_ref = o_refs[0]
        h0 = pl.program_id(1) * hpt

        def store32():
            for hh in range(hpt):
                o32_ref[pl.ds(h0 + hh, rows32, stride=nh), :] = ys[hh][tm - rows32:, :]

        if rows32 == tm:
            store32()
        else:
            pl.when(pl.program_id(0) % tiles_per_seq == tiles_per_seq - 1)(store32)


def _proj(a, w, col0, n, g, *, do_rms, dh, tm, tn, f32_rows=0, seq_len=None):
    m, k = a.shape
    tm = min(tm, m)
    tn = _tile(n, tn)
    nh = n // dh
    j0 = col0 // tn
    assert col0 % tn == 0 and m % tm == 0
    out_specs = [pl.BlockSpec((tm, tn), lambda i, j: (i, j))]
    out_shape = [jax.ShapeDtypeStruct((m, n), BF16)]
    rows32, tps = 0, 1
    if f32_rows:
        tps = seq_len // tm
        assert seq_len % tm == 0
        if f32_rows == seq_len:
            rows32 = tm
            out_specs.insert(0, pl.BlockSpec((tm * nh, dh), lambda i, j: (i, 0)))
            out_shape.insert(0, jax.ShapeDtypeStruct((m * nh, dh), F32))
        else:
            rows32 = f32_rows
            assert rows32 <= tm
            out_specs.insert(0, pl.BlockSpec((rows32 * nh, dh), lambda i, j: (i // tps, 0)))
            out_shape.insert(0, jax.ShapeDtypeStruct((m // seq_len * rows32 * nh, dh), F32))
    return pl.pallas_call(
        functools.partial(_proj_kernel, do_rms=do_rms, dh=dh, nh=nh, rows32=rows32, tiles_per_seq=tps),
        grid=(m // tm, n // tn),
        in_specs=[pl.BlockSpec((tm, k), lambda i, j: (i, 0)),
                  pl.BlockSpec((k, tn), lambda i, j: (0, j0 + j)),
                  pl.BlockSpec((1, dh), lambda i, j: (0, 0))],
        out_specs=out_specs,
        out_shape=out_shape,
        compiler_params=_params("arbitrary", "arbitrary"),
        name="proj",
    )(a, w, g.reshape(1, dh).astype(F32))


def _gate_kernel(a_ref, w_ref, b_ref, o_ref):
    z = jnp.dot(a_ref[...], w_ref[...], preferred_element_type=F32)
    z = z[:, :o_ref.shape[1]] + b_ref[...]
    o_ref[...] = jnp.minimum(z, 0.0) - jnp.log1p(jnp.exp(-jnp.abs(z)))


def _forget_gate(a, w_f, b_f, tm):
    m, k = a.shape
    nh = b_f.shape[0]
    tm = min(tm, m)
    return pl.pallas_call(
        _gate_kernel,
        grid=(m // tm,),
        in_specs=[pl.BlockSpec((tm, k), lambda i: (i, 0)),
                  pl.BlockSpec((k, LANES), lambda i: (0, 0)),
                  pl.BlockSpec((1, nh), lambda i: (0, 0))],
        out_specs=pl.BlockSpec((tm, nh), lambda i: (i, 0)),
        out_shape=jax.ShapeDtypeStruct((m, nh), F32),
        compiler_params=_params("arbitrary"),
        name="forget_gate",
    )(a, w_f, b_f.reshape(1, nh).astype(F32))


def _cumsum_kernel(x_ref, o_ref):
    h, tp = x_ref.shape[1], x_ref.shape[2]
    r = lax.broadcasted_iota(jnp.int32, (CUM_BLK, CUM_BLK), 0)
    c = lax.broadcasted_iota(jnp.int32, (CUM_BLK, CUM_BLK), 1)
    tri = (r <= c).astype(BF16)
    carry = jnp.zeros((h, 1), F32)
    for blk in range(tp // CUM_BLK):
        x = x_ref[0, :, blk * CUM_BLK:(blk + 1) * CUM_BLK]
        hi = x.astype(BF16)
        r1 = x - hi.astype(F32)
        mid = r1.astype(BF16)
        lo = (r1 - mid.astype(F32)).astype(BF16)
        cs = (jnp.dot(hi, tri, preferred_element_type=F32)
              + jnp.dot(mid, tri, preferred_element_type=F32)
              + jnp.dot(lo, tri, preferred_element_type=F32)) + carry
        o_ref[0, :, blk * CUM_BLK:(blk + 1) * CUM_BLK] = cs
        carry = cs[:, CUM_BLK - 1:CUM_BLK]


def _cumsum_t(logf_t):
    b, h, tp = logf_t.shape
    return pl.pallas_call(
        _cumsum_kernel,
        grid=(b,),
        in_specs=[pl.BlockSpec((1, h, tp), lambda i: (i, 0, 0))],
        out_specs=pl.BlockSpec((1, h, tp), lambda i: (i, 0, 0)),
        out_shape=jax.ShapeDtypeStruct((b, h, tp), F32),
        compiler_params=_params("arbitrary"),
        name="logf_cumsum",
    )(logf_t)


def _head_column(cc, h):
    lane = lax.broadcasted_iota(jnp.int32, cc.shape, 1)
    return jnp.sum(jnp.where(lane == h, cc, 0.0), axis=1, keepdims=True)


def _qk(q, k):
    return lax.dot_general(q, k, (((1,), (1,)), ((), ())), preferred_element_type=F32)


def _fox_kernel(q_ref, k_ref, v_ref, cc_ref, cr_ref, o_ref, *, tq, hb, dh):
    hg = pl.program_id(1)
    i = pl.program_id(2)
    cc = cc_ref[0]
    heads = range(hb)
    qs = [q_ref[0, :, hh * dh:(hh + 1) * dh] for hh in heads]
    cqs = [_head_column(cc, hg * hb + hh) for hh in heads]

    def scores(kb, hh):
        off = pl.multiple_of(kb * tq, tq)
        k = k_ref[0, pl.ds(off, tq), hh * dh:(hh + 1) * dh]
        v = v_ref[0, pl.ds(off, tq), hh * dh:(hh + 1) * dh]
        s = _qk(qs[hh], k) + (cqs[hh] - cr_ref[0, hh, kb])
        return s, v

    def step(kb, carries, mask=None):
        svs = [scores(kb, hh) for hh in heads]
        stats = []
        for (s, _), (m, l, _) in zip(svs, carries):
            if mask is not None:
                s = jnp.where(mask, s, NEG)
            m_new = jnp.maximum(m, jnp.max(s, axis=1, keepdims=True))
            alpha = jnp.exp(m - m_new)
            p = jnp.exp(s - m_new)
            stats.append((m_new, alpha * l + jnp.sum(p, axis=1, keepdims=True), alpha, p.astype(BF16)))
        return tuple((m_new, l, alpha * acc + jnp.dot(p, v, preferred_element_type=F32))
                     for (m_new, l, alpha, p), (_, v), (_, _, acc) in zip(stats, svs, carries))

    init = tuple((jnp.full((tq, 1), NEG, F32), jnp.zeros((tq, 1), F32), jnp.zeros((tq, dh), F32)) for _ in heads)
    carries = lax.fori_loop(0, i, step, init)
    r = lax.broadcasted_iota(jnp.int32, (tq, tq), 0)
    c = lax.broadcasted_iota(jnp.int32, (tq, tq), 1)
    carries = step(i, carries, mask=c <= r)
    for hh, (_, l, acc) in enumerate(carries):
        o_ref[0, :, hh * dh:(hh + 1) * dh] = (acc / l).astype(o_ref.dtype)


def _fox_prompt(q, k, v, cum_col, cum_row, nh, dh, tq, hb):
    b, t, _ = q.shape
    nq = t // tq
    hb = min(hb, nh)
    qspec = pl.BlockSpec((1, tq, hb * dh), lambda bb, h, i: (bb, i, h))
    kspec = pl.BlockSpec((1, t, hb * dh), lambda bb, h, i: (bb, 0, h))
    return pl.pallas_call(
        functools.partial(_fox_kernel, tq=tq, hb=hb, dh=dh),
        grid=(b, nh // hb, nq),
        in_specs=[qspec, kspec, kspec,
                  pl.BlockSpec((1, tq, nh), lambda bb, h, i: (bb, i, 0)),
                  pl.BlockSpec((1, hb, nq, 1, tq), lambda bb, h, i: (bb, h, 0, 0, 0))],
        out_specs=qspec,
        out_shape=jax.ShapeDtypeStruct((b, t, nh * dh), BF16),
        compiler_params=_params("arbitrary", "arbitrary", "arbitrary"),
        name="fox_prompt",
    )(q, k, v, cum_col, cum_row)


def _head_rows(ref, h, n, nh):
    return ref[0, pl.ds(h, n, stride=nh), :].astype(BF16)


def _fox_s_kernel(q_ref, kc_ref, vc_ref, kn_ref, vn_ref, cc_ref, crc_ref, crn_ref, o_ref, m_sc, l_sc, acc_sc,
                  *, nh, dh, tc, s_len):
    c = pl.program_id(1)
    cc = cc_ref[0][:s_len]
    r = lax.broadcasted_iota(jnp.int32, (s_len, s_len), 0)
    col = lax.broadcasted_iota(jnp.int32, (s_len, s_len), 1)

    @pl.when(c == 0)
    def _():
        hcols = [slice(h * dh, (h + 1) * dh) for h in range(nh)]
        ss = [_qk(q_ref[0, :, cols], kn_ref[0, :, cols]) + (cc[:, h:h + 1] - crn_ref[0, h:h + 1, :s_len])
              for h, cols in enumerate(hcols)]
        ps = []
        for h, s in enumerate(ss):
            s = jnp.where(col <= r, s, NEG)
            m = jnp.max(s, axis=1, keepdims=True)
            p = jnp.exp(s - m)
            m_sc[h] = m
            l_sc[h] = jnp.sum(p, axis=1, keepdims=True)
            ps.append(p.astype(BF16))
        for h, p in enumerate(ps):
            acc_sc[h] = jnp.dot(p, vn_ref[0, :, hcols[h]], preferred_element_type=F32)

    ss = [_qk(q_ref[0, :, h * dh:(h + 1) * dh], _head_rows(kc_ref, h, tc, nh))
          + (cc[:, h:h + 1] - crc_ref[0, h:h + 1, :]) for h in range(nh)]
    ps = []
    for h, s in enumerate(ss):
        m_old = m_sc[h]
        m_new = jnp.maximum(m_old, jnp.max(s, axis=1, keepdims=True))
        alpha = jnp.exp(m_old - m_new)
        p = jnp.exp(s - m_new)
        m_sc[h] = m_new
        l_sc[h] = alpha * l_sc[h] + jnp.sum(p, axis=1, keepdims=True)
        ps.append((alpha, p.astype(BF16)))
    for h, (alpha, p) in enumerate(ps):
        acc_sc[h] = alpha * acc_sc[h] + jnp.dot(p, _head_rows(vc_ref, h, tc, nh), preferred_element_type=F32)

    @pl.when(c == pl.num_programs(1) - 1)
    def _():
        for h in range(nh):
            o_ref[0, :, h * dh:(h + 1) * dh] = (acc_sc[h] / l_sc[h]).astype(o_ref.dtype)


def _fox_sample(q, kc, vc, kn, vn, cum_col, cum_t, nh, dh, tc):
    b, s_len, _ = q.shape
    past = kc.shape[1] // nh
    tp = cum_col.shape[1]
    tc = min(tc, past)
    assert past % CUM_BLK == 0 and tp - past == CUM_BLK and past % tc == 0 and s_len <= CUM_BLK
    new = pl.BlockSpec((1, s_len, nh * dh), lambda bb, c: (bb, 0, 0))
    cache = pl.BlockSpec((1, tc * nh, dh), lambda bb, c: (bb, c, 0))
    return pl.pallas_call(
        functools.partial(_fox_s_kernel, nh=nh, dh=dh, tc=tc, s_len=s_len),
        grid=(b, past // tc),
        in_specs=[new, cache, cache, new, new,
                  pl.BlockSpec((1, CUM_BLK, nh), lambda bb, c: (bb, past // CUM_BLK, 0)),
                  pl.BlockSpec((1, nh, tc), lambda bb, c: (bb, 0, c)),
                  pl.BlockSpec((1, nh, CUM_BLK), lambda bb, c: (bb, 0, past // CUM_BLK))],
        out_specs=new,
        out_shape=jax.ShapeDtypeStruct((b, s_len, nh * dh), BF16),
        scratch_shapes=[pltpu.VMEM((nh, s_len, 1), F32), pltpu.VMEM((nh, s_len, 1), F32),
                        pltpu.VMEM((nh, s_len, dh), F32)],
        compiler_params=_params("arbitrary", "arbitrary"),
        name="fox_sample",
    )(q, kc, vc, kn, vn, cum_col, cum_t, cum_t)


def _band_bias_kernel(gf_ref, o_ref):
    f8 = jnp.broadcast_to(gf_ref[0], (8, BAND_ROLL))
    sub = lax.broadcasted_iota(jnp.int32, (8, BAND_ROLL), 0)
    base = f8
    for bb in range(1, 8):
        base = jnp.where(sub == bb, pltpu.roll(f8, bb, axis=1), base)
    row8 = lax.broadcasted_iota(jnp.int32, (8, BAND_EXT), 0)
    col = lax.broadcasted_iota(jnp.int32, (8, BAND_EXT), 1)
    key_chunk = col // CHUNK - LEFT_CHUNKS
    for a in range(BAND_TQ // 8):
        rows = base if a == 0 else pltpu.roll(base, 8 * a, axis=1)
        rows = rows[:, :BAND_EXT]
        cd = (row8 + 8 * a) // CHUNK - key_chunk
        rows = jnp.where((cd >= 0) & (cd <= LEFT_CHUNKS), rows, NEG)
        for w in range(3):
            off = LEFT - w * BAND_TQ
            o_ref[0, w, 8 * a:8 * a + 8, :] = rows[:, off:off + BAND_WIN]


def _band_bias(rel_table):
    nh = rel_table.shape[0]
    u = jnp.arange(BAND_ROLL)
    v = jnp.where(u < BAND_ROLL - BAND_TQ, u, u - BAND_ROLL)
    idx = jnp.clip(LEFT - v, -MAX_REL, MAX_REL) + MAX_REL
    gf = jnp.take(rel_table.astype(F32), idx, axis=1).reshape(nh, 1, BAND_ROLL)
    return pl.pallas_call(
        _band_bias_kernel,
        grid=(nh,),
        in_specs=[pl.BlockSpec((1, 1, BAND_ROLL), lambda h: (h, 0, 0))],
        out_specs=pl.BlockSpec((1, 3, BAND_TQ, BAND_WIN), lambda h: (h, 0, 0, 0)),
        out_shape=jax.ShapeDtypeStruct((nh, 3, BAND_TQ, BAND_WIN), F32),
        compiler_params=_params("arbitrary"),
        name="band_bias",
    )(gf)


def _softmax_pv(score_parts, value_parts):
    probs = []
    for parts in score_parts:
        m = functools.reduce(jnp.maximum, [jnp.max(s, axis=1, keepdims=True) for s in parts])
        ps = [jnp.exp(s - m) for s in parts]
        l = functools.reduce(jnp.add, [jnp.sum(p, axis=1, keepdims=True) for p in ps])
        probs.append(([p.astype(BF16) for p in ps], l))
    outs = []
    for (ps, l), vals in zip(probs, value_parts):
        o = functools.reduce(jnp.add, [jnp.dot(p, v(), preferred_element_type=F32) for p, v in zip(ps, vals)])
        outs.append(o / l)
    return outs


def _band_kernel(q_ref, k_ref, v_ref, e_ref, o_ref, *, hb, dh):
    g = pl.program_id(2)
    ws = pl.multiple_of(jnp.maximum(g * BAND_TQ - LEFT, 0), BAND_TQ)
    w = jnp.minimum(g, LEFT // BAND_TQ)
    hcols = [slice(hh * dh, (hh + 1) * dh) for hh in range(hb)]
    scores = [[_qk(q_ref[0, :, cols], k_ref[0, pl.ds(ws, BAND_WIN), cols]) + e_ref[hh, w]]
              for hh, cols in enumerate(hcols)]
    values = [[functools.partial(lambda cols: v_ref[0, pl.ds(ws, BAND_WIN), cols], cols)] for cols in hcols]
    for cols, o in zip(hcols, _softmax_pv(scores, values)):
        o_ref[0, :, cols] = o.astype(o_ref.dtype)


def _band_prompt(q, k, v, bias, nh, dh, hb):
    b, t, _ = q.shape
    assert t % BAND_TQ == 0 and t >= BAND_WIN
    hb = min(hb, nh)
    qspec = pl.BlockSpec((1, BAND_TQ, hb * dh), lambda h, bb, g: (bb, g, h))
    kspec = pl.BlockSpec((1, t, hb * dh), lambda h, bb, g: (bb, 0, h))
    return pl.pallas_call(
        functools.partial(_band_kernel, hb=hb, dh=dh),
        grid=(nh // hb, b, t // BAND_TQ),
        in_specs=[qspec, kspec, kspec,
                  pl.BlockSpec((hb, 3, BAND_TQ, BAND_WIN), lambda h, bb, g: (h, 0, 0, 0),
                               pipeline_mode=pl.Buffered(1))],
        out_specs=qspec,
        out_shape=jax.ShapeDtypeStruct((b, t, nh * dh), BF16),
        compiler_params=_params("arbitrary", "arbitrary", "arbitrary"),
        name="band_prompt",
    )(q, k, v, bias)


def _band_s_kernel(q_ref, kc_ref, vc_ref, kn_ref, vn_ref, e_ref, o_ref, *, nh, dh, keep, s_len):
    hcols = [slice(h * dh, (h + 1) * dh) for h in range(nh)]
    scores, values = [], []
    for h, cols in enumerate(hcols):
        q = q_ref[0, :, cols]
        e = e_ref[h, 0]
        scores.append([_qk(q, _head_rows(kc_ref, h, keep, nh)) + e[:, :keep],
                       _qk(q, kn_ref[0, :, cols]) + e[:, keep:keep + s_len]])
        values.append([functools.partial(_head_rows, vc_ref, h, keep, nh),
                       functools.partial(lambda cols: vn_ref[0, :, cols], cols)])
    for cols, o in zip(hcols, _softmax_pv(scores, values)):
        o_ref[0, :, cols] = o.astype(o_ref.dtype)


def _band_sample(q, kc, vc, kn, vn, bias, nh, dh):
    b, s_len, _ = q.shape
    keep = kc.shape[1] // nh
    assert keep == LEFT and s_len <= CHUNK and s_len % 8 == 0
    new = pl.BlockSpec((1, s_len, nh * dh), lambda bb: (bb, 0, 0))
    cache = pl.BlockSpec((1, keep * nh, dh), lambda bb: (bb, 0, 0))
    return pl.pallas_call(
        functools.partial(_band_s_kernel, nh=nh, dh=dh, keep=keep, s_len=s_len),
        grid=(b,),
        in_specs=[new, cache, cache, new, new,
                  pl.BlockSpec((nh, 1, s_len, BAND_WIN), lambda bb: (0, LEFT // BAND_TQ, 0, 0))],
        out_specs=new,
        out_shape=jax.ShapeDtypeStruct((b, s_len, nh * dh), BF16),
        compiler_params=_params("arbitrary"),
        name="band_sample",
    )(q, kc, vc, kn, vn, bias)


def _oproj_kernel(oa_ref, ob_ref, wa_ref, wb_ref, x_ref, gate_ref, o_ref):
    acc = (jnp.dot(oa_ref[...], wa_ref[...], preferred_element_type=F32)
           + jnp.dot(ob_ref[...], wb_ref[...], preferred_element_type=F32))
    o_ref[...] = x_ref[...] + gate_ref[...] * acc


def _gate_spec(gate, m, tm, tn):
    if gate.ndim == 2:
        return pl.BlockSpec((tm, tn), lambda i, j: (i, j))
    tiles_per_seq = m // gate.shape[0] // tm
    return pl.BlockSpec((None, 1, tn), lambda i, j: (i // tiles_per_seq, 0, j))


def _oproj(oa, ob, w_o, x, gate, tm, tn):
    m, da = oa.shape
    db = ob.shape[1]
    d = w_o.shape[1]
    tm = min(tm, m)
    tn = _tile(d, tn)
    assert da == db
    return pl.pallas_call(
        _oproj_kernel,
        grid=(m // tm, d // tn),
        in_specs=[pl.BlockSpec((tm, da), lambda i, j: (i, 0)),
                  pl.BlockSpec((tm, db), lambda i, j: (i, 0)),
                  pl.BlockSpec((da, tn), lambda i, j: (0, j)),
                  pl.BlockSpec((db, tn), lambda i, j: (1, j)),
                  pl.BlockSpec((tm, tn), lambda i, j: (i, j)),
                  _gate_spec(gate, m, tm, tn)],
        out_specs=pl.BlockSpec((tm, tn), lambda i, j: (i, j)),
        out_shape=jax.ShapeDtypeStruct((m, d), F32),
        compiler_params=_params("arbitrary", "arbitrary"),
        name="out_proj",
    )(oa, ob, w_o, w_o, x, gate)


def _ffn_up_kernel(a_ref, wg_ref, wu_ref, o_ref):
    a = a_ref[...]
    g = jnp.dot(a, wg_ref[...], preferred_element_type=F32)
    u = jnp.dot(a, wu_ref[...], preferred_element_type=F32)
    o_ref[...] = (g * jax.nn.sigmoid(g) * u).astype(o_ref.dtype)


def _ffn_up(a, w_gu, tm, tn):
    m, k = a.shape
    f = w_gu.shape[1] // 2
    tm = min(tm, m)
    tn = _tile(f, tn)
    nj = f // tn
    return pl.pallas_call(
        _ffn_up_kernel,
        grid=(m // tm, nj),
        in_specs=[pl.BlockSpec((tm, k), lambda i, j: (i, 0)),
                  pl.BlockSpec((k, tn), lambda i, j: (0, j)),
                  pl.BlockSpec((k, tn), lambda i, j: (0, nj + j))],
        out_specs=pl.BlockSpec((tm, tn), lambda i, j: (i, j)),
        out_shape=jax.ShapeDtypeStruct((m, f), BF16),
        compiler_params=_params("arbitrary", "arbitrary"),
        name="ffn_up",
    )(a, w_gu, w_gu)


def _ffn_down_kernel(a_ref, w_ref, x_ref, gate_ref, o_ref):
    acc = jnp.dot(a_ref[...], w_ref[...], preferred_element_type=F32)
    o_ref[...] = x_ref[...] + gate_ref[...] * acc


def _ffn_down(a, w_down, x, gate, tm, tn):
    m, f = a.shape
    d = w_down.shape[1]
    tm = min(tm, m)
    tn = _tile(d, tn)
    return pl.pallas_call(
        _ffn_down_kernel,
        grid=(m // tm, d // tn),
        in_specs=[pl.BlockSpec((tm, f), lambda i, j: (i, 0)),
                  pl.BlockSpec((f, tn), lambda i, j: (0, j)),
                  pl.BlockSpec((tm, tn), lambda i, j: (i, j)),
                  _gate_spec(gate, m, tm, tn)],
        out_specs=pl.BlockSpec((tm, tn), lambda i, j: (i, j)),
        out_shape=jax.ShapeDtypeStruct((m, d), F32),
        compiler_params=_params("arbitrary", "arbitrary"),
        name="ffn_down",
    )(a, w_down, x, gate)


def _layer(x, mod, cache, w, bias, *, nb, tr):
    b, t, d = x.shape
    m = b * t
    shift_a, scale_a, gate_a, shift_f, scale_f, gate_f = mod
    dh = w["g_q_a"].shape[0]
    nh_a = w["b_f"].shape[0]
    d_a = nh_a * dh
    d_b = (w["w_qkv"].shape[1] - 3 * d_a) // 3
    nh_b = d_b // dh
    scale = dh ** -0.5
    prompt = cache is None
    tm = 1024

    def gates(gt):
        if prompt:
            return gt.reshape(b, 1, d)
        return jnp.broadcast_to(gt[:, None, :], (b, t, d)).reshape(m, d)

    h = _modnorm(x, scale_a, shift_a, w["g_attn"], nb, tr).reshape(m, d)
    pj = functools.partial(_proj, h, w["w_qkv"], dh=dh, tn=1024)
    keep_p = min(LEFT, t)
    full32 = dict(tm=512, f32_rows=t, seq_len=t) if prompt else dict(tm=tm, f32_rows=m, seq_len=m)
    tail32 = dict(tm=512, f32_rows=keep_p, seq_len=t) if prompt else full32
    (q_a,) = pj(0, d_a, w["g_q_a"] * scale, do_rms=True, tm=tm)
    k_a32, k_a = pj(d_a, d_a, w["g_k_a"], do_rms=True, **full32)
    v_a32, v_a = pj(2 * d_a, d_a, w["g_k_a"], do_rms=False, **full32)
    (q_b,) = pj(3 * d_a, d_b, w["g_q_b"] * scale, do_rms=True, tm=tm)
    k_b32, k_b = pj(3 * d_a + d_b, d_b, w["g_k_b"], do_rms=True, **tail32)
    v_b32, v_b = pj(3 * d_a + 2 * d_b, d_b, w["g_k_b"], do_rms=False, **tail32)
    logf = _forget_gate(h, w["w_f"], w["b_f"], tm).reshape(b, t, nh_a)

    r3 = lambda z: z.reshape(b, t, -1)
    if prompt:
        cum_t = _cumsum_t(logf.transpose(0, 2, 1))
        tq = 256
        o_a = _fox_prompt(r3(q_a), r3(k_a), r3(v_a), cum_t.transpose(0, 2, 1),
                          cum_t.reshape(b, nh_a, t // tq, 1, tq), nh_a, dh, tq, 4)
        o_b = _band_prompt(r3(q_b), r3(k_b), r3(v_b), bias, nh_b, dh, 4)
        new_k_b, new_v_b = k_b32.reshape(b, keep_p, nh_b, dh), v_b32.reshape(b, keep_p, nh_b, dh)
    else:
        ck_a, cv_a, clogf_a, ck_b, cv_b = cache
        past = ck_a.shape[1]
        tp = past + CUM_BLK
        lf_all = jnp.concatenate([clogf_a.astype(F32), logf, jnp.zeros((b, tp - past - t, nh_a), F32)], axis=1)
        cum_t = _cumsum_t(lf_all.transpose(0, 2, 1))
        o_a = _fox_sample(r3(q_a), ck_a.reshape(b, past * nh_a, dh), cv_a.reshape(b, past * nh_a, dh),
                          r3(k_a), r3(v_a), cum_t.transpose(0, 2, 1), cum_t, nh_a, dh, 512)
        keep = ck_b.shape[1]
        o_b = _band_sample(r3(q_b), ck_b.reshape(b, keep * nh_b, dh), cv_b.reshape(b, keep * nh_b, dh),
                           r3(k_b), r3(v_b), bias, nh_b, dh)
        new_b = lambda ck, z: jnp.concatenate([ck, z.reshape(b, t, nh_b, dh).astype(ck.dtype)], axis=1)[:, -keep:]
        new_k_b, new_v_b = new_b(ck_b, k_b32), new_b(cv_b, v_b32)

    x1 = _oproj(o_a.reshape(m, d_a), o_b.reshape(m, d_b), w["w_o"], x.reshape(m, d), gates(gate_a), tm, 1024)
    h2 = _modnorm(x1.reshape(b, t, d), scale_f, shift_f, w["g_ffn"], nb, tr).reshape(m, d)
    act = _ffn_up(h2, w["w_gu"], tm, 256)
    y = _ffn_down(act, w["w_down"], x1, gates(gate_f), 512, 512)
    return (y.reshape(b, t, d), k_a32.reshape(b, t, nh_a, dh), v_a32.reshape(b, t, nh_a, dh), logf,
            new_k_b, new_v_b)


def kernel(x_prompt, x_sample, cache_k_a, cache_v_a, cache_logf_a, cache_k_b, cache_v_b, c_prompt, c_sample,
           w_ada, b_ada, g_attn, g_ffn, w_in, b_f, g_q_a, g_k_a, g_q_b, g_k_b, rel_table, w_o, w_gu, w_down):
    depth = w_ada.shape[0]
    d = x_prompt.shape[-1]
    nb_p, nb_s = c_prompt.shape[0], c_sample.shape[0]
    nh_a = b_f.shape[1]
    n_qkv = w_in.shape[2] - nh_a
    y_p, y_s = x_prompt, x_sample
    outs_p, outs_s = [], []
    for l in range(depth):
        w_f = jnp.pad(w_in[l][:, n_qkv:], ((0, 0), (0, LANES - nh_a))).astype(BF16)
        w = dict(w_qkv=w_in[l][:, :n_qkv].astype(BF16), w_f=w_f, b_f=b_f[l], g_attn=g_attn[l], g_ffn=g_ffn[l],
                 g_q_a=g_q_a[l], g_k_a=g_k_a[l], g_q_b=g_q_b[l], g_k_b=g_k_b[l],
                 w_o=w_o[l].astype(BF16), w_gu=w_gu[l].astype(BF16), w_down=w_down[l].astype(BF16))
        c_all = jnp.concatenate([c_prompt, c_sample], axis=0)
        pad = (-c_all.shape[0]) % 16
        mod = _ada(jnp.pad(c_all, ((0, pad), (0, 0))), w_ada[l], b_ada[l])
        mod_p = [mod[:nb_p, i * d:(i + 1) * d] for i in range(6)]
        mod_s = [mod[nb_p:nb_p + nb_s, i * d:(i + 1) * d] for i in range(6)]
        bias = _band_bias(rel_table[l])
        y_p, *rest_p = _layer(y_p, mod_p, None, w, bias, nb=1, tr=min(512, x_prompt.shape[1]))
        cache = (cache_k_a[l], cache_v_a[l], cache_logf_a[l], cache_k_b[l], cache_v_b[l])
        y_s, *rest_s = _layer(y_s, mod_s, cache, w, bias, nb=nb_s, tr=x_sample.shape[1])
        outs_p.append(rest_p)
        outs_s.append(rest_s)
    stack = lambda outs, i: jnp.stack([o[i] for o in outs])
    return (y_p, y_s, *[stack(outs_p, i) for i in range(5)], *[stack(outs_s, i) for i in range(5)])
```

```python
import functools

import jax
import jax.numpy as jnp
from jax import lax
from jax.experimental import pallas as pl
from jax.experimental.pallas import tpu as pltpu

CHUNK = 64
LEFT_CHUNKS = 8
LEFT = LEFT_CHUNKS * CHUNK
MAX_REL = 128
EPS = 1e-6
NEG = -1e30
LOG2E = 1.4426950408889634

LANES = 128
BAND_TQ = 256
BAND_WIN = LEFT + BAND_TQ
BAND_EXT = BAND_WIN + LEFT
BAND_ROLL = 2048
CUM_BLK = 256
FOX_PAD = 16
VMEM_LIMIT_BYTES = 56 * 1024 * 1024

F32 = jnp.float32
BF16 = jnp.bfloat16


def _params(*semantics):
    return pltpu.CompilerParams(dimension_semantics=semantics, vmem_limit_bytes=VMEM_LIMIT_BYTES)


def _tile(n, want):
    if n <= want:
        return n
    t = (want // LANES) * LANES
    while t >= LANES:
        if n % t == 0:
            return t
        t -= LANES
    raise ValueError(f"no lane-aligned tile for {n}")


def _ada_kernel(c_ref, w_ref, b_ref, o_ref):
    c = c_ref[...]
    a = (c * jax.nn.sigmoid(c)).astype(BF16)
    o_ref[...] = jnp.dot(a, w_ref[...].astype(BF16), preferred_element_type=F32) + b_ref[...]


def _ada(c, w_ada, b_ada):
    m, d = c.shape
    n = w_ada.shape[1]
    tn = _tile(n, 512)
    return pl.pallas_call(
        _ada_kernel,
        grid=(n // tn,),
        in_specs=[pl.BlockSpec((m, d), lambda j: (0, 0)),
                  pl.BlockSpec((d, tn), lambda j: (0, j)),
                  pl.BlockSpec((1, tn), lambda j: (0, j))],
        out_specs=pl.BlockSpec((m, tn), lambda j: (0, j)),
        out_shape=jax.ShapeDtypeStruct((m, n), F32),
        compiler_params=_params("arbitrary"),
        name="ada_mod",
    )(c, w_ada, b_ada.reshape(1, n))


def _modnorm_kernel(x_ref, sc_ref, sh_ref, g_ref, o_ref):
    x = x_ref[...]
    ms = jnp.mean(x * x, axis=-1, keepdims=True)
    y = x * lax.rsqrt(ms + EPS) * g_ref[...]
    o_ref[...] = (y * (1.0 + sc_ref[...]) + sh_ref[...]).astype(o_ref.dtype)


def _modnorm(x, scale, shift, g, nb, tr):
    b, t, d = x.shape
    bs = pl.BlockSpec((nb, 1, d), lambda i, j: (i, 0, 0))
    return pl.pallas_call(
        _modnorm_kernel,
        grid=(b // nb, t // tr),
        in_specs=[pl.BlockSpec((nb, tr, d), lambda i, j: (i, j, 0)), bs, bs,
                  pl.BlockSpec((1, 1, d), lambda i, j: (0, 0, 0))],
        out_specs=pl.BlockSpec((nb, tr, d), lambda i, j: (i, j, 0)),
        out_shape=jax.ShapeDtypeStruct((b, t, d), BF16),
        compiler_params=_params("arbitrary", "arbitrary"),
        name="modnorm",
    )(x, scale.reshape(b, 1, d), shift.reshape(b, 1, d), g.reshape(1, 1, d))


def _bf16_weights(w_ref, copy_ref):
    if copy_ref is None:
        return w_ref[...]
    w = w_ref[...].astype(BF16)
    copy_ref[...] = w
    return w


def _proj_kernel(a_ref, w_ref, g_ref, *o_refs, do_rms, dh, nh, rows32, tiles_per_seq, emit_w):
    if emit_w:
        *o_refs, wcopy_ref = o_refs
    w = _bf16_weights(w_ref, wcopy_ref if emit_w else None)
    acc = jnp.dot(a_ref[...], w, preferred_element_type=F32)
    o16_ref = o_refs[-1]
    tm, tn = acc.shape
    hpt = tn // dh
    g = g_ref[...]
    ys = []
    for hh in range(hpt):
        y = acc[:, hh * dh:(hh + 1) * dh]
        if do_rms:
            ms = jnp.mean(y * y, axis=-1, keepdims=True)
            y = y * lax.rsqrt(ms + EPS) * g
        o16_ref[:, hh * dh:(hh + 1) * dh] = y.astype(o16_ref.dtype)
        ys.append(y)
    if rows32:
        o32_ref = o_refs[0]
        h0 = pl.program_id(1) * hpt

        def store32():
            for hh in range(hpt):
                o32_ref[pl.ds(h0 + hh, rows32, stride=nh), :] = ys[hh][tm - rows32:, :]

        if rows32 == tm:
            store32()
        else:
            pl.when(pl.program_id(0) % tiles_per_seq == tiles_per_seq - 1)(store32)


def _proj(a, w, col0, n, g, *, do_rms, dh, tm, tn, f32_rows=0, seq_len=None):
    m, k = a.shape
    tm = min(tm, m)
    tn = _tile(n, tn)
    nh = n // dh
    j0 = col0 // tn
    emit_w = w.dtype != BF16
    assert col0 % tn == 0 and m % tm == 0 and (not emit_w or m == tm)
    out_specs = [pl.BlockSpec((tm, tn), lambda i, j: (i, j))]
    out_shape = [jax.ShapeDtypeStruct((m, n), BF16)]
    rows32, tps = 0, 1
    if f32_rows:
        tps = seq_len // tm
        assert seq_len % tm == 0
        if f32_rows == seq_len:
            rows32 = tm
            out_specs.insert(0, pl.BlockSpec((tm * nh, dh), lambda i, j: (i, 0)))
            out_shape.insert(0, jax.ShapeDtypeStruct((m * nh, dh), F32))
        else:
            rows32 = f32_rows
            assert rows32 <= tm
            out_specs.insert(0, pl.BlockSpec((rows32 * nh, dh), lambda i, j: (i // tps, 0)))
            out_shape.insert(0, jax.ShapeDtypeStruct((m // seq_len * rows32 * nh, dh), F32))
    if emit_w:
        out_specs.append(pl.BlockSpec((k, tn), lambda i, j: (0, j)))
        out_shape.append(jax.ShapeDtypeStruct((k, n), BF16))
    return pl.pallas_call(
        functools.partial(_proj_kernel, do_rms=do_rms, dh=dh, nh=nh, rows32=rows32, tiles_per_seq=tps,
                          emit_w=emit_w),
        grid=(m // tm, n // tn),
        in_specs=[pl.BlockSpec((tm, k), lambda i, j: (i, 0)),
                  pl.BlockSpec((k, tn), lambda i, j: (0, j0 + j)),
                  pl.BlockSpec((1, dh), lambda i, j: (0, 0))],
        out_specs=out_specs,
        out_shape=out_shape,
        compiler_params=_params("arbitrary", "arbitrary"),
        name="proj",
    )(a, w, g.reshape(1, dh).astype(F32))


def _gate_kernel(a_ref, w_ref, b_ref, o_ref):
    z = jnp.dot(a_ref[...], w_ref[...], preferred_element_type=F32)
    z = z[:, :o_ref.shape[1]] + b_ref[...]
    o_ref[...] = jnp.minimum(z, 0.0) - jnp.log1p(jnp.exp(-jnp.abs(z)))


def _forget_gate(a, w_f, b_f, tm):
    m, k = a.shape
    nh = b_f.shape[0]
    tm = min(tm, m)
    return pl.pallas_call(
        _gate_kernel,
        grid=(m // tm,),
        in_specs=[pl.BlockSpec((tm, k), lambda i: (i, 0)),
                  pl.BlockSpec((k, LANES), lambda i: (0, 0)),
                  pl.BlockSpec((1, nh), lambda i: (0, 0))],
        out_specs=pl.BlockSpec((tm, nh), lambda i: (i, 0)),
        out_shape=jax.ShapeDtypeStruct((m, nh), F32),
        compiler_params=_params("arbitrary"),
        name="forget_gate",
    )(a, w_f, b_f.reshape(1, nh).astype(F32))


def _cumsum_kernel(x_ref, o_ref):
    h, tp = x_ref.shape[1], x_ref.shape[2]
    r = lax.broadcasted_iota(jnp.int32, (CUM_BLK, CUM_BLK), 0)
    c = lax.broadcasted_iota(jnp.int32, (CUM_BLK, CUM_BLK), 1)
    tri = (r <= c).astype(BF16)
    carry = jnp.zeros((h, 1), F32)
    for blk in range(tp // CUM_BLK):
        x = x_ref[0, :, blk * CUM_BLK:(blk + 1) * CUM_BLK]
        hi = x.astype(BF16)
        r1 = x - hi.astype(F32)
        mid = r1.astype(BF16)
        lo = (r1 - mid.astype(F32)).astype(BF16)
        cs = (jnp.dot(hi, tri, preferred_element_type=F32)
              + jnp.dot(mid, tri, preferred_element_type=F32)
              + jnp.dot(lo, tri, preferred_element_type=F32)) + carry
        o_ref[0, :, blk * CUM_BLK:(blk + 1) * CUM_BLK] = cs
        carry = cs[:, CUM_BLK - 1:CUM_BLK]


def _cumsum_t(logf_t):
    b, h, tp = logf_t.shape
    return pl.pallas_call(
        _cumsum_kernel,
        grid=(b,),
        in_specs=[pl.BlockSpec((1, h, tp), lambda i: (i, 0, 0))],
        out_specs=pl.BlockSpec((1, h, tp), lambda i: (i, 0, 0)),
        out_shape=jax.ShapeDtypeStruct((b, h, tp), F32),
        compiler_params=_params("arbitrary"),
        name="logf_cumsum",
    )(logf_t)


def _head_column(cc, h):
    lane = lax.broadcasted_iota(jnp.int32, cc.shape, 1)
    return jnp.sum(jnp.where(lane == h, cc, 0.0), axis=1, keepdims=True)


def _qk(q, k):
    return lax.dot_general(q, k, (((1,), (1,)), ((), ())), preferred_element_type=F32)


def _fox_kernel(q_ref, k_ref, v_ref, cc_ref, o_ref, vt_sc, ck_sc, *, tq, hb, dh):
    hg = pl.program_id(1)
    i = pl.program_id(2)
    nk = vt_sc.shape[1]
    heads = range(hb)
    hcols = [slice(hh * dh, (hh + 1) * dh) for hh in heads]

    @pl.when(i == 0)
    def _():
        ones = jnp.ones((FOX_PAD, tq), BF16)
        for kb in range(nk):
            rows = slice(kb * tq, (kb + 1) * tq)
            cc = cc_ref[0, rows, :]
            for hh in heads:
                vt_sc[hh, kb, :dh, :] = v_ref[0, rows, hcols[hh]].T
                vt_sc[hh, kb, dh:, :] = ones
                ck_sc[hh, rows, :] = jnp.broadcast_to(_head_column(cc, hg * hb + hh) * (-LOG2E), (tq, LANES))

    qts = [q_ref[0, :, cols].T for cols in hcols]

    def scores(kb):
        off = pl.multiple_of(kb * tq, tq)
        out = []
        for hh in heads:
            s = jnp.dot(k_ref[0, pl.ds(off, tq), hcols[hh]], qts[hh], preferred_element_type=F32)
            ck = ck_sc[hh, pl.ds(off, tq), :]
            out.append(s + jnp.concatenate([ck] * (tq // LANES), axis=1))
        return out

    def step(kb, state, mask=None):
        stats = []
        for s, (m, _) in zip(scores(kb), state):
            if mask is not None:
                s = jnp.where(mask, s, NEG)
            m_new = jnp.maximum(m, jnp.max(s, axis=0, keepdims=True))
            stats.append((m_new, jnp.exp2(m - m_new), jnp.exp2(s - m_new).astype(BF16)))
        return [(m_new, alpha * acc + jnp.dot(vt_sc[hh, kb], p, preferred_element_type=F32))
                for hh, ((m_new, alpha, p), (_, acc)) in enumerate(zip(stats, state))]

    state = [(jnp.full((1, tq), NEG, F32), jnp.zeros((dh + FOX_PAD, tq), F32)) for _ in heads]
    state = lax.fori_loop(0, i, step, state)
    key = lax.broadcasted_iota(jnp.int32, (tq, tq), 0)
    qry = lax.broadcasted_iota(jnp.int32, (tq, tq), 1)
    state = step(i, state, mask=key <= qry)
    for cols, (_, acc) in zip(hcols, state):
        o_ref[0, :, cols] = (acc[:dh] / acc[dh:dh + 1]).T.astype(o_ref.dtype)


def _fox_prompt(q, k, v, cum_col, nh, dh, tq, hb):
    b, t, _ = q.shape
    hb = min(hb, nh)
    assert t % tq == 0 and tq % LANES == 0
    qspec = pl.BlockSpec((1, tq, hb * dh), lambda bb, h, i: (bb, i, h))
    kspec = pl.BlockSpec((1, t, hb * dh), lambda bb, h, i: (bb, 0, h))
    return pl.pallas_call(
        functools.partial(_fox_kernel, tq=tq, hb=hb, dh=dh),
        grid=(b, nh // hb, t // tq),
        in_specs=[qspec, kspec, kspec, pl.BlockSpec((1, t, nh), lambda bb, h, i: (bb, 0, 0))],
        out_specs=qspec,
        out_shape=jax.ShapeDtypeStruct((b, t, nh * dh), BF16),
        scratch_shapes=[pltpu.VMEM((hb, t // tq, dh + FOX_PAD, tq), BF16), pltpu.VMEM((hb, t, LANES), F32)],
        compiler_params=_params("arbitrary", "arbitrary", "arbitrary"),
        name="fox_prompt",
    )(q, k, v, cum_col)


def _head_rows(ref, h, n, nh):
    return ref[0, pl.ds(h, n, stride=nh), :].astype(BF16)


def _fox_s_kernel(q_ref, kc_ref, vc_ref, kn_ref, vn_ref, cc_ref, crc_ref, crn_ref, o_ref, m_sc, l_sc, acc_sc,
                  *, nh, dh, tc, s_len):
    c = pl.program_id(1)
    cc = cc_ref[0][:s_len]
    r = lax.broadcasted_iota(jnp.int32, (s_len, s_len), 0)
    col = lax.broadcasted_iota(jnp.int32, (s_len, s_len), 1)

    @pl.when(c == 0)
    def _():
        hcols = [slice(h * dh, (h + 1) * dh) for h in range(nh)]
        ss = [_qk(q_ref[0, :, cols], kn_ref[0, :, cols]) + (cc[:, h:h + 1] - crn_ref[0, h:h + 1, :s_len])
              for h, cols in enumerate(hcols)]
        ps = []
        for h, s in enumerate(ss):
            s = jnp.where(col <= r, s, NEG)
            m = jnp.max(s, axis=1, keepdims=True)
            p = jnp.exp(s - m)
            m_sc[h] = m
            l_sc[h] = jnp.sum(p, axis=1, keepdims=True)
            ps.append(p.astype(BF16))
        for h, p in enumerate(ps):
            acc_sc[h] = jnp.dot(p, vn_ref[0, :, hcols[h]], preferred_element_type=F32)

    ss = [_qk(q_ref[0, :, h * dh:(h + 1) * dh], _head_rows(kc_ref, h, tc, nh))
          + (cc[:, h:h + 1] - crc_ref[0, h:h + 1, :]) for h in range(nh)]
    ps = []
    for h, s in enumerate(ss):
        m_old = m_sc[h]
        m_new = jnp.maximum(m_old, jnp.max(s, axis=1, keepdims=True))
        alpha = jnp.exp(m_old - m_new)
        p = jnp.exp(s - m_new)
        m_sc[h] = m_new
        l_sc[h] = alpha * l_sc[h] + jnp.sum(p, axis=1, keepdims=True)
        ps.append((alpha, p.astype(BF16)))
    for h, (alpha, p) in enumerate(ps):
        acc_sc[h] = alpha * acc_sc[h] + jnp.dot(p, _head_rows(vc_ref, h, tc, nh), preferred_element_type=F32)

    @pl.when(c == pl.num_programs(1) - 1)
    def _():
        for h in range(nh):
            o_ref[0, :, h * dh:(h + 1) * dh] = (acc_sc[h] / l_sc[h]).astype(o_ref.dtype)


def _fox_sample(q, kc, vc, kn, vn, cum_col, cum_t, nh, dh, tc):
    b, s_len, _ = q.shape
    past = kc.shape[1] // nh
    tp = cum_col.shape[1]
    tc = min(tc, past)
    assert past % CUM_BLK == 0 and tp - past == CUM_BLK and past % tc == 0 and s_len <= CUM_BLK
    new = pl.BlockSpec((1, s_len, nh * dh), lambda bb, c: (bb, 0, 0))
    cache = pl.BlockSpec((1, tc * nh, dh), lambda bb, c: (bb, c, 0))
    return pl.pallas_call(
        functools.partial(_fox_s_kernel, nh=nh, dh=dh, tc=tc, s_len=s_len),
        grid=(b, past // tc),
        in_specs=[new, cache, cache, new, new,
                  pl.BlockSpec((1, CUM_BLK, nh), lambda bb, c: (bb, past // CUM_BLK, 0)),
                  pl.BlockSpec((1, nh, tc), lambda bb, c: (bb, 0, c)),
                  pl.BlockSpec((1, nh, CUM_BLK), lambda bb, c: (bb, 0, past // CUM_BLK))],
        out_specs=new,
        out_shape=jax.ShapeDtypeStruct((b, s_len, nh * dh), BF16),
        scratch_shapes=[pltpu.VMEM((nh, s_len, 1), F32), pltpu.VMEM((nh, s_len, 1), F32),
                        pltpu.VMEM((nh, s_len, dh), F32)],
        compiler_params=_params("arbitrary", "arbitrary"),
        name="fox_sample",
    )(q, kc, vc, kn, vn, cum_col, cum_t, cum_t)


def _band_bias_kernel(gf_ref, o_ref):
    f8 = jnp.broadcast_to(gf_ref[0], (8, BAND_ROLL))
    sub = lax.broadcasted_iota(jnp.int32, (8, BAND_ROLL), 0)
    base = f8
    for bb in range(1, 8):
        base = jnp.where(sub == bb, pltpu.roll(f8, bb, axis=1), base)
    row8 = lax.broadcasted_iota(jnp.int32, (8, BAND_EXT), 0)
    col = lax.broadcasted_iota(jnp.int32, (8, BAND_EXT), 1)
    key_chunk = col // CHUNK - LEFT_CHUNKS
    for a in range(BAND_TQ // 8):
        rows = base if a == 0 else pltpu.roll(base, 8 * a, axis=1)
        rows = rows[:, :BAND_EXT]
        cd = (row8 + 8 * a) // CHUNK - key_chunk
        rows = jnp.where((cd >= 0) & (cd <= LEFT_CHUNKS), rows, NEG)
        for w in range(3):
            off = LEFT - w * BAND_TQ
            o_ref[0, w, 8 * a:8 * a + 8, :] = rows[:, off:off + BAND_WIN]


def _band_bias(rel_table):
    nh = rel_table.shape[0]
    u = jnp.arange(BAND_ROLL)
    v = jnp.where(u < BAND_ROLL - BAND_TQ, u, u - BAND_ROLL)
    idx = jnp.clip(LEFT - v, -MAX_REL, MAX_REL) + MAX_REL
    gf = jnp.take(rel_table.astype(F32), idx, axis=1).reshape(nh, 1, BAND_ROLL)
    return pl.pallas_call(
        _band_bias_kernel,
        grid=(nh,),
        in_specs=[pl.BlockSpec((1, 1, BAND_ROLL), lambda h: (h, 0, 0))],
        out_specs=pl.BlockSpec((1, 3, BAND_TQ, BAND_WIN), lambda h: (h, 0, 0, 0)),
        out_shape=jax.ShapeDtypeStruct((nh, 3, BAND_TQ, BAND_WIN), F32),
        compiler_params=_params("arbitrary"),
        name="band_bias",
    )(gf)


def _softmax_pv(score_parts, value_parts):
    probs = []
    for parts in score_parts:
        m = functools.reduce(jnp.maximum, [jnp.max(s, axis=1, keepdims=True) for s in parts])
        ps = [jnp.exp(s - m) for s in parts]
        l = functools.reduce(jnp.add, [jnp.sum(p, axis=1, keepdims=True) for p in ps])
        probs.append(([p.astype(BF16) for p in ps], l))
    outs = []
    for (ps, l), vals in zip(probs, value_parts):
        o = functools.reduce(jnp.add, [jnp.dot(p, v(), preferred_element_type=F32) for p, v in zip(ps, vals)])
        outs.append(o / l)
    return outs


def _band_kernel(q_ref, k_ref, v_ref, e_ref, o_ref, *, hb, dh):
    g = pl.program_id(2)
    ws = pl.multiple_of(jnp.maximum(g * BAND_TQ - LEFT, 0), BAND_TQ)
    w = jnp.minimum(g, LEFT // BAND_TQ)
    hcols = [slice(hh * dh, (hh + 1) * dh) for hh in range(hb)]
    scores = [[_qk(q_ref[0, :, cols], k_ref[0, pl.ds(ws, BAND_WIN), cols]) + e_ref[hh, w]]
              for hh, cols in enumerate(hcols)]
    values = [[functools.partial(lambda cols: v_ref[0, pl.ds(ws, BAND_WIN), cols], cols)] for cols in hcols]
    for cols, o in zip(hcols, _softmax_pv(scores, values)):
        o_ref[0, :, cols] = o.astype(o_ref.dtype)


def _band_prompt(q, k, v, bias, nh, dh, hb):
    b, t, _ = q.shape
    assert t % BAND_TQ == 0 and t >= BAND_WIN
    hb = min(hb, nh)
    qspec = pl.BlockSpec((1, BAND_TQ, hb * dh), lambda h, bb, g: (bb, g, h))
    kspec = pl.BlockSpec((1, t, hb * dh), lambda h, bb, g: (bb, 0, h))
    return pl.pallas_call(
        functools.partial(_band_kernel, hb=hb, dh=dh),
        grid=(nh // hb, b, t // BAND_TQ),
        in_specs=[qspec, kspec, kspec,
                  pl.BlockSpec((hb, 3, BAND_TQ, BAND_WIN), lambda h, bb, g: (h, 0, 0, 0),
                               pipeline_mode=pl.Buffered(1))],
        out_specs=qspec,
        out_shape=jax.ShapeDtypeStruct((b, t, nh * dh), BF16),
        compiler_params=_params("arbitrary", "arbitrary", "arbitrary"),
        name="band_prompt",
    )(q, k, v, bias)


def _band_s_kernel(q_ref, kc_ref, vc_ref, kn_ref, vn_ref, e_ref, o_ref, *, nh, dh, keep, s_len):
    hcols = [slice(h * dh, (h + 1) * dh) for h in range(nh)]
    scores, values = [], []
    for h, cols in enumerate(hcols):
        q = q_ref[0, :, cols]
        e = e_ref[h, 0]
        scores.append([_qk(q, _head_rows(kc_ref, h, keep, nh)) + e[:, :keep],
                       _qk(q, kn_ref[0, :, cols]) + e[:, keep:keep + s_len]])
        values.append([functools.partial(_head_rows, vc_ref, h, keep, nh),
                       functools.partial(lambda cols: vn_ref[0, :, cols], cols)])
    for cols, o in zip(hcols, _softmax_pv(scores, values)):
        o_ref[0, :, cols] = o.astype(o_ref.dtype)


def _band_sample(q, kc, vc, kn, vn, bias, nh, dh):
    b, s_len, _ = q.shape
    keep = kc.shape[1] // nh
    assert keep == LEFT and s_len <= CHUNK and s_len % 8 == 0
    new = pl.BlockSpec((1, s_len, nh * dh), lambda bb: (bb, 0, 0))
    cache = pl.BlockSpec((1, keep * nh, dh), lambda bb: (bb, 0, 0))
    return pl.pallas_call(
        functools.partial(_band_s_kernel, nh=nh, dh=dh, keep=keep, s_len=s_len),
        grid=(b,),
        in_specs=[new, cache, cache, new, new,
                  pl.BlockSpec((nh, 1, s_len, BAND_WIN), lambda bb: (0, LEFT // BAND_TQ, 0, 0))],
        out_specs=new,
        out_shape=jax.ShapeDtypeStruct((b, s_len, nh * dh), BF16),
        compiler_params=_params("arbitrary"),
        name="band_sample",
    )(q, kc, vc, kn, vn, bias)


def _oproj_kernel(oa_ref, ob_ref, wa_ref, wb_ref, x_ref, gate_ref, o_ref, *wcopy_refs):
    ca, cb = wcopy_refs if wcopy_refs else (None, None)
    acc = (jnp.dot(oa_ref[...], _bf16_weights(wa_ref, ca), preferred_element_type=F32)
           + jnp.dot(ob_ref[...], _bf16_weights(wb_ref, cb), preferred_element_type=F32))
    o_ref[...] = x_ref[...] + gate_ref[...] * acc


def _gate_spec(gate, m, tm, tn):
    if gate.ndim == 2:
        return pl.BlockSpec((tm, tn), lambda i, j: (i, j))
    tiles_per_seq = m // gate.shape[0] // tm
    return pl.BlockSpec((None, 1, tn), lambda i, j: (i // tiles_per_seq, 0, j))


def _oproj(oa, ob, wa, wb, x, gate, tm, tn):
    m, da = oa.shape
    db = ob.shape[1]
    (wa, ra), (wb, rb) = wa, wb
    d = wa.shape[1]
    tm = min(tm, m)
    tn = _tile(d, tn)
    emit_w = wa.dtype != BF16
    assert da == db and (not emit_w or m == tm)
    out_specs = [pl.BlockSpec((tm, tn), lambda i, j: (i, j))]
    out_shape = [jax.ShapeDtypeStruct((m, d), F32)]
    if emit_w:
        out_specs += [pl.BlockSpec((da, tn), lambda i, j: (0, j))] * 2
        out_shape += [jax.ShapeDtypeStruct((da, d), BF16)] * 2
    return pl.pallas_call(
        _oproj_kernel,
        grid=(m // tm, d // tn),
        in_specs=[pl.BlockSpec((tm, da), lambda i, j: (i, 0)),
                  pl.BlockSpec((tm, db), lambda i, j: (i, 0)),
                  pl.BlockSpec((da, tn), lambda i, j: (ra, j)),
                  pl.BlockSpec((db, tn), lambda i, j: (rb, j)),
                  pl.BlockSpec((tm, tn), lambda i, j: (i, j)),
                  _gate_spec(gate, m, tm, tn)],
        out_specs=out_specs,
        out_shape=out_shape,
        compiler_params=_params("arbitrary", "arbitrary"),
        name="out_proj",
    )(oa, ob, wa, wb, x, gate)


def _ffn_up_kernel(a_ref, wg_ref, wu_ref, o_ref, *wcopy_refs):
    cg, cu = wcopy_refs if wcopy_refs else (None, None)
    a = a_ref[...]
    g = jnp.dot(a, _bf16_weights(wg_ref, cg), preferred_element_type=F32)
    u = jnp.dot(a, _bf16_weights(wu_ref, cu), preferred_element_type=F32)
    o_ref[...] = (g * jax.nn.sigmoid(g) * u).astype(o_ref.dtype)


def _ffn_up(a, wg, wu, f, tm, tn):
    m, k = a.shape
    (wg, cg), (wu, cu) = wg, wu
    tm = min(tm, m)
    tn = _tile(f, tn)
    jg, ju = cg // tn, cu // tn
    emit_w = wg.dtype != BF16
    assert cg % tn == 0 and cu % tn == 0 and (not emit_w or m == tm)
    out_specs = [pl.BlockSpec((tm, tn), lambda i, j: (i, j))]
    out_shape = [jax.ShapeDtypeStruct((m, f), BF16)]
    if emit_w:
        out_specs += [pl.BlockSpec((k, tn), lambda i, j: (0, j))] * 2
        out_shape += [jax.ShapeDtypeStruct((k, f), BF16)] * 2
    return pl.pallas_call(
        _ffn_up_kernel,
        grid=(m // tm, f // tn),
        in_specs=[pl.BlockSpec((tm, k), lambda i, j: (i, 0)),
                  pl.BlockSpec((k, tn), lambda i, j: (0, jg + j)),
                  pl.BlockSpec((k, tn), lambda i, j: (0, ju + j))],
        out_specs=out_specs,
        out_shape=out_shape,
        compiler_params=_params("arbitrary", "arbitrary"),
        name="ffn_up",
    )(a, wg, wu)


def _ffn_down_kernel(a_ref, w_ref, x_ref, gate_ref, o_ref, *wcopy_refs):
    w = _bf16_weights(w_ref, wcopy_refs[0] if wcopy_refs else None)
    acc = jnp.dot(a_ref[...], w, preferred_element_type=F32)
    o_ref[...] = x_ref[...] + gate_ref[...] * acc


def _ffn_down(a, w_down, x, gate, tm, tn):
    m, f = a.shape
    d = w_down.shape[1]
    tm = min(tm, m)
    tn = _tile(d, tn)
    emit_w = w_down.dtype != BF16
    assert not emit_w or m == tm
    out_specs = [pl.BlockSpec((tm, tn), lambda i, j: (i, j))]
    out_shape = [jax.ShapeDtypeStruct((m, d), F32)]
    if emit_w:
        out_specs.append(pl.BlockSpec((f, tn), lambda i, j: (0, j)))
        out_shape.append(jax.ShapeDtypeStruct((f, d), BF16))
    return pl.pallas_call(
        _ffn_down_kernel,
        grid=(m // tm, d // tn),
        in_specs=[pl.BlockSpec((tm, f), lambda i, j: (i, 0)),
                  pl.BlockSpec((f, tn), lambda i, j: (0, j)),
                  pl.BlockSpec((tm, tn), lambda i, j: (i, j)),
                  _gate_spec(gate, m, tm, tn)],
        out_specs=out_specs,
        out_shape=out_shape,
        compiler_params=_params("arbitrary", "arbitrary"),
        name="ffn_down",
    )(a, w_down, x, gate)


def _layer(x, mod, cache, w, bias, *, nb, tr):
    b, t, d = x.shape
    m = b * t
    shift_a, scale_a, gate_a, shift_f, scale_f, gate_f = mod
    dh = w["g_q_a"].shape[0]
    nh_a = w["b_f"].shape[0]
    d_a = d_b = w["d_a"]
    nh_b = d_b // dh
    f = w["f"]
    scale = dh ** -0.5
    prompt = cache is None
    tm = 1024
    w16 = {}

    def gates(gt):
        if prompt:
            return gt.reshape(b, 1, d)
        return jnp.broadcast_to(gt[:, None, :], (b, t, d)).reshape(m, d)

    h = _modnorm(x, scale_a, shift_a, w["g_attn"], nb, tr).reshape(m, d)
    keep_p = min(LEFT, t)
    full32 = dict(tm=512, f32_rows=t, seq_len=t) if prompt else dict(tm=tm, f32_rows=m, seq_len=m)
    tail32 = dict(tm=512, f32_rows=keep_p, seq_len=t) if prompt else full32

    def pj(name, n, g, **kw):
        arr, col0 = w[name]
        outs = _proj(h, arr, col0, n, g, dh=dh, tn=1024 if prompt else 512, **kw)
        if not prompt:
            *outs, wcopy = outs
            w16[name] = (wcopy, 0)
        return outs

    (q_a,) = pj("w_q_a", d_a, w["g_q_a"] * (scale * LOG2E if prompt else scale), do_rms=True, tm=tm)
    k_a32, k_a = pj("w_k_a", d_a, w["g_k_a"], do_rms=True, **full32)
    v_a32, v_a = pj("w_v_a", d_a, w["g_k_a"], do_rms=False, **full32)
    (q_b,) = pj("w_q_b", d_b, w["g_q_b"] * scale, do_rms=True, tm=tm)
    k_b32, k_b = pj("w_k_b", d_b, w["g_k_b"], do_rms=True, **tail32)
    v_b32, v_b = pj("w_v_b", d_b, w["g_k_b"], do_rms=False, **tail32)
    logf = _forget_gate(h, w["w_f"], w["b_f"], tm).reshape(b, t, nh_a)

    r3 = lambda z: z.reshape(b, t, -1)
    if prompt:
        cum_t = _cumsum_t(logf.transpose(0, 2, 1))
        tq = 256
        o_a = _fox_prompt(r3(q_a), r3(k_a), r3(v_a), cum_t.transpose(0, 2, 1), nh_a, dh, tq, 8)
        o_b = _band_prompt(r3(q_b), r3(k_b), r3(v_b), bias, nh_b, dh, 4)
        new_k_b, new_v_b = k_b32.reshape(b, keep_p, nh_b, dh), v_b32.reshape(b, keep_p, nh_b, dh)
    else:
        ck_a, cv_a, clogf_a, ck_b, cv_b = cache
        past = ck_a.shape[1]
        tp = past + CUM_BLK
        lf_all = jnp.concatenate([clogf_a.astype(F32), logf, jnp.zeros((b, tp - past - t, nh_a), F32)], axis=1)
        cum_t = _cumsum_t(lf_all.transpose(0, 2, 1))
        o_a = _fox_sample(r3(q_a), ck_a.reshape(b, past * nh_a, dh), cv_a.reshape(b, past * nh_a, dh),
                          r3(k_a), r3(v_a), cum_t.transpose(0, 2, 1), cum_t, nh_a, dh, 512)
        keep = ck_b.shape[1]
        o_b = _band_sample(r3(q_b), ck_b.reshape(b, keep * nh_b, dh), cv_b.reshape(b, keep * nh_b, dh),
                           r3(k_b), r3(v_b), bias, nh_b, dh)
        new_b = lambda ck, z: jnp.concatenate([ck, z.reshape(b, t, nh_b, dh).astype(ck.dtype)], axis=1)[:, -keep:]
        new_k_b, new_v_b = new_b(ck_b, k_b32), new_b(cv_b, v_b32)

    x1, *wo16 = _oproj(o_a.reshape(m, d_a), o_b.reshape(m, d_b), w["w_oa"], w["w_ob"], x.reshape(m, d),
                       gates(gate_a), tm, 1024 if prompt else 512)
    h2 = _modnorm(x1.reshape(b, t, d), scale_f, shift_f, w["g_ffn"], nb, tr).reshape(m, d)
    act, *wgu16 = _ffn_up(h2, w["w_g"], w["w_u"], f, tm, 256)
    y, *wd16 = _ffn_down(act, w["w_down"], x1, gates(gate_f), 512, 512 if prompt else 128)
    if not prompt:
        w16.update(w_oa=(wo16[0], 0), w_ob=(wo16[1], 0), w_g=(wgu16[0], 0), w_u=(wgu16[1], 0), w_down=wd16[0])
    return (y.reshape(b, t, d), k_a32.reshape(b, t, nh_a, dh), v_a32.reshape(b, t, nh_a, dh), logf,
            new_k_b, new_v_b, w16)


def kernel(x_prompt, x_sample, cache_k_a, cache_v_a, cache_logf_a, cache_k_b, cache_v_b, c_prompt, c_sample,
           w_ada, b_ada, g_attn, g_ffn, w_in, b_f, g_q_a, g_k_a, g_q_b, g_k_b, rel_table, w_o, w_gu, w_down):
    depth = w_ada.shape[0]
    d = x_prompt.shape[-1]
    nb_p, nb_s = c_prompt.shape[0], c_sample.shape[0]
    nh_a = b_f.shape[1]
    n_qkv = w_in.shape[2] - nh_a
    d_a = n_qkv // 6
    f = w_down.shape[1]
    assert 2 * d_a == w_o.shape[1] and w_gu.shape[2] == 2 * f
    y_p, y_s = x_prompt, x_sample
    outs_p, outs_s = [], []
    for l in range(depth):
        w_f = jnp.pad(w_in[l][:, n_qkv:], ((0, 0), (0, LANES - nh_a))).astype(BF16)
        small = dict(w_f=w_f, b_f=b_f[l], g_attn=g_attn[l], g_ffn=g_ffn[l], g_q_a=g_q_a[l], g_k_a=g_k_a[l],
                     g_q_b=g_q_b[l], g_k_b=g_k_b[l], d_a=d_a, f=f)
        names = ("w_q_a", "w_k_a", "w_v_a", "w_q_b", "w_k_b", "w_v_b")
        w32 = dict(small, w_oa=(w_o[l], 0), w_ob=(w_o[l], 1), w_g=(w_gu[l], 0), w_u=(w_gu[l], f), w_down=w_down[l],
                   **{name: (w_in[l], i * d_a) for i, name in enumerate(names)})
        c_all = jnp.concatenate([c_prompt, c_sample], axis=0)
        pad = (-c_all.shape[0]) % 16
        mod = _ada(jnp.pad(c_all, ((0, pad), (0, 0))), w_ada[l], b_ada[l])
        mod_p = [mod[:nb_p, i * d:(i + 1) * d] for i in range(6)]
        mod_s = [mod[nb_p:nb_p + nb_s, i * d:(i + 1) * d] for i in range(6)]
        bias = _band_bias(rel_table[l])
        cache = (cache_k_a[l], cache_v_a[l], cache_logf_a[l], cache_k_b[l], cache_v_b[l])
        y_s, *rest_s, w16 = _layer(y_s, mod_s, cache, w32, bias, nb=nb_s, tr=x_sample.shape[1])
        y_p, *rest_p, _ = _layer(y_p, mod_p, None, dict(small, **w16), bias, nb=1, tr=min(512, x_prompt.shape[1]))
        outs_p.append(rest_p)
        outs_s.append(rest_s)
    stack = lambda outs, i: jnp.stack([o[i] for o in outs])
    return (y_p, y_s, *[stack(outs_p, i) for i in range(5)], *[stack(outs_s, i) for i in range(5)])
```

```python
import functools

import jax
import jax.numpy as jnp
from jax import lax
from jax.experimental import pallas as pl
from jax.experimental.pallas import tpu as pltpu

CHUNK = 64
LEFT_CHUNKS = 8
LEFT = LEFT_CHUNKS * CHUNK
MAX_REL = 128
EPS = 1e-6
NEG = -1e30
LOG2E = 1.4426950408889634

LANES = 128
BAND_TQ = 256
BAND_WIN = LEFT + BAND_TQ
BAND_EXT = BAND_WIN + LEFT
BAND_ROLL = 2048
CUM_BLK = 256
FOX_PAD = 16
VMEM_LIMIT_BYTES = 56 * 1024 * 1024

F32 = jnp.float32
BF16 = jnp.bfloat16


def _params(*semantics):
    return pltpu.CompilerParams(dimension_semantics=semantics, vmem_limit_bytes=VMEM_LIMIT_BYTES)


def _tile(n, want):
    if n <= want:
        return n
    t = (want // LANES) * LANES
    while t >= LANES:
        if n % t == 0:
            return t
        t -= LANES
    raise ValueError(f"no lane-aligned tile for {n}")


def _ada_kernel(c_ref, w_ref, b_ref, o_ref):
    c = c_ref[...]
    a = (c * jax.nn.sigmoid(c)).astype(BF16)
    o_ref[...] = jnp.dot(a, w_ref[...].astype(BF16), preferred_element_type=F32) + b_ref[...]


def _ada(c, w_ada, b_ada):
    m, d = c.shape
    n = w_ada.shape[1]
    tn = _tile(n, 512)
    return pl.pallas_call(
        _ada_kernel,
        grid=(n // tn,),
        in_specs=[pl.BlockSpec((m, d), lambda j: (0, 0)),
                  pl.BlockSpec((d, tn), lambda j: (0, j)),
                  pl.BlockSpec((1, tn), lambda j: (0, j))],
        out_specs=pl.BlockSpec((m, tn), lambda j: (0, j)),
        out_shape=jax.ShapeDtypeStruct((m, n), F32),
        compiler_params=_params("arbitrary"),
        name="ada_mod",
    )(c, w_ada, b_ada.reshape(1, n))


def _modnorm_kernel(x_ref, sc_ref, sh_ref, g_ref, o_ref):
    x = x_ref[...]
    ms = jnp.mean(x * x, axis=-1, keepdims=True)
    y = x * lax.rsqrt(ms + EPS) * g_ref[...]
    o_ref[...] = (y * (1.0 + sc_ref[...]) + sh_ref[...]).astype(o_ref.dtype)


def _modnorm(x, scale, shift, g, nb, tr):
    b, t, d = x.shape
    bs = pl.BlockSpec((nb, 1, d), lambda i, j: (i, 0, 0))
    return pl.pallas_call(
        _modnorm_kernel,
        grid=(b // nb, t // tr),
        in_specs=[pl.BlockSpec((nb, tr, d), lambda i, j: (i, j, 0)), bs, bs,
                  pl.BlockSpec((1, 1, d), lambda i, j: (0, 0, 0))],
        out_specs=pl.BlockSpec((nb, tr, d), lambda i, j: (i, j, 0)),
        out_shape=jax.ShapeDtypeStruct((b, t, d), BF16),
        compiler_params=_params("arbitrary", "arbitrary"),
        name="modnorm",
    )(x, scale.reshape(b, 1, d), shift.reshape(b, 1, d), g.reshape(1, 1, d))


def _bf16_weights(w_ref, copy_ref, transposed=False):
    if copy_ref is None:
        return w_ref[...]
    w = w_ref[...]
    w = (w.T if transposed else w).astype(BF16)
    copy_ref[...] = w
    return w


def _proj_kernel(a_ref, w_ref, g_ref, *o_refs, do_rms, dh, nh, rows32, tiles_per_seq, emit_w):
    if emit_w:
        *o_refs, wcopy_ref = o_refs
    w = _bf16_weights(w_ref, wcopy_ref if emit_w else None, transposed=True)
    acc = jnp.dot(a_ref[...], w, preferred_element_type=F32)
    o16_ref = o_refs[-1]
    tm, tn = acc.shape
    hpt = tn // dh
    g = g_ref[...]
    ys = []
    for hh in range(hpt):
        y = acc[:, hh * dh:(hh + 1) * dh]
        if do_rms:
            ms = jnp.mean(y * y, axis=-1, keepdims=True)
            y = y * lax.rsqrt(ms + EPS) * g
        o16_ref[:, hh * dh:(hh + 1) * dh] = y.astype(o16_ref.dtype)
        ys.append(y)
    if rows32:
        o32_ref = o_refs[0]
        h0 = pl.program_id(1) * hpt

        def store32():
            for hh in range(hpt):
                o32_ref[pl.ds(h0 + hh, rows32, stride=nh), :] = ys[hh][tm - rows32:, :]

        if rows32 == tm:
            store32()
        else:
            pl.when(pl.program_id(0) % tiles_per_seq == tiles_per_seq - 1)(store32)


def _proj(a, w, col0, n, g, *, do_rms, dh, tm, tn, f32_rows=0, seq_len=None):
    m, k = a.shape
    tm = min(tm, m)
    tn = _tile(n, tn)
    nh = n // dh
    j0 = col0 // tn
    emit_w = w.dtype != BF16
    assert col0 % tn == 0 and m % tm == 0 and (not emit_w or m == tm)
    out_specs = [pl.BlockSpec((tm, tn), lambda i, j: (i, j))]
    out_shape = [jax.ShapeDtypeStruct((m, n), BF16)]
    rows32, tps = 0, 1
    if f32_rows:
        tps = seq_len // tm
        assert seq_len % tm == 0
        if f32_rows == seq_len:
            rows32 = tm
            out_specs.insert(0, pl.BlockSpec((tm * nh, dh), lambda i, j: (i, 0)))
            out_shape.insert(0, jax.ShapeDtypeStruct((m * nh, dh), F32))
        else:
            rows32 = f32_rows
            assert rows32 <= tm
            out_specs.insert(0, pl.BlockSpec((rows32 * nh, dh), lambda i, j: (i // tps, 0)))
            out_shape.insert(0, jax.ShapeDtypeStruct((m // seq_len * rows32 * nh, dh), F32))
    if emit_w:
        out_specs.append(pl.BlockSpec((k, tn), lambda i, j: (0, j)))
        out_shape.append(jax.ShapeDtypeStruct((k, n), BF16))
    return pl.pallas_call(
        functools.partial(_proj_kernel, do_rms=do_rms, dh=dh, nh=nh, rows32=rows32, tiles_per_seq=tps,
                          emit_w=emit_w),
        grid=(m // tm, n // tn),
        in_specs=[pl.BlockSpec((tm, k), lambda i, j: (i, 0)),
                  pl.BlockSpec((tn, k), lambda i, j: (j0 + j, 0)) if emit_w else
                  pl.BlockSpec((k, tn), lambda i, j: (0, j0 + j)),
                  pl.BlockSpec((1, dh), lambda i, j: (0, 0))],
        out_specs=out_specs,
        out_shape=out_shape,
        compiler_params=_params("arbitrary", "arbitrary"),
        name="proj",
    )(a, w, g.reshape(1, dh).astype(F32))


def _gate_kernel(a_ref, w_ref, b_ref, o_ref):
    z = _qk(a_ref[...], w_ref[...].astype(BF16)) + b_ref[...]
    o_ref[...] = jnp.minimum(z, 0.0) - jnp.log1p(jnp.exp(-jnp.abs(z)))


def _forget_gate(a, w_t, row0, b_f, tm):
    m, k = a.shape
    nh = b_f.shape[0]
    tm = min(tm, m)
    assert row0 % nh == 0
    return pl.pallas_call(
        _gate_kernel,
        grid=(m // tm,),
        in_specs=[pl.BlockSpec((tm, k), lambda i: (i, 0)),
                  pl.BlockSpec((nh, k), lambda i: (row0 // nh, 0)),
                  pl.BlockSpec((1, nh), lambda i: (0, 0))],
        out_specs=pl.BlockSpec((tm, nh), lambda i: (i, 0)),
        out_shape=jax.ShapeDtypeStruct((m, nh), F32),
        compiler_params=_params("arbitrary"),
        name="forget_gate",
    )(a, w_t, b_f.reshape(1, nh).astype(F32))


def _cumsum_kernel(x_ref, o_ref):
    h, tp = x_ref.shape[1], x_ref.shape[2]
    r = lax.broadcasted_iota(jnp.int32, (CUM_BLK, CUM_BLK), 0)
    c = lax.broadcasted_iota(jnp.int32, (CUM_BLK, CUM_BLK), 1)
    tri = (r <= c).astype(BF16)
    carry = jnp.zeros((h, 1), F32)
    for blk in range(tp // CUM_BLK):
        x = x_ref[0, :, blk * CUM_BLK:(blk + 1) * CUM_BLK]
        hi = x.astype(BF16)
        r1 = x - hi.astype(F32)
        mid = r1.astype(BF16)
        lo = (r1 - mid.astype(F32)).astype(BF16)
        cs = (jnp.dot(hi, tri, preferred_element_type=F32)
              + jnp.dot(mid, tri, preferred_element_type=F32)
              + jnp.dot(lo, tri, preferred_element_type=F32)) + carry
        o_ref[0, :, blk * CUM_BLK:(blk + 1) * CUM_BLK] = cs
        carry = cs[:, CUM_BLK - 1:CUM_BLK]


def _cumsum_t(logf_t):
    b, h, tp = logf_t.shape
    return pl.pallas_call(
        _cumsum_kernel,
        grid=(b,),
        in_specs=[pl.BlockSpec((1, h, tp), lambda i: (i, 0, 0))],
        out_specs=pl.BlockSpec((1, h, tp), lambda i: (i, 0, 0)),
        out_shape=jax.ShapeDtypeStruct((b, h, tp), F32),
        compiler_params=_params("arbitrary"),
        name="logf_cumsum",
    )(logf_t)


def _head_column(cc, h):
    lane = lax.broadcasted_iota(jnp.int32, cc.shape, 1)
    return jnp.sum(jnp.where(lane == h, cc, 0.0), axis=1, keepdims=True)


def _qk(q, k):
    return lax.dot_general(q, k, (((1,), (1,)), ((), ())), preferred_element_type=F32)


def _fox_kernel(q_ref, k_ref, v_ref, cc_ref, o_ref, vt_sc, ck_sc, *, tq, hb, dh):
    hg = pl.program_id(1)
    i = pl.program_id(2)
    nk = vt_sc.shape[1]
    heads = range(hb)
    hcols = [slice(hh * dh, (hh + 1) * dh) for hh in heads]

    @pl.when(i == 0)
    def _():
        ones = jnp.ones((FOX_PAD, tq), BF16)
        for kb in range(nk):
            rows = slice(kb * tq, (kb + 1) * tq)
            cc = cc_ref[0, rows, :]
            for hh in heads:
                vt_sc[hh, kb, :dh, :] = v_ref[0, rows, hcols[hh]].T
                vt_sc[hh, kb, dh:, :] = ones
                ck_sc[hh, rows, :] = jnp.broadcast_to(_head_column(cc, hg * hb + hh) * (-LOG2E), (tq, LANES))

    qts = [q_ref[0, :, cols].T for cols in hcols]

    def scores(kb):
        off = pl.multiple_of(kb * tq, tq)
        out = []
        for hh in heads:
            s = jnp.dot(k_ref[0, pl.ds(off, tq), hcols[hh]], qts[hh], preferred_element_type=F32)
            ck = ck_sc[hh, pl.ds(off, tq), :]
            out.append(s + jnp.concatenate([ck] * (tq // LANES), axis=1))
        return out

    def step(kb, state, mask=None):
        stats = []
        for s, (m, _) in zip(scores(kb), state):
            if mask is not None:
                s = jnp.where(mask, s, NEG)
            m_new = jnp.maximum(m, jnp.max(s, axis=0, keepdims=True))
            stats.append((m_new, jnp.exp2(m - m_new), jnp.exp2(s - m_new).astype(BF16)))
        return [(m_new, alpha * acc + jnp.dot(vt_sc[hh, kb], p, preferred_element_type=F32))
                for hh, ((m_new, alpha, p), (_, acc)) in enumerate(zip(stats, state))]

    state = [(jnp.full((1, tq), NEG, F32), jnp.zeros((dh + FOX_PAD, tq), F32)) for _ in heads]
    state = lax.fori_loop(0, i, step, state)
    key = lax.broadcasted_iota(jnp.int32, (tq, tq), 0)
    qry = lax.broadcasted_iota(jnp.int32, (tq, tq), 1)
    state = step(i, state, mask=key <= qry)
    for cols, (_, acc) in zip(hcols, state):
        o_ref[0, :, cols] = (acc[:dh] / acc[dh:dh + 1]).T.astype(o_ref.dtype)


def _fox_prompt(q, k, v, cum_col, nh, dh, tq, hb):
    b, t, _ = q.shape
    hb = min(hb, nh)
    assert t % tq == 0 and tq % LANES == 0
    qspec = pl.BlockSpec((1, tq, hb * dh), lambda bb, h, i: (bb, i, h))
    kspec = pl.BlockSpec((1, t, hb * dh), lambda bb, h, i: (bb, 0, h))
    return pl.pallas_call(
        functools.partial(_fox_kernel, tq=tq, hb=hb, dh=dh),
        grid=(b, nh // hb, t // tq),
        in_specs=[qspec, kspec, kspec, pl.BlockSpec((1, t, nh), lambda bb, h, i: (bb, 0, 0))],
        out_specs=qspec,
        out_shape=jax.ShapeDtypeStruct((b, t, nh * dh), BF16),
        scratch_shapes=[pltpu.VMEM((hb, t // tq, dh + FOX_PAD, tq), BF16), pltpu.VMEM((hb, t, LANES), F32)],
        compiler_params=_params("arbitrary", "arbitrary", "arbitrary"),
        name="fox_prompt",
    )(q, k, v, cum_col)


def _head_rows(ref, h, n, nh):
    return ref[0, pl.ds(h, n, stride=nh), :].astype(BF16)


def _fox_s_kernel(q_ref, kc_ref, vc_ref, kn_ref, vn_ref, cc_ref, crc_ref, crn_ref, o_ref, m_sc, l_sc, acc_sc,
                  *, nh, dh, tc, s_len):
    c = pl.program_id(1)
    cc = cc_ref[0][:s_len]
    r = lax.broadcasted_iota(jnp.int32, (s_len, s_len), 0)
    col = lax.broadcasted_iota(jnp.int32, (s_len, s_len), 1)

    @pl.when(c == 0)
    def _():
        hcols = [slice(h * dh, (h + 1) * dh) for h in range(nh)]
        ss = [_qk(q_ref[0, :, cols], kn_ref[0, :, cols]) + (cc[:, h:h + 1] - crn_ref[0, h:h + 1, :s_len])
              for h, cols in enumerate(hcols)]
        ps = []
        for h, s in enumerate(ss):
            s = jnp.where(col <= r, s, NEG)
            m = jnp.max(s, axis=1, keepdims=True)
            p = jnp.exp(s - m)
            m_sc[h] = m
            l_sc[h] = jnp.sum(p, axis=1, keepdims=True)
            ps.append(p.astype(BF16))
        for h, p in enumerate(ps):
            acc_sc[h] = jnp.dot(p, vn_ref[0, :, hcols[h]], preferred_element_type=F32)

    ss = [_qk(q_ref[0, :, h * dh:(h + 1) * dh], _head_rows(kc_ref, h, tc, nh))
          + (cc[:, h:h + 1] - crc_ref[0, h:h + 1, :]) for h in range(nh)]
    ps = []
    for h, s in enumerate(ss):
        m_old = m_sc[h]
        m_new = jnp.maximum(m_old, jnp.max(s, axis=1, keepdims=True))
        alpha = jnp.exp(m_old - m_new)
        p = jnp.exp(s - m_new)
        m_sc[h] = m_new
        l_sc[h] = alpha * l_sc[h] + jnp.sum(p, axis=1, keepdims=True)
        ps.append((alpha, p.astype(BF16)))
    for h, (alpha, p) in enumerate(ps):
        acc_sc[h] = alpha * acc_sc[h] + jnp.dot(p, _head_rows(vc_ref, h, tc, nh), preferred_element_type=F32)

    @pl.when(c == pl.num_programs(1) - 1)
    def _():
        for h in range(nh):
            o_ref[0, :, h * dh:(h + 1) * dh] = (acc_sc[h] / l_sc[h]).astype(o_ref.dtype)


def _fox_sample(q, kc, vc, kn, vn, cum_col, cum_t, nh, dh, tc):
    b, s_len, _ = q.shape
    past = kc.shape[1] // nh
    tp = cum_col.shape[1]
    tc = min(tc, past)
    assert past % CUM_BLK == 0 and tp - past == CUM_BLK and past % tc == 0 and s_len <= CUM_BLK
    new = pl.BlockSpec((1, s_len, nh * dh), lambda bb, c: (bb, 0, 0))
    cache = pl.BlockSpec((1, tc * nh, dh), lambda bb, c: (bb, c, 0))
    return pl.pallas_call(
        functools.partial(_fox_s_kernel, nh=nh, dh=dh, tc=tc, s_len=s_len),
        grid=(b, past // tc),
        in_specs=[new, cache, cache, new, new,
                  pl.BlockSpec((1, CUM_BLK, nh), lambda bb, c: (bb, past // CUM_BLK, 0)),
                  pl.BlockSpec((1, nh, tc), lambda bb, c: (bb, 0, c)),
                  pl.BlockSpec((1, nh, CUM_BLK), lambda bb, c: (bb, 0, past // CUM_BLK))],
        out_specs=new,
        out_shape=jax.ShapeDtypeStruct((b, s_len, nh * dh), BF16),
        scratch_shapes=[pltpu.VMEM((nh, s_len, 1), F32), pltpu.VMEM((nh, s_len, 1), F32),
                        pltpu.VMEM((nh, s_len, dh), F32)],
        compiler_params=_params("arbitrary", "arbitrary"),
        name="fox_sample",
    )(q, kc, vc, kn, vn, cum_col, cum_t, cum_t)


def _band_bias_kernel(gf_ref, o_ref):
    f8 = jnp.broadcast_to(gf_ref[0], (8, BAND_ROLL))
    sub = lax.broadcasted_iota(jnp.int32, (8, BAND_ROLL), 0)
    base = f8
    for bb in range(1, 8):
        base = jnp.where(sub == bb, pltpu.roll(f8, bb, axis=1), base)
    row8 = lax.broadcasted_iota(jnp.int32, (8, BAND_EXT), 0)
    col = lax.broadcasted_iota(jnp.int32, (8, BAND_EXT), 1)
    key_chunk = col // CHUNK - LEFT_CHUNKS
    for a in range(BAND_TQ // 8):
        rows = base if a == 0 else pltpu.roll(base, 8 * a, axis=1)
        rows = rows[:, :BAND_EXT]
        cd = (row8 + 8 * a) // CHUNK - key_chunk
        rows = jnp.where((cd >= 0) & (cd <= LEFT_CHUNKS), rows, NEG)
        for w in range(3):
            off = LEFT - w * BAND_TQ
            o_ref[0, w, 8 * a:8 * a + 8, :] = rows[:, off:off + BAND_WIN]


def _band_bias(rel_table):
    nh = rel_table.shape[0]
    u = jnp.arange(BAND_ROLL)
    v = jnp.where(u < BAND_ROLL - BAND_TQ, u, u - BAND_ROLL)
    idx = jnp.clip(LEFT - v, -MAX_REL, MAX_REL) + MAX_REL
    gf = jnp.take(rel_table.astype(F32), idx, axis=1).reshape(nh, 1, BAND_ROLL)
    return pl.pallas_call(
        _band_bias_kernel,
        grid=(nh,),
        in_specs=[pl.BlockSpec((1, 1, BAND_ROLL), lambda h: (h, 0, 0))],
        out_specs=pl.BlockSpec((1, 3, BAND_TQ, BAND_WIN), lambda h: (h, 0, 0, 0)),
        out_shape=jax.ShapeDtypeStruct((nh, 3, BAND_TQ, BAND_WIN), F32),
        compiler_params=_params("arbitrary"),
        name="band_bias",
    )(gf)


def _band_bias_t_kernel(gf_ref, o_ref):
    f8 = jnp.broadcast_to(gf_ref[0], (8, BAND_ROLL))
    sub = lax.broadcasted_iota(jnp.int32, (8, BAND_ROLL), 0)
    base = f8
    for bb in range(1, 8):
        base = jnp.where(sub == bb, pltpu.roll(f8, bb, axis=1), base)
    key8 = lax.broadcasted_iota(jnp.int32, (8, BAND_TQ), 0)
    qry_chunk = lax.broadcasted_iota(jnp.int32, (8, BAND_TQ), 1) // CHUNK
    for a in range(BAND_EXT // 8):
        rows = base if a == 0 else pltpu.roll(base, 8 * a, axis=1)
        cd = qry_chunk - ((key8 + 8 * a) // CHUNK - LEFT_CHUNKS)
        rows = jnp.where((cd >= 0) & (cd <= LEFT_CHUNKS), rows[:, :BAND_TQ], NEG)
        for w in range(3):
            r0 = 8 * a - (LEFT - w * BAND_TQ)
            if 0 <= r0 < BAND_WIN:
                o_ref[0, w, r0:r0 + 8, :] = rows


def _band_bias_t(rel_table):
    nh = rel_table.shape[0]
    u = jnp.arange(BAND_ROLL)
    v = jnp.where(u < 2 * BAND_TQ, u, u - BAND_ROLL)
    idx = jnp.clip(v + LEFT, -MAX_REL, MAX_REL) + MAX_REL
    gf = (jnp.take(rel_table.astype(F32), idx, axis=1) * LOG2E).reshape(nh, 1, BAND_ROLL)
    return pl.pallas_call(
        _band_bias_t_kernel,
        grid=(nh,),
        in_specs=[pl.BlockSpec((1, 1, BAND_ROLL), lambda h: (h, 0, 0))],
        out_specs=pl.BlockSpec((1, 3, BAND_WIN, BAND_TQ), lambda h: (h, 0, 0, 0)),
        out_shape=jax.ShapeDtypeStruct((nh, 3, BAND_WIN, BAND_TQ), F32),
        compiler_params=_params("arbitrary"),
        name="band_bias_t",
    )(gf)


def _softmax_pv(score_parts, value_parts):
    probs = []
    for parts in score_parts:
        m = functools.reduce(jnp.maximum, [jnp.max(s, axis=1, keepdims=True) for s in parts])
        ps = [jnp.exp(s - m) for s in parts]
        l = functools.reduce(jnp.add, [jnp.sum(p, axis=1, keepdims=True) for p in ps])
        probs.append(([p.astype(BF16) for p in ps], l))
    outs = []
    for (ps, l), vals in zip(probs, value_parts):
        o = functools.reduce(jnp.add, [jnp.dot(p, v(), preferred_element_type=F32) for p, v in zip(ps, vals)])
        outs.append(o / l)
    return outs


def _band_kernel(q_ref, k_ref, v_ref, e_ref, o_ref, vt_sc, *, hb, dh):
    g = pl.program_id(2)
    nk = vt_sc.shape[1]
    nwin = BAND_WIN // BAND_TQ
    hcols = [slice(hh * dh, (hh + 1) * dh) for hh in range(hb)]

    @pl.when(g == 0)
    def _():
        ones = jnp.ones((FOX_PAD, BAND_TQ), BF16)
        for kb in range(nk):
            for hh, cols in enumerate(hcols):
                vt_sc[hh, kb, :dh, :] = v_ref[0, kb * BAND_TQ:(kb + 1) * BAND_TQ, cols].T
                vt_sc[hh, kb, dh:, :] = ones

    wb = jnp.maximum(g - LEFT // BAND_TQ, 0)
    ws = pl.multiple_of(wb * BAND_TQ, BAND_TQ)
    w = jnp.minimum(g, LEFT // BAND_TQ)
    ss = [jnp.dot(k_ref[0, pl.ds(ws, BAND_WIN), cols], q_ref[0, :, cols].T, preferred_element_type=F32)
          + e_ref[hh, w] for hh, cols in enumerate(hcols)]
    ps = [jnp.exp2(s - jnp.max(s, axis=0, keepdims=True)).astype(BF16) for s in ss]
    for hh, (cols, p) in enumerate(zip(hcols, ps)):
        acc = functools.reduce(jnp.add, [
            jnp.dot(vt_sc[hh, wb + j], p[j * BAND_TQ:(j + 1) * BAND_TQ], preferred_element_type=F32)
            for j in range(nwin)])
        o_ref[0, :, cols] = (acc[:dh] / acc[dh:dh + 1]).T.astype(o_ref.dtype)


def _band_prompt(q, k, v, bias_t, nh, dh, hb):
    b, t, _ = q.shape
    assert t % BAND_TQ == 0 and t >= BAND_WIN
    hb = min(hb, nh)
    qspec = pl.BlockSpec((1, BAND_TQ, hb * dh), lambda h, bb, g: (bb, g, h))
    kspec = pl.BlockSpec((1, t, hb * dh), lambda h, bb, g: (bb, 0, h))
    return pl.pallas_call(
        functools.partial(_band_kernel, hb=hb, dh=dh),
        grid=(nh // hb, b, t // BAND_TQ),
        in_specs=[qspec, kspec, kspec,
                  pl.BlockSpec((hb, 3, BAND_WIN, BAND_TQ), lambda h, bb, g: (h, 0, 0, 0),
                               pipeline_mode=pl.Buffered(1))],
        out_specs=qspec,
        out_shape=jax.ShapeDtypeStruct((b, t, nh * dh), BF16),
        scratch_shapes=[pltpu.VMEM((hb, t // BAND_TQ, dh + FOX_PAD, BAND_TQ), BF16)],
        compiler_params=_params("arbitrary", "arbitrary", "arbitrary"),
        name="band_prompt",
    )(q, k, v, bias_t)


def _band_s_kernel(q_ref, kc_ref, vc_ref, kn_ref, vn_ref, e_ref, o_ref, *, nh, dh, keep, s_len):
    hcols = [slice(h * dh, (h + 1) * dh) for h in range(nh)]
    scores, values = [], []
    for h, cols in enumerate(hcols):
        q = q_ref[0, :, cols]
        e = e_ref[h, 0]
        scores.append([_qk(q, _head_rows(kc_ref, h, keep, nh)) + e[:, :keep],
                       _qk(q, kn_ref[0, :, cols]) + e[:, keep:keep + s_len]])
        values.append([functools.partial(_head_rows, vc_ref, h, keep, nh),
                       functools.partial(lambda cols: vn_ref[0, :, cols], cols)])
    for cols, o in zip(hcols, _softmax_pv(scores, values)):
        o_ref[0, :, cols] = o.astype(o_ref.dtype)


def _band_sample(q, kc, vc, kn, vn, bias, nh, dh):
    b, s_len, _ = q.shape
    keep = kc.shape[1] // nh
    assert keep == LEFT and s_len <= CHUNK and s_len % 8 == 0
    new = pl.BlockSpec((1, s_len, nh * dh), lambda bb: (bb, 0, 0))
    cache = pl.BlockSpec((1, keep * nh, dh), lambda bb: (bb, 0, 0))
    return pl.pallas_call(
        functools.partial(_band_s_kernel, nh=nh, dh=dh, keep=keep, s_len=s_len),
        grid=(b,),
        in_specs=[new, cache, cache, new, new,
                  pl.BlockSpec((nh, 1, s_len, BAND_WIN), lambda bb: (0, LEFT // BAND_TQ, 0, 0))],
        out_specs=new,
        out_shape=jax.ShapeDtypeStruct((b, s_len, nh * dh), BF16),
        compiler_params=_params("arbitrary"),
        name="band_sample",
    )(q, kc, vc, kn, vn, bias)


def _oproj_kernel(oa_ref, ob_ref, wa_ref, wb_ref, x_ref, gate_ref, o_ref, *wcopy_refs):
    ca, cb = wcopy_refs if wcopy_refs else (None, None)
    acc = (jnp.dot(oa_ref[...], _bf16_weights(wa_ref, ca), preferred_element_type=F32)
           + jnp.dot(ob_ref[...], _bf16_weights(wb_ref, cb), preferred_element_type=F32))
    o_ref[...] = x_ref[...] + gate_ref[...] * acc


def _gate_spec(gate, m, tm, tn):
    if gate.ndim == 2:
        return pl.BlockSpec((tm, tn), lambda i, j: (i, j))
    tiles_per_seq = m // gate.shape[0] // tm
    return pl.BlockSpec((None, 1, tn), lambda i, j: (i // tiles_per_seq, 0, j))


def _oproj(oa, ob, wa, wb, x, gate, tm, tn):
    m, da = oa.shape
    db = ob.shape[1]
    (wa, ra), (wb, rb) = wa, wb
    d = wa.shape[1]
    tm = min(tm, m)
    tn = _tile(d, tn)
    emit_w = wa.dtype != BF16
    assert da == db and (not emit_w or m == tm)
    out_specs = [pl.BlockSpec((tm, tn), lambda i, j: (i, j))]
    out_shape = [jax.ShapeDtypeStruct((m, d), F32)]
    if emit_w:
        out_specs += [pl.BlockSpec((da, tn), lambda i, j: (0, j))] * 2
        out_shape += [jax.ShapeDtypeStruct((da, d), BF16)] * 2
    return pl.pallas_call(
        _oproj_kernel,
        grid=(m // tm, d // tn),
        in_specs=[pl.BlockSpec((tm, da), lambda i, j: (i, 0)),
                  pl.BlockSpec((tm, db), lambda i, j: (i, 0)),
                  pl.BlockSpec((da, tn), lambda i, j: (ra, j)),
                  pl.BlockSpec((db, tn), lambda i, j: (rb, j)),
                  pl.BlockSpec((tm, tn), lambda i, j: (i, j)),
                  _gate_spec(gate, m, tm, tn)],
        out_specs=out_specs,
        out_shape=out_shape,
        compiler_params=_params("arbitrary", "arbitrary"),
        name="out_proj",
    )(oa, ob, wa, wb, x, gate)


def _ffn_up_kernel(a_ref, wg_ref, wu_ref, o_ref, *wcopy_refs):
    cg, cu = wcopy_refs if wcopy_refs else (None, None)
    a = a_ref[...]
    g = jnp.dot(a, _bf16_weights(wg_ref, cg), preferred_element_type=F32)
    u = jnp.dot(a, _bf16_weights(wu_ref, cu), preferred_element_type=F32)
    o_ref[...] = (g * jax.nn.sigmoid(g) * u).astype(o_ref.dtype)


def _ffn_up(a, wg, wu, f, tm, tn):
    m, k = a.shape
    (wg, cg), (wu, cu) = wg, wu
    tm = min(tm, m)
    tn = _tile(f, tn)
    jg, ju = cg // tn, cu // tn
    emit_w = wg.dtype != BF16
    assert cg % tn == 0 and cu % tn == 0 and (not emit_w or m == tm)
    out_specs = [pl.BlockSpec((tm, tn), lambda i, j: (i, j))]
    out_shape = [jax.ShapeDtypeStruct((m, f), BF16)]
    if emit_w:
        out_specs += [pl.BlockSpec((k, tn), lambda i, j: (0, j))] * 2
        out_shape += [jax.ShapeDtypeStruct((k, f), BF16)] * 2
    return pl.pallas_call(
        _ffn_up_kernel,
        grid=(m // tm, f // tn),
        in_specs=[pl.BlockSpec((tm, k), lambda i, j: (i, 0)),
                  pl.BlockSpec((k, tn), lambda i, j: (0, jg + j)),
                  pl.BlockSpec((k, tn), lambda i, j: (0, ju + j))],
        out_specs=out_specs,
        out_shape=out_shape,
        compiler_params=_params("arbitrary", "arbitrary"),
        name="ffn_up",
    )(a, wg, wu)


def _ffn_down_kernel(a_ref, w_ref, x_ref, gate_ref, o_ref, *wcopy_refs):
    w = _bf16_weights(w_ref, wcopy_refs[0] if wcopy_refs else None)
    acc = jnp.dot(a_ref[...], w, preferred_element_type=F32)
    o_ref[...] = x_ref[...] + gate_ref[...] * acc


def _ffn_down(a, w_down, x, gate, tm, tn):
    m, f = a.shape
    d = w_down.shape[1]
    tm = min(tm, m)
    tn = _tile(d, tn)
    emit_w = w_down.dtype != BF16
    assert not emit_w or m == tm
    out_specs = [pl.BlockSpec((tm, tn), lambda i, j: (i, j))]
    out_shape = [jax.ShapeDtypeStruct((m, d), F32)]
    if emit_w:
        out_specs.append(pl.BlockSpec((f, tn), lambda i, j: (0, j)))
        out_shape.append(jax.ShapeDtypeStruct((f, d), BF16))
    return pl.pallas_call(
        _ffn_down_kernel,
        grid=(m // tm, d // tn),
        in_specs=[pl.BlockSpec((tm, f), lambda i, j: (i, 0)),
                  pl.BlockSpec((f, tn), lambda i, j: (0, j)),
                  pl.BlockSpec((tm, tn), lambda i, j: (i, j)),
                  _gate_spec(gate, m, tm, tn)],
        out_specs=out_specs,
        out_shape=out_shape,
        compiler_params=_params("arbitrary", "arbitrary"),
        name="ffn_down",
    )(a, w_down, x, gate)


def _layer(x, mod, cache, w, bias, *, nb, tr):
    b, t, d = x.shape
    m = b * t
    shift_a, scale_a, gate_a, shift_f, scale_f, gate_f = mod
    dh = w["g_q_a"].shape[0]
    nh_a = w["b_f"].shape[0]
    d_a = d_b = w["d_a"]
    nh_b = d_b // dh
    f = w["f"]
    scale = dh ** -0.5
    prompt = cache is None
    tm = 1024
    w16 = {}

    def gates(gt):
        if prompt:
            return gt.reshape(b, 1, d)
        return jnp.broadcast_to(gt[:, None, :], (b, t, d)).reshape(m, d)

    h = _modnorm(x, scale_a, shift_a, w["g_attn"], nb, tr).reshape(m, d)
    keep_p = min(LEFT, t)
    full32 = dict(tm=tm, tn=512, f32_rows=t, seq_len=t) if prompt else dict(tm=tm, tn=512, f32_rows=m, seq_len=m)
    tail32 = dict(tm=tm, tn=1024, f32_rows=keep_p, seq_len=t) if prompt else full32

    def pj(name, n, g, **kw):
        arr, col0 = w[name]
        kw.setdefault("tn", 1024 if prompt else 512)
        outs = _proj(h, arr, col0, n, g, dh=dh, **kw)
        if not prompt:
            *outs, wcopy = outs
            w16[name] = (wcopy, 0)
        return outs

    (q_a,) = pj("w_q_a", d_a, w["g_q_a"] * (scale * LOG2E if prompt else scale), do_rms=True, tm=tm)
    k_a32, k_a = pj("w_k_a", d_a, w["g_k_a"], do_rms=True, **full32)
    v_a32, v_a = pj("w_v_a", d_a, w["g_k_a"], do_rms=False, **full32)
    (q_b,) = pj("w_q_b", d_b, w["g_q_b"] * (scale * LOG2E if prompt else scale), do_rms=True, tm=tm)
    k_b32, k_b = pj("w_k_b", d_b, w["g_k_b"], do_rms=True, **tail32)
    v_b32, v_b = pj("w_v_b", d_b, w["g_k_b"], do_rms=False, **tail32)
    logf = _forget_gate(h, *w["w_f"], w["b_f"], tm).reshape(b, t, nh_a)

    r3 = lambda z: z.reshape(b, t, -1)
    if prompt:
        cum_t = _cumsum_t(logf.transpose(0, 2, 1))
        tq = 256
        o_a = _fox_prompt(r3(q_a), r3(k_a), r3(v_a), cum_t.transpose(0, 2, 1), nh_a, dh, tq, 8)
        o_b = _band_prompt(r3(q_b), r3(k_b), r3(v_b), bias[1], nh_b, dh, 8)
        new_k_b, new_v_b = k_b32.reshape(b, keep_p, nh_b, dh), v_b32.reshape(b, keep_p, nh_b, dh)
    else:
        ck_a, cv_a, clogf_a, ck_b, cv_b = cache
        past = ck_a.shape[1]
        tp = past + CUM_BLK
        lf_all = jnp.concatenate([clogf_a.astype(F32), logf, jnp.zeros((b, tp - past - t, nh_a), F32)], axis=1)
        cum_t = _cumsum_t(lf_all.transpose(0, 2, 1))
        o_a = _fox_sample(r3(q_a), ck_a.reshape(b, past * nh_a, dh), cv_a.reshape(b, past * nh_a, dh),
                          r3(k_a), r3(v_a), cum_t.transpose(0, 2, 1), cum_t, nh_a, dh, 1024)
        keep = ck_b.shape[1]
        o_b = _band_sample(r3(q_b), ck_b.reshape(b, keep * nh_b, dh), cv_b.reshape(b, keep * nh_b, dh),
                           r3(k_b), r3(v_b), bias[0], nh_b, dh)
        new_b = lambda ck, z: jnp.concatenate([ck, z.reshape(b, t, nh_b, dh).astype(ck.dtype)], axis=1)[:, -keep:]
        new_k_b, new_v_b = new_b(ck_b, k_b32), new_b(cv_b, v_b32)

    x1, *wo16 = _oproj(o_a.reshape(m, d_a), o_b.reshape(m, d_b), w["w_oa"], w["w_ob"], x.reshape(m, d),
                       gates(gate_a), tm, 1024 if prompt else 512)
    h2 = _modnorm(x1.reshape(b, t, d), scale_f, shift_f, w["g_ffn"], nb, tr).reshape(m, d)
    act, *wgu16 = _ffn_up(h2, w["w_g"], w["w_u"], f, tm, 256)
    y, *wd16 = _ffn_down(act, w["w_down"], x1, gates(gate_f), 512, 512 if prompt else 128)
    if not prompt:
        w16.update(w_oa=(wo16[0], 0), w_ob=(wo16[1], 0), w_g=(wgu16[0], 0), w_u=(wgu16[1], 0), w_down=wd16[0])
    return (y.reshape(b, t, d), k_a32.reshape(b, t, nh_a, dh), v_a32.reshape(b, t, nh_a, dh), logf,
            new_k_b, new_v_b, w16)


def kernel(x_prompt, x_sample, cache_k_a, cache_v_a, cache_logf_a, cache_k_b, cache_v_b, c_prompt, c_sample,
           w_ada, b_ada, g_attn, g_ffn, w_in, b_f, g_q_a, g_k_a, g_q_b, g_k_b, rel_table, w_o, w_gu, w_down):
    depth = w_ada.shape[0]
    d = x_prompt.shape[-1]
    nb_p, nb_s = c_prompt.shape[0], c_sample.shape[0]
    nh_a = b_f.shape[1]
    n_qkv = w_in.shape[2] - nh_a
    d_a = n_qkv // 6
    f = w_down.shape[1]
    assert 2 * d_a == w_o.shape[1] and w_gu.shape[2] == 2 * f
    y_p, y_s = x_prompt, x_sample
    outs_p, outs_s = [], []
    for l in range(depth):
        w_in_t = jnp.swapaxes(w_in[l], 0, 1)
        small = dict(w_f=(w_in_t, n_qkv), b_f=b_f[l], g_attn=g_attn[l], g_ffn=g_ffn[l], g_q_a=g_q_a[l], g_k_a=g_k_a[l],
                     g_q_b=g_q_b[l], g_k_b=g_k_b[l], d_a=d_a, f=f)
        names = ("w_q_a", "w_k_a", "w_v_a", "w_q_b", "w_k_b", "w_v_b")
        w32 = dict(small, w_oa=(w_o[l], 0), w_ob=(w_o[l], 1), w_g=(w_gu[l], 0), w_u=(w_gu[l], f), w_down=w_down[l],
                   **{name: (w_in_t, i * d_a) for i, name in enumerate(names)})
        c_all = jnp.concatenate([c_prompt, c_sample], axis=0)
        pad = (-c_all.shape[0]) % 16
        mod = _ada(jnp.pad(c_all, ((0, pad), (0, 0))), w_ada[l], b_ada[l])
        mod_p = [mod[:nb_p, i * d:(i + 1) * d] for i in range(6)]
        mod_s = [mod[nb_p:nb_p + nb_s, i * d:(i + 1) * d] for i in range(6)]
        bias = (_band_bias(rel_table[l]), _band_bias_t(rel_table[l]))
        cache = (cache_k_a[l], cache_v_a[l], cache_logf_a[l], cache_k_b[l], cache_v_b[l])
        y_s, *rest_s, w16 = _layer(y_s, mod_s, cache, w32, bias, nb=nb_s, tr=x_sample.shape[1])
        y_p, *rest_p, _ = _layer(y_p, mod_p, None, dict(small, **w16), bias, nb=1, tr=min(512, x_prompt.shape[1]))
        outs_p.append(rest_p)
        outs_s.append(rest_s)
    stack = lambda outs, i: jnp.stack([o[i] for o in outs])
    return (y_p, y_s, *[stack(outs_p, i) for i in range(5)], *[stack(outs_s, i) for i in range(5)])
```

```python
import functools

import jax
import jax.numpy as jnp
from jax import lax
from jax.experimental import pallas as pl
from jax.experimental.pallas import tpu as pltpu

CHUNK = 64
LEFT_CHUNKS = 8
LEFT = LEFT_CHUNKS * CHUNK
MAX_REL = 128
EPS = 1e-6
NEG = -1e30
LOG2E = 1.4426950408889634

LANES = 128
BAND_TQ = 256
BAND_WIN = LEFT + BAND_TQ
BAND_EXT = BAND_WIN + LEFT
BAND_ROLL = 2048
CUM_BLK = 256
FOX_PAD = 16
VMEM_LIMIT_BYTES = 56 * 1024 * 1024

F32 = jnp.float32
BF16 = jnp.bfloat16


def _params(*semantics):
    return pltpu.CompilerParams(dimension_semantics=semantics, vmem_limit_bytes=VMEM_LIMIT_BYTES)


def _tile(n, want):
    if n <= want:
        return n
    t = (want // LANES) * LANES
    while t >= LANES:
        if n % t == 0:
            return t
        t -= LANES
    raise ValueError(f"no lane-aligned tile for {n}")


def _ada_kernel(c_ref, w_ref, b_ref, o_ref):
    c = c_ref[...]
    a = (c * jax.nn.sigmoid(c)).astype(BF16)
    o_ref[...] = jnp.dot(a, w_ref[...].astype(BF16), preferred_element_type=F32) + b_ref[...]


def _ada(c, w_ada, b_ada):
    m, d = c.shape
    n = w_ada.shape[1]
    tn = _tile(n, 512)
    return pl.pallas_call(
        _ada_kernel,
        grid=(n // tn,),
        in_specs=[pl.BlockSpec((m, d), lambda j: (0, 0)),
                  pl.BlockSpec((d, tn), lambda j: (0, j)),
                  pl.BlockSpec((1, tn), lambda j: (0, j))],
        out_specs=pl.BlockSpec((m, tn), lambda j: (0, j)),
        out_shape=jax.ShapeDtypeStruct((m, n), F32),
        compiler_params=_params("arbitrary"),
        name="ada_mod",
    )(c, w_ada, b_ada.reshape(1, n))


def _modnorm_kernel(x_ref, sc_ref, sh_ref, g_ref, o_ref):
    x = x_ref[...]
    ms = jnp.mean(x * x, axis=-1, keepdims=True)
    y = x * lax.rsqrt(ms + EPS) * g_ref[...]
    o_ref[...] = (y * (1.0 + sc_ref[...]) + sh_ref[...]).astype(o_ref.dtype)


def _modnorm(x, scale, shift, g, nb, tr):
    b, t, d = x.shape
    bs = pl.BlockSpec((nb, 1, d), lambda i, j: (i, 0, 0))
    return pl.pallas_call(
        _modnorm_kernel,
        grid=(b // nb, t // tr),
        in_specs=[pl.BlockSpec((nb, tr, d), lambda i, j: (i, j, 0)), bs, bs,
                  pl.BlockSpec((1, 1, d), lambda i, j: (0, 0, 0))],
        out_specs=pl.BlockSpec((nb, tr, d), lambda i, j: (i, j, 0)),
        out_shape=jax.ShapeDtypeStruct((b, t, d), BF16),
        compiler_params=_params("arbitrary", "arbitrary"),
        name="modnorm",
    )(x, scale.reshape(b, 1, d), shift.reshape(b, 1, d), g.reshape(1, 1, d))


def _bf16_weights(w_ref, copy_ref, transposed=False):
    if copy_ref is None:
        return w_ref[...]
    w = w_ref[...]
    w = (w.T if transposed else w).astype(BF16)
    copy_ref[...] = w
    return w


def _proj_kernel(a_ref, w_ref, g_ref, *o_refs, do_rms, dh, nh, rows32, tiles_per_seq, emit_w):
    if emit_w:
        *o_refs, wcopy_ref = o_refs
    w = _bf16_weights(w_ref, wcopy_ref if emit_w else None, transposed=True)
    acc = jnp.dot(a_ref[...], w, preferred_element_type=F32)
    o16_ref = o_refs[-1]
    tm, tn = acc.shape
    hpt = tn // dh
    g = g_ref[...]
    ys = []
    for hh in range(hpt):
        y = acc[:, hh * dh:(hh + 1) * dh]
        if do_rms:
            ms = jnp.mean(y * y, axis=-1, keepdims=True)
            y = y * lax.rsqrt(ms + EPS) * g
        o16_ref[:, hh * dh:(hh + 1) * dh] = y.astype(o16_ref.dtype)
        ys.append(y)
    if rows32:
        o32_ref = o_refs[0]
        h0 = pl.program_id(1) * hpt

        def store32():
            for hh in range(hpt):
                o32_ref[pl.ds(h0 + hh, rows32, stride=nh), :] = ys[hh][tm - rows32:, :]

        if rows32 == tm:
            store32()
        else:
            pl.when(pl.program_id(0) % tiles_per_seq == tiles_per_seq - 1)(store32)


def _proj(a, w, col0, n, g, *, do_rms, dh, tm, tn, f32_rows=0, seq_len=None):
    m, k = a.shape
    tm = min(tm, m)
    tn = _tile(n, tn)
    nh = n // dh
    j0 = col0 // tn
    emit_w = w.dtype != BF16
    assert col0 % tn == 0 and m % tm == 0 and (not emit_w or m == tm)
    out_specs = [pl.BlockSpec((tm, tn), lambda i, j: (i, j))]
    out_shape = [jax.ShapeDtypeStruct((m, n), BF16)]
    rows32, tps = 0, 1
    if f32_rows:
        tps = seq_len // tm
        assert seq_len % tm == 0
        if f32_rows == seq_len:
            rows32 = tm
            out_specs.insert(0, pl.BlockSpec((tm * nh, dh), lambda i, j: (i, 0)))
            out_shape.insert(0, jax.ShapeDtypeStruct((m * nh, dh), F32))
        else:
            rows32 = f32_rows
            assert rows32 <= tm
            out_specs.insert(0, pl.BlockSpec((rows32 * nh, dh), lambda i, j: (i // tps, 0)))
            out_shape.insert(0, jax.ShapeDtypeStruct((m // seq_len * rows32 * nh, dh), F32))
    if emit_w:
        out_specs.append(pl.BlockSpec((k, tn), lambda i, j: (0, j)))
        out_shape.append(jax.ShapeDtypeStruct((k, n), BF16))
    return pl.pallas_call(
        functools.partial(_proj_kernel, do_rms=do_rms, dh=dh, nh=nh, rows32=rows32, tiles_per_seq=tps,
                          emit_w=emit_w),
        grid=(m // tm, n // tn),
        in_specs=[pl.BlockSpec((tm, k), lambda i, j: (i, 0)),
                  pl.BlockSpec((tn, k), lambda i, j: (j0 + j, 0)) if emit_w else
                  pl.BlockSpec((k, tn), lambda i, j: (0, j0 + j)),
                  pl.BlockSpec((1, dh), lambda i, j: (0, 0))],
        out_specs=out_specs,
        out_shape=out_shape,
        compiler_params=_params("arbitrary", "arbitrary"),
        name="proj",
    )(a, w, g.reshape(1, dh).astype(F32))


def _gate_kernel(a_ref, w_ref, b_ref, o_ref):
    z = _qk(a_ref[...], w_ref[...].astype(BF16)) + b_ref[...]
    o_ref[...] = jnp.minimum(z, 0.0) - jnp.log1p(jnp.exp(-jnp.abs(z)))


def _forget_gate(a, w_t, row0, b_f, tm):
    m, k = a.shape
    nh = b_f.shape[0]
    tm = min(tm, m)
    assert row0 % nh == 0
    return pl.pallas_call(
        _gate_kernel,
        grid=(m // tm,),
        in_specs=[pl.BlockSpec((tm, k), lambda i: (i, 0)),
                  pl.BlockSpec((nh, k), lambda i: (row0 // nh, 0)),
                  pl.BlockSpec((1, nh), lambda i: (0, 0))],
        out_specs=pl.BlockSpec((tm, nh), lambda i: (i, 0)),
        out_shape=jax.ShapeDtypeStruct((m, nh), F32),
        compiler_params=_params("arbitrary"),
        name="forget_gate",
    )(a, w_t, b_f.reshape(1, nh).astype(F32))


def _cumsum_kernel(x_ref, o_ref):
    h, tp = x_ref.shape[1], x_ref.shape[2]
    r = lax.broadcasted_iota(jnp.int32, (CUM_BLK, CUM_BLK), 0)
    c = lax.broadcasted_iota(jnp.int32, (CUM_BLK, CUM_BLK), 1)
    tri = (r <= c).astype(BF16)
    carry = jnp.zeros((h, 1), F32)
    for blk in range(tp // CUM_BLK):
        x = x_ref[0, :, blk * CUM_BLK:(blk + 1) * CUM_BLK]
        hi = x.astype(BF16)
        r1 = x - hi.astype(F32)
        mid = r1.astype(BF16)
        lo = (r1 - mid.astype(F32)).astype(BF16)
        cs = (jnp.dot(hi, tri, preferred_element_type=F32)
              + jnp.dot(mid, tri, preferred_element_type=F32)
              + jnp.dot(lo, tri, preferred_element_type=F32)) + carry
        o_ref[0, :, blk * CUM_BLK:(blk + 1) * CUM_BLK] = cs
        carry = cs[:, CUM_BLK - 1:CUM_BLK]


def _cumsum_t(logf_t):
    b, h, tp = logf_t.shape
    return pl.pallas_call(
        _cumsum_kernel,
        grid=(b,),
        in_specs=[pl.BlockSpec((1, h, tp), lambda i: (i, 0, 0))],
        out_specs=pl.BlockSpec((1, h, tp), lambda i: (i, 0, 0)),
        out_shape=jax.ShapeDtypeStruct((b, h, tp), F32),
        compiler_params=_params("arbitrary"),
        name="logf_cumsum",
    )(logf_t)


def _head_column(cc, h):
    lane = lax.broadcasted_iota(jnp.int32, cc.shape, 1)
    return jnp.sum(jnp.where(lane == h, cc, 0.0), axis=1, keepdims=True)


def _qk(q, k):
    return lax.dot_general(q, k, (((1,), (1,)), ((), ())), preferred_element_type=F32)


def _fox_kernel(q_ref, k_ref, v_ref, cc_ref, o_ref, vt_sc, ck_sc, *, tq, hb, dh):
    hg = pl.program_id(1)
    i = pl.program_id(2)
    nk = vt_sc.shape[1]
    heads = range(hb)
    hcols = [slice(hh * dh, (hh + 1) * dh) for hh in heads]

    @pl.when(i == 0)
    def _():
        ones = jnp.ones((FOX_PAD, tq), BF16)
        for kb in range(nk):
            rows = slice(kb * tq, (kb + 1) * tq)
            cc = cc_ref[0, rows, :]
            for hh in heads:
                vt_sc[hh, kb, :dh, :] = v_ref[0, rows, hcols[hh]].T
                vt_sc[hh, kb, dh:, :] = ones
                ck_sc[hh, rows, :] = jnp.broadcast_to(_head_column(cc, hg * hb + hh) * (-LOG2E), (tq, LANES))

    qts = [q_ref[0, :, cols].T for cols in hcols]

    def scores(kb):
        off = pl.multiple_of(kb * tq, tq)
        out = []
        for hh in heads:
            s = jnp.dot(k_ref[0, pl.ds(off, tq), hcols[hh]], qts[hh], preferred_element_type=F32)
            ck = ck_sc[hh, pl.ds(off, tq), :]
            out.append(s + jnp.concatenate([ck] * (tq // LANES), axis=1))
        return out

    def step(kb, state, mask=None):
        stats = []
        for s, (m, _) in zip(scores(kb), state):
            if mask is not None:
                s = jnp.where(mask, s, NEG)
            m_new = jnp.maximum(m, jnp.max(s, axis=0, keepdims=True))
            stats.append((m_new, jnp.exp2(m - m_new), jnp.exp2(s - m_new).astype(BF16)))
        return [(m_new, alpha * acc + jnp.dot(vt_sc[hh, kb], p, preferred_element_type=F32))
                for hh, ((m_new, alpha, p), (_, acc)) in enumerate(zip(stats, state))]

    state = [(jnp.full((1, tq), NEG, F32), jnp.zeros((dh + FOX_PAD, tq), F32)) for _ in heads]
    state = lax.fori_loop(0, i, step, state)
    key = lax.broadcasted_iota(jnp.int32, (tq, tq), 0)
    qry = lax.broadcasted_iota(jnp.int32, (tq, tq), 1)
    state = step(i, state, mask=key <= qry)
    for cols, (_, acc) in zip(hcols, state):
        o_ref[0, :, cols] = (acc[:dh] / acc[dh:dh + 1]).T.astype(o_ref.dtype)


def _fox_prompt(q, k, v, cum_col, nh, dh, tq, hb):
    b, t, _ = q.shape
    hb = min(hb, nh)
    assert t % tq == 0 and tq % LANES == 0
    qspec = pl.BlockSpec((1, tq, hb * dh), lambda bb, h, i: (bb, i, h))
    kspec = pl.BlockSpec((1, t, hb * dh), lambda bb, h, i: (bb, 0, h))
    return pl.pallas_call(
        functools.partial(_fox_kernel, tq=tq, hb=hb, dh=dh),
        grid=(b, nh // hb, t // tq),
        in_specs=[qspec, kspec, kspec, pl.BlockSpec((1, t, nh), lambda bb, h, i: (bb, 0, 0))],
        out_specs=qspec,
        out_shape=jax.ShapeDtypeStruct((b, t, nh * dh), BF16),
        scratch_shapes=[pltpu.VMEM((hb, t // tq, dh + FOX_PAD, tq), BF16), pltpu.VMEM((hb, t, LANES), F32)],
        compiler_params=_params("arbitrary", "arbitrary", "arbitrary"),
        name="fox_prompt",
    )(q, k, v, cum_col)


def _head_rows(ref, h, n, nh):
    return ref[0, pl.ds(h, n, stride=nh), :].astype(BF16)


def _fox_s_kernel(q_ref, kc_ref, vc_ref, kn_ref, vn_ref, cc_ref, crc_ref, crn_ref, o_ref, m_sc, l_sc, acc_sc,
                  *, nh, dh, tc, s_len):
    c = pl.program_id(1)
    cc = cc_ref[0][:s_len]
    r = lax.broadcasted_iota(jnp.int32, (s_len, s_len), 0)
    col = lax.broadcasted_iota(jnp.int32, (s_len, s_len), 1)

    @pl.when(c == 0)
    def _():
        hcols = [slice(h * dh, (h + 1) * dh) for h in range(nh)]
        ss = [_qk(q_ref[0, :, cols], kn_ref[0, :, cols]) + (cc[:, h:h + 1] - crn_ref[0, h:h + 1, :s_len])
              for h, cols in enumerate(hcols)]
        ps = []
        for h, s in enumerate(ss):
            s = jnp.where(col <= r, s, NEG)
            m = jnp.max(s, axis=1, keepdims=True)
            p = jnp.exp(s - m)
            m_sc[h] = m
            l_sc[h] = jnp.sum(p, axis=1, keepdims=True)
            ps.append(p.astype(BF16))
        for h, p in enumerate(ps):
            acc_sc[h] = jnp.dot(p, vn_ref[0, :, hcols[h]], preferred_element_type=F32)

    ss = [_qk(q_ref[0, :, h * dh:(h + 1) * dh], _head_rows(kc_ref, h, tc, nh))
          + (cc[:, h:h + 1] - crc_ref[0, h:h + 1, :]) for h in range(nh)]
    ps = []
    for h, s in enumerate(ss):
        m_old = m_sc[h]
        m_new = jnp.maximum(m_old, jnp.max(s, axis=1, keepdims=True))
        alpha = jnp.exp(m_old - m_new)
        p = jnp.exp(s - m_new)
        m_sc[h] = m_new
        l_sc[h] = alpha * l_sc[h] + jnp.sum(p, axis=1, keepdims=True)
        ps.append((alpha, p.astype(BF16)))
    for h, (alpha, p) in enumerate(ps):
        acc_sc[h] = alpha * acc_sc[h] + jnp.dot(p, _head_rows(vc_ref, h, tc, nh), preferred_element_type=F32)

    @pl.when(c == pl.num_programs(1) - 1)
    def _():
        for h in range(nh):
            o_ref[0, :, h * dh:(h + 1) * dh] = (acc_sc[h] / l_sc[h]).astype(o_ref.dtype)


def _fox_sample(q, kc, vc, kn, vn, cum_col, cum_t, nh, dh, tc):
    b, s_len, _ = q.shape
    past = kc.shape[1] // nh
    tp = cum_col.shape[1]
    tc = min(tc, past)
    assert past % CUM_BLK == 0 and tp - past == CUM_BLK and past % tc == 0 and s_len <= CUM_BLK
    new = pl.BlockSpec((1, s_len, nh * dh), lambda bb, c: (bb, 0, 0))
    cache = pl.BlockSpec((1, tc * nh, dh), lambda bb, c: (bb, c, 0))
    return pl.pallas_call(
        functools.partial(_fox_s_kernel, nh=nh, dh=dh, tc=tc, s_len=s_len),
        grid=(b, past // tc),
        in_specs=[new, cache, cache, new, new,
                  pl.BlockSpec((1, CUM_BLK, nh), lambda bb, c: (bb, past // CUM_BLK, 0)),
                  pl.BlockSpec((1, nh, tc), lambda bb, c: (bb, 0, c)),
                  pl.BlockSpec((1, nh, CUM_BLK), lambda bb, c: (bb, 0, past // CUM_BLK))],
        out_specs=new,
        out_shape=jax.ShapeDtypeStruct((b, s_len, nh * dh), BF16),
        scratch_shapes=[pltpu.VMEM((nh, s_len, 1), F32), pltpu.VMEM((nh, s_len, 1), F32),
                        pltpu.VMEM((nh, s_len, dh), F32)],
        compiler_params=_params("arbitrary", "arbitrary"),
        name="fox_sample",
    )(q, kc, vc, kn, vn, cum_col, cum_t, cum_t)


def _band_bias_kernel(gf_ref, o_ref):
    f8 = jnp.broadcast_to(gf_ref[0], (8, BAND_ROLL))
    sub = lax.broadcasted_iota(jnp.int32, (8, BAND_ROLL), 0)
    base = f8
    for bb in range(1, 8):
        base = jnp.where(sub == bb, pltpu.roll(f8, bb, axis=1), base)
    for a in range(o_ref.shape[1] // 8):
        rows = base if a == 0 else pltpu.roll(base, 8 * a, axis=1)
        o_ref[0, 8 * a:8 * a + 8, :] = rows[:, :BAND_WIN]


def _band_bias(rel_table, rows):
    nh = rel_table.shape[0]
    assert rows % 8 == 0 and rows <= CHUNK
    u = jnp.arange(BAND_ROLL)
    v = jnp.where(u < BAND_ROLL - BAND_TQ, u, u - BAND_ROLL)
    idx = jnp.clip(LEFT - v, -MAX_REL, MAX_REL) + MAX_REL
    gf = jnp.take(rel_table.astype(F32), idx, axis=1).reshape(nh, 1, BAND_ROLL)
    return pl.pallas_call(
        _band_bias_kernel,
        grid=(nh,),
        in_specs=[pl.BlockSpec((1, 1, BAND_ROLL), lambda h: (h, 0, 0))],
        out_specs=pl.BlockSpec((1, rows, BAND_WIN), lambda h: (h, 0, 0)),
        out_shape=jax.ShapeDtypeStruct((nh, rows, BAND_WIN), F32),
        compiler_params=_params("arbitrary"),
        name="band_bias",
    )(gf)


def _band_bias_t_kernel(gf_ref, o_ref):
    f8 = jnp.broadcast_to(gf_ref[0], (8, BAND_ROLL))
    sub = lax.broadcasted_iota(jnp.int32, (8, BAND_ROLL), 0)
    base = f8
    for bb in range(1, 8):
        base = jnp.where(sub == bb, pltpu.roll(f8, bb, axis=1), base)
    key8 = lax.broadcasted_iota(jnp.int32, (8, BAND_TQ), 0)
    qry_chunk = lax.broadcasted_iota(jnp.int32, (8, BAND_TQ), 1) // CHUNK
    for a in range(BAND_EXT // 8):
        rows = base if a == 0 else pltpu.roll(base, 8 * a, axis=1)
        cd = qry_chunk - ((key8 + 8 * a) // CHUNK - LEFT_CHUNKS)
        rows = jnp.where((cd >= 0) & (cd <= LEFT_CHUNKS), rows[:, :BAND_TQ], NEG)
        for w in range(3):
            r0 = 8 * a - (LEFT - w * BAND_TQ)
            if 0 <= r0 < BAND_WIN:
                o_ref[0, w, r0:r0 + 8, :] = rows


def _band_bias_t(rel_table):
    nh = rel_table.shape[0]
    u = jnp.arange(BAND_ROLL)
    v = jnp.where(u < 2 * BAND_TQ, u, u - BAND_ROLL)
    idx = jnp.clip(v + LEFT, -MAX_REL, MAX_REL) + MAX_REL
    gf = (jnp.take(rel_table.astype(F32), idx, axis=1) * LOG2E).reshape(nh, 1, BAND_ROLL)
    return pl.pallas_call(
        _band_bias_t_kernel,
        grid=(nh,),
        in_specs=[pl.BlockSpec((1, 1, BAND_ROLL), lambda h: (h, 0, 0))],
        out_specs=pl.BlockSpec((1, 3, BAND_WIN, BAND_TQ), lambda h: (h, 0, 0, 0)),
        out_shape=jax.ShapeDtypeStruct((nh, 3, BAND_WIN, BAND_TQ), F32),
        compiler_params=_params("arbitrary"),
        name="band_bias_t",
    )(gf)


def _softmax_pv(score_parts, value_parts):
    probs = []
    for parts in score_parts:
        m = functools.reduce(jnp.maximum, [jnp.max(s, axis=1, keepdims=True) for s in parts])
        ps = [jnp.exp(s - m) for s in parts]
        l = functools.reduce(jnp.add, [jnp.sum(p, axis=1, keepdims=True) for p in ps])
        probs.append(([p.astype(BF16) for p in ps], l))
    outs = []
    for (ps, l), vals in zip(probs, value_parts):
        o = functools.reduce(jnp.add, [jnp.dot(p, v(), preferred_element_type=F32) for p, v in zip(ps, vals)])
        outs.append(o / l)
    return outs


def _band_kernel(q_ref, k_ref, v_ref, e_ref, o_ref, vt_sc, *, hb, dh):
    g = pl.program_id(2)
    nk = vt_sc.shape[1]
    nwin = BAND_WIN // BAND_TQ
    hcols = [slice(hh * dh, (hh + 1) * dh) for hh in range(hb)]

    @pl.when(g == 0)
    def _():
        ones = jnp.ones((FOX_PAD, BAND_TQ), BF16)
        for kb in range(nk):
            for hh, cols in enumerate(hcols):
                vt_sc[hh, kb, :dh, :] = v_ref[0, kb * BAND_TQ:(kb + 1) * BAND_TQ, cols].T
                vt_sc[hh, kb, dh:, :] = ones

    wb = jnp.maximum(g - LEFT // BAND_TQ, 0)
    ws = pl.multiple_of(wb * BAND_TQ, BAND_TQ)
    w = jnp.minimum(g, LEFT // BAND_TQ)
    ss = [jnp.dot(k_ref[0, pl.ds(ws, BAND_WIN), cols], q_ref[0, :, cols].T, preferred_element_type=F32)
          + e_ref[hh, w] for hh, cols in enumerate(hcols)]
    ps = [jnp.exp2(s - jnp.max(s, axis=0, keepdims=True)).astype(BF16) for s in ss]
    for hh, (cols, p) in enumerate(zip(hcols, ps)):
        acc = functools.reduce(jnp.add, [
            jnp.dot(vt_sc[hh, wb + j], p[j * BAND_TQ:(j + 1) * BAND_TQ], preferred_element_type=F32)
            for j in range(nwin)])
        o_ref[0, :, cols] = (acc[:dh] / acc[dh:dh + 1]).T.astype(o_ref.dtype)


def _band_prompt(q, k, v, bias_t, nh, dh, hb):
    b, t, _ = q.shape
    assert t % BAND_TQ == 0 and t >= BAND_WIN
    hb = min(hb, nh)
    qspec = pl.BlockSpec((1, BAND_TQ, hb * dh), lambda h, bb, g: (bb, g, h))
    kspec = pl.BlockSpec((1, t, hb * dh), lambda h, bb, g: (bb, 0, h))
    return pl.pallas_call(
        functools.partial(_band_kernel, hb=hb, dh=dh),
        grid=(nh // hb, b, t // BAND_TQ),
        in_specs=[qspec, kspec, kspec,
                  pl.BlockSpec((hb, 3, BAND_WIN, BAND_TQ), lambda h, bb, g: (h, 0, 0, 0),
                               pipeline_mode=pl.Buffered(1))],
        out_specs=qspec,
        out_shape=jax.ShapeDtypeStruct((b, t, nh * dh), BF16),
        scratch_shapes=[pltpu.VMEM((hb, t // BAND_TQ, dh + FOX_PAD, BAND_TQ), BF16)],
        compiler_params=_params("arbitrary", "arbitrary", "arbitrary"),
        name="band_prompt",
    )(q, k, v, bias_t)


def _band_s_kernel(q_ref, kc_ref, vc_ref, kn_ref, vn_ref, e_ref, o_ref, *, nh, dh, keep, s_len):
    hcols = [slice(h * dh, (h + 1) * dh) for h in range(nh)]
    scores, values = [], []
    for h, cols in enumerate(hcols):
        q = q_ref[0, :, cols]
        e = e_ref[h]
        scores.append([_qk(q, _head_rows(kc_ref, h, keep, nh)) + e[:, :keep],
                       _qk(q, kn_ref[0, :, cols]) + e[:, keep:keep + s_len]])
        values.append([functools.partial(_head_rows, vc_ref, h, keep, nh),
                       functools.partial(lambda cols: vn_ref[0, :, cols], cols)])
    for cols, o in zip(hcols, _softmax_pv(scores, values)):
        o_ref[0, :, cols] = o.astype(o_ref.dtype)


def _band_sample(q, kc, vc, kn, vn, bias, nh, dh):
    b, s_len, _ = q.shape
    keep = kc.shape[1] // nh
    assert keep == LEFT and s_len <= CHUNK and s_len % 8 == 0
    new = pl.BlockSpec((1, s_len, nh * dh), lambda bb: (bb, 0, 0))
    cache = pl.BlockSpec((1, keep * nh, dh), lambda bb: (bb, 0, 0))
    return pl.pallas_call(
        functools.partial(_band_s_kernel, nh=nh, dh=dh, keep=keep, s_len=s_len),
        grid=(b,),
        in_specs=[new, cache, cache, new, new,
                  pl.BlockSpec((nh, s_len, BAND_WIN), lambda bb: (0, 0, 0))],
        out_specs=new,
        out_shape=jax.ShapeDtypeStruct((b, s_len, nh * dh), BF16),
        compiler_params=_params("arbitrary"),
        name="band_sample",
    )(q, kc, vc, kn, vn, bias)


def _oproj_kernel(oa_ref, ob_ref, wa_ref, wb_ref, x_ref, gate_ref, o_ref, *wcopy_refs):
    ca, cb = wcopy_refs if wcopy_refs else (None, None)
    acc = (jnp.dot(oa_ref[...], _bf16_weights(wa_ref, ca), preferred_element_type=F32)
           + jnp.dot(ob_ref[...], _bf16_weights(wb_ref, cb), preferred_element_type=F32))
    o_ref[...] = x_ref[...] + gate_ref[...] * acc


def _gate_spec(gate, m, tm, tn):
    if gate.ndim == 2:
        return pl.BlockSpec((tm, tn), lambda i, j: (i, j))
    tiles_per_seq = m // gate.shape[0] // tm
    return pl.BlockSpec((None, 1, tn), lambda i, j: (i // tiles_per_seq, 0, j))


def _oproj(oa, ob, wa, wb, x, gate, tm, tn):
    m, da = oa.shape
    db = ob.shape[1]
    (wa, ra), (wb, rb) = wa, wb
    d = wa.shape[1]
    tm = min(tm, m)
    tn = _tile(d, tn)
    emit_w = wa.dtype != BF16
    assert da == db and (not emit_w or m == tm)
    out_specs = [pl.BlockSpec((tm, tn), lambda i, j: (i, j))]
    out_shape = [jax.ShapeDtypeStruct((m, d), F32)]
    if emit_w:
        out_specs += [pl.BlockSpec((da, tn), lambda i, j: (0, j))] * 2
        out_shape += [jax.ShapeDtypeStruct((da, d), BF16)] * 2
    return pl.pallas_call(
        _oproj_kernel,
        grid=(m // tm, d // tn),
        in_specs=[pl.BlockSpec((tm, da), lambda i, j: (i, 0)),
                  pl.BlockSpec((tm, db), lambda i, j: (i, 0)),
                  pl.BlockSpec((da, tn), lambda i, j: (ra, j)),
                  pl.BlockSpec((db, tn), lambda i, j: (rb, j)),
                  pl.BlockSpec((tm, tn), lambda i, j: (i, j)),
                  _gate_spec(gate, m, tm, tn)],
        out_specs=out_specs,
        out_shape=out_shape,
        compiler_params=_params("arbitrary", "arbitrary"),
        name="out_proj",
    )(oa, ob, wa, wb, x, gate)


def _ffn_up_kernel(a_ref, wg_ref, wu_ref, o_ref, *wcopy_refs):
    cg, cu = wcopy_refs if wcopy_refs else (None, None)
    a = a_ref[...]
    g = jnp.dot(a, _bf16_weights(wg_ref, cg), preferred_element_type=F32)
    u = jnp.dot(a, _bf16_weights(wu_ref, cu), preferred_element_type=F32)
    o_ref[...] = (g * jax.nn.sigmoid(g) * u).astype(o_ref.dtype)


def _ffn_up(a, wg, wu, f, tm, tn):
    m, k = a.shape
    (wg, cg), (wu, cu) = wg, wu
    tm = min(tm, m)
    tn = _tile(f, tn)
    jg, ju = cg // tn, cu // tn
    emit_w = wg.dtype != BF16
    assert cg % tn == 0 and cu % tn == 0 and (not emit_w or m == tm)
    out_specs = [pl.BlockSpec((tm, tn), lambda i, j: (i, j))]
    out_shape = [jax.ShapeDtypeStruct((m, f), BF16)]
    if emit_w:
        out_specs += [pl.BlockSpec((k, tn), lambda i, j: (0, j))] * 2
        out_shape += [jax.ShapeDtypeStruct((k, f), BF16)] * 2
    return pl.pallas_call(
        _ffn_up_kernel,
        grid=(m // tm, f // tn),
        in_specs=[pl.BlockSpec((tm, k), lambda i, j: (i, 0)),
                  pl.BlockSpec((k, tn), lambda i, j: (0, jg + j)),
                  pl.BlockSpec((k, tn), lambda i, j: (0, ju + j))],
        out_specs=out_specs,
        out_shape=out_shape,
        compiler_params=_params("arbitrary", "arbitrary"),
        name="ffn_up",
    )(a, wg, wu)


def _ffn_down_kernel(a_ref, w_ref, x_ref, gate_ref, o_ref, *wcopy_refs):
    w = _bf16_weights(w_ref, wcopy_refs[0] if wcopy_refs else None)
    acc = jnp.dot(a_ref[...], w, preferred_element_type=F32)
    o_ref[...] = x_ref[...] + gate_ref[...] * acc


def _ffn_down(a, w_down, x, gate, tm, tn):
    m, f = a.shape
    d = w_down.shape[1]
    tm = min(tm, m)
    tn = _tile(d, tn)
    emit_w = w_down.dtype != BF16
    assert not emit_w or m == tm
    out_specs = [pl.BlockSpec((tm, tn), lambda i, j: (i, j))]
    out_shape = [jax.ShapeDtypeStruct((m, d), F32)]
    if emit_w:
        out_specs.append(pl.BlockSpec((f, tn), lambda i, j: (0, j)))
        out_shape.append(jax.ShapeDtypeStruct((f, d), BF16))
    return pl.pallas_call(
        _ffn_down_kernel,
        grid=(m // tm, d // tn),
        in_specs=[pl.BlockSpec((tm, f), lambda i, j: (i, 0)),
                  pl.BlockSpec((f, tn), lambda i, j: (0, j)),
                  pl.BlockSpec((tm, tn), lambda i, j: (i, j)),
                  _gate_spec(gate, m, tm, tn)],
        out_specs=out_specs,
        out_shape=out_shape,
        compiler_params=_params("arbitrary", "arbitrary"),
        name="ffn_down",
    )(a, w_down, x, gate)


def _layer(x, mod, cache, w, bias, *, nb, tr):
    b, t, d = x.shape
    m = b * t
    shift_a, scale_a, gate_a, shift_f, scale_f, gate_f = mod
    dh = w["g_q_a"].shape[0]
    nh_a = w["b_f"].shape[0]
    d_a = d_b = w["d_a"]
    nh_b = d_b // dh
    f = w["f"]
    scale = dh ** -0.5
    prompt = cache is None
    tm = 1024
    w16 = {}

    def gates(gt):
        if prompt:
            return gt.reshape(b, 1, d)
        return jnp.broadcast_to(gt[:, None, :], (b, t, d)).reshape(m, d)

    h = _modnorm(x, scale_a, shift_a, w["g_attn"], nb, tr).reshape(m, d)
    keep_p = min(LEFT, t)
    full32 = dict(tm=tm, tn=512, f32_rows=t, seq_len=t) if prompt else dict(tm=tm, tn=512, f32_rows=m, seq_len=m)
    tail32 = dict(tm=tm, tn=1024, f32_rows=keep_p, seq_len=t) if prompt else full32

    def pj(name, n, g, **kw):
        arr, col0 = w[name]
        kw.setdefault("tn", 1024 if prompt else 512)
        outs = _proj(h, arr, col0, n, g, dh=dh, **kw)
        if not prompt:
            *outs, wcopy = outs
            w16[name] = (wcopy, 0)
        return outs

    (q_a,) = pj("w_q_a", d_a, w["g_q_a"] * (scale * LOG2E if prompt else scale), do_rms=True, tm=tm)
    k_a32, k_a = pj("w_k_a", d_a, w["g_k_a"], do_rms=True, **full32)
    v_a32, v_a = pj("w_v_a", d_a, w["g_k_a"], do_rms=False, **full32)
    (q_b,) = pj("w_q_b", d_b, w["g_q_b"] * (scale * LOG2E if prompt else scale), do_rms=True, tm=tm)
    k_b32, k_b = pj("w_k_b", d_b, w["g_k_b"], do_rms=True, **tail32)
    v_b32, v_b = pj("w_v_b", d_b, w["g_k_b"], do_rms=False, **tail32)
    logf = _forget_gate(h, *w["w_f"], w["b_f"], tm).reshape(b, t, nh_a)

    r3 = lambda z: z.reshape(b, t, -1)
    if prompt:
        cum_t = _cumsum_t(logf.transpose(0, 2, 1))
        tq = 256
        o_a = _fox_prompt(r3(q_a), r3(k_a), r3(v_a), cum_t.transpose(0, 2, 1), nh_a, dh, tq, 8)
        o_b = _band_prompt(r3(q_b), r3(k_b), r3(v_b), bias[1], nh_b, dh, 8)
        new_k_b, new_v_b = k_b32.reshape(b, keep_p, nh_b, dh), v_b32.reshape(b, keep_p, nh_b, dh)
    else:
        ck_a, cv_a, clogf_a, ck_b, cv_b = cache
        past = ck_a.shape[1]
        tp = past + CUM_BLK
        lf_all = jnp.concatenate([clogf_a.astype(F32), logf, jnp.zeros((b, tp - past - t, nh_a), F32)], axis=1)
        cum_t = _cumsum_t(lf_all.transpose(0, 2, 1))
        o_a = _fox_sample(r3(q_a), ck_a.reshape(b, past * nh_a, dh), cv_a.reshape(b, past * nh_a, dh),
                          r3(k_a), r3(v_a), cum_t.transpose(0, 2, 1), cum_t, nh_a, dh, 1024)
        keep = ck_b.shape[1]
        o_b = _band_sample(r3(q_b), ck_b.reshape(b, keep * nh_b, dh), cv_b.reshape(b, keep * nh_b, dh),
                           r3(k_b), r3(v_b), bias[0], nh_b, dh)
        new_b = lambda ck, z: jnp.concatenate([ck, z.reshape(b, t, nh_b, dh).astype(ck.dtype)], axis=1)[:, -keep:]
        new_k_b, new_v_b = new_b(ck_b, k_b32), new_b(cv_b, v_b32)

    x1, *wo16 = _oproj(o_a.reshape(m, d_a), o_b.reshape(m, d_b), w["w_oa"], w["w_ob"], x.reshape(m, d),
                       gates(gate_a), tm, 1024 if prompt else 512)
    h2 = _modnorm(x1.reshape(b, t, d), scale_f, shift_f, w["g_ffn"], nb, tr).reshape(m, d)
    act, *wgu16 = _ffn_up(h2, w["w_g"], w["w_u"], f, 2 * tm, 256)
    y, *wd16 = _ffn_down(act, w["w_down"], x1, gates(gate_f), 512, 512 if prompt else 128)
    if not prompt:
        w16.update(w_oa=(wo16[0], 0), w_ob=(wo16[1], 0), w_g=(wgu16[0], 0), w_u=(wgu16[1], 0), w_down=wd16[0])
    return (y.reshape(b, t, d), k_a32.reshape(b, t, nh_a, dh), v_a32.reshape(b, t, nh_a, dh), logf,
            new_k_b, new_v_b, w16)


def kernel(x_prompt, x_sample, cache_k_a, cache_v_a, cache_logf_a, cache_k_b, cache_v_b, c_prompt, c_sample,
           w_ada, b_ada, g_attn, g_ffn, w_in, b_f, g_q_a, g_k_a, g_q_b, g_k_b, rel_table, w_o, w_gu, w_down):
    depth = w_ada.shape[0]
    d = x_prompt.shape[-1]
    nb_p, nb_s = c_prompt.shape[0], c_sample.shape[0]
    nh_a = b_f.shape[1]
    n_qkv = w_in.shape[2] - nh_a
    d_a = n_qkv // 6
    f = w_down.shape[1]
    assert 2 * d_a == w_o.shape[1] and w_gu.shape[2] == 2 * f
    y_p, y_s = x_prompt, x_sample
    outs_p, outs_s = [], []
    for l in range(depth):
        w_in_t = jnp.swapaxes(w_in[l], 0, 1)
        small = dict(w_f=(w_in_t, n_qkv), b_f=b_f[l], g_attn=g_attn[l], g_ffn=g_ffn[l], g_q_a=g_q_a[l], g_k_a=g_k_a[l],
                     g_q_b=g_q_b[l], g_k_b=g_k_b[l], d_a=d_a, f=f)
        names = ("w_q_a", "w_k_a", "w_v_a", "w_q_b", "w_k_b", "w_v_b")
        w32 = dict(small, w_oa=(w_o[l], 0), w_ob=(w_o[l], 1), w_g=(w_gu[l], 0), w_u=(w_gu[l], f), w_down=w_down[l],
                   **{name: (w_in_t, i * d_a) for i, name in enumerate(names)})
        c_all = jnp.concatenate([c_prompt, c_sample], axis=0)
        pad = (-c_all.shape[0]) % 16
        mod = _ada(jnp.pad(c_all, ((0, pad), (0, 0))), w_ada[l], b_ada[l])
        mod_p = [mod[:nb_p, i * d:(i + 1) * d] for i in range(6)]
        mod_s = [mod[nb_p:nb_p + nb_s, i * d:(i + 1) * d] for i in range(6)]
        bias = (_band_bias(rel_table[l], x_sample.shape[1]), _band_bias_t(rel_table[l]))
        cache = (cache_k_a[l], cache_v_a[l], cache_logf_a[l], cache_k_b[l], cache_v_b[l])
        y_s, *rest_s, w16 = _layer(y_s, mod_s, cache, w32, bias, nb=nb_s, tr=x_sample.shape[1])
        y_p, *rest_p, _ = _layer(y_p, mod_p, None, dict(small, **w16), bias, nb=1, tr=min(512, x_prompt.shape[1]))
        outs_p.append(rest_p)
        outs_s.append(rest_s)
    stack = lambda outs, i: jnp.stack([o[i] for o in outs])
    return (y_p, y_s, *[stack(outs_p, i) for i in range(5)], *[stack(outs_s, i) for i in range(5)])
```

```python
import functools

import jax
import jax.numpy as jnp
from jax import lax
from jax.experimental import pallas as pl
from jax.experimental.pallas import tpu as pltpu

CHUNK = 64
LEFT_CHUNKS = 8
LEFT = LEFT_CHUNKS * CHUNK
MAX_REL = 128
EPS = 1e-6
NEG = -1e30
LOG2E = 1.4426950408889634

LANES = 128
BAND_TQ = 256
BAND_WIN = LEFT + BAND_TQ
BAND_EXT = BAND_WIN + LEFT
BAND_ROLL = 2048
CUM_BLK = 256
FOX_PAD = 16
VMEM_LIMIT_BYTES = 56 * 1024 * 1024

F32 = jnp.float32
BF16 = jnp.bfloat16


def _params(*semantics):
    return pltpu.CompilerParams(dimension_semantics=semantics, vmem_limit_bytes=VMEM_LIMIT_BYTES)


def _tile(n, want):
    if n <= want:
        return n
    t = (want // LANES) * LANES
    while t >= LANES:
        if n % t == 0:
            return t
        t -= LANES
    raise ValueError(f"no lane-aligned tile for {n}")


def _ada_kernel(c_ref, w_ref, b_ref, o_ref):
    c = c_ref[...]
    a = (c * jax.nn.sigmoid(c)).astype(BF16)
    o_ref[...] = jnp.dot(a, w_ref[...].astype(BF16), preferred_element_type=F32) + b_ref[...]


def _ada(c, w_ada, b_ada):
    m, d = c.shape
    n = w_ada.shape[1]
    tn = _tile(n, 512)
    return pl.pallas_call(
        _ada_kernel,
        grid=(n // tn,),
        in_specs=[pl.BlockSpec((m, d), lambda j: (0, 0)),
                  pl.BlockSpec((d, tn), lambda j: (0, j)),
                  pl.BlockSpec((1, tn), lambda j: (0, j))],
        out_specs=pl.BlockSpec((m, tn), lambda j: (0, j)),
        out_shape=jax.ShapeDtypeStruct((m, n), F32),
        compiler_params=_params("arbitrary"),
        name="ada_mod",
    )(c, w_ada, b_ada.reshape(1, n))


def _modnorm_kernel(x_ref, sc_ref, sh_ref, g_ref, o_ref):
    x = x_ref[...]
    ms = jnp.mean(x * x, axis=-1, keepdims=True)
    y = x * lax.rsqrt(ms + EPS) * g_ref[...]
    o_ref[...] = (y * (1.0 + sc_ref[...]) + sh_ref[...]).astype(o_ref.dtype)


def _modnorm(x, scale, shift, g, nb, tr):
    b, t, d = x.shape
    bs = pl.BlockSpec((nb, 1, d), lambda i, j: (i, 0, 0))
    return pl.pallas_call(
        _modnorm_kernel,
        grid=(b // nb, t // tr),
        in_specs=[pl.BlockSpec((nb, tr, d), lambda i, j: (i, j, 0)), bs, bs,
                  pl.BlockSpec((1, 1, d), lambda i, j: (0, 0, 0))],
        out_specs=pl.BlockSpec((nb, tr, d), lambda i, j: (i, j, 0)),
        out_shape=jax.ShapeDtypeStruct((b, t, d), BF16),
        compiler_params=_params("arbitrary", "arbitrary"),
        name="modnorm",
    )(x, scale.reshape(b, 1, d), shift.reshape(b, 1, d), g.reshape(1, 1, d))


def _bf16_weights(w_ref, copy_ref, transposed=False):
    if copy_ref is None:
        return w_ref[...]
    w = w_ref[...]
    w = (w.T if transposed else w).astype(BF16)
    copy_ref[...] = w
    return w


def _proj_kernel(a_ref, w_ref, g_ref, *o_refs, do_rms, dh, nh, rows32, tiles_per_seq, emit_w):
    if emit_w:
        *o_refs, wcopy_ref = o_refs
    w = _bf16_weights(w_ref, wcopy_ref if emit_w else None, transposed=True)
    acc = jnp.dot(a_ref[...], w, preferred_element_type=F32)
    o16_ref = o_refs[-1]
    tm, tn = acc.shape
    hpt = tn // dh
    g = g_ref[...]
    ys = []
    for hh in range(hpt):
        y = acc[:, hh * dh:(hh + 1) * dh]
        if do_rms:
            ms = jnp.mean(y * y, axis=-1, keepdims=True)
            y = y * lax.rsqrt(ms + EPS) * g
        o16_ref[:, hh * dh:(hh + 1) * dh] = y.astype(o16_ref.dtype)
        ys.append(y)
    if rows32:
        o32_ref = o_refs[0]
        h0 = pl.program_id(1) * hpt

        def store32():
            for hh in range(hpt):
                o32_ref[pl.ds(h0 + hh, rows32, stride=nh), :] = ys[hh][tm - rows32:, :]

        if rows32 == tm:
            store32()
        else:
            pl.when(pl.program_id(0) % tiles_per_seq == tiles_per_seq - 1)(store32)


def _proj(a, w, col0, n, g, *, do_rms, dh, tm, tn, f32_rows=0, seq_len=None):
    m, k = a.shape
    tm = min(tm, m)
    tn = _tile(n, tn)
    nh = n // dh
    j0 = col0 // tn
    emit_w = w.dtype != BF16
    assert col0 % tn == 0 and m % tm == 0 and (not emit_w or m == tm)
    out_specs = [pl.BlockSpec((tm, tn), lambda i, j: (i, j))]
    out_shape = [jax.ShapeDtypeStruct((m, n), BF16)]
    rows32, tps = 0, 1
    if f32_rows:
        tps = seq_len // tm
        assert seq_len % tm == 0
        if f32_rows == seq_len:
            rows32 = tm
            out_specs.insert(0, pl.BlockSpec((tm * nh, dh), lambda i, j: (i, 0)))
            out_shape.insert(0, jax.ShapeDtypeStruct((m * nh, dh), F32))
        else:
            rows32 = f32_rows
            assert rows32 <= tm
            out_specs.insert(0, pl.BlockSpec((rows32 * nh, dh), lambda i, j: (i // tps, 0)))
            out_shape.insert(0, jax.ShapeDtypeStruct((m // seq_len * rows32 * nh, dh), F32))
    if emit_w:
        out_specs.append(pl.BlockSpec((k, tn), lambda i, j: (0, j)))
        out_shape.append(jax.ShapeDtypeStruct((k, n), BF16))
    return pl.pallas_call(
        functools.partial(_proj_kernel, do_rms=do_rms, dh=dh, nh=nh, rows32=rows32, tiles_per_seq=tps,
                          emit_w=emit_w),
        grid=(m // tm, n // tn),
        in_specs=[pl.BlockSpec((tm, k), lambda i, j: (i, 0)),
                  pl.BlockSpec((tn, k), lambda i, j: (j0 + j, 0)) if emit_w else
                  pl.BlockSpec((k, tn), lambda i, j: (0, j0 + j)),
                  pl.BlockSpec((1, dh), lambda i, j: (0, 0))],
        out_specs=out_specs,
        out_shape=out_shape,
        compiler_params=_params("arbitrary", "arbitrary"),
        name="proj",
    )(a, w, g.reshape(1, dh).astype(F32))


def _gate_kernel(a_ref, w_ref, b_ref, o_ref):
    z = _qk(a_ref[...], w_ref[...].astype(BF16)) + b_ref[...]
    o_ref[...] = jnp.minimum(z, 0.0) - jnp.log1p(jnp.exp(-jnp.abs(z)))


def _forget_gate(a, w_t, row0, b_f, tm):
    m, k = a.shape
    nh = b_f.shape[0]
    tm = min(tm, m)
    assert row0 % nh == 0
    return pl.pallas_call(
        _gate_kernel,
        grid=(m // tm,),
        in_specs=[pl.BlockSpec((tm, k), lambda i: (i, 0)),
                  pl.BlockSpec((nh, k), lambda i: (row0 // nh, 0)),
                  pl.BlockSpec((1, nh), lambda i: (0, 0))],
        out_specs=pl.BlockSpec((tm, nh), lambda i: (i, 0)),
        out_shape=jax.ShapeDtypeStruct((m, nh), F32),
        compiler_params=_params("arbitrary"),
        name="forget_gate",
    )(a, w_t, b_f.reshape(1, nh).astype(F32))


def _cumsum_kernel(x_ref, o_ref):
    h, tp = x_ref.shape[1], x_ref.shape[2]
    r = lax.broadcasted_iota(jnp.int32, (CUM_BLK, CUM_BLK), 0)
    c = lax.broadcasted_iota(jnp.int32, (CUM_BLK, CUM_BLK), 1)
    tri = (r <= c).astype(BF16)
    carry = jnp.zeros((h, 1), F32)
    for blk in range(tp // CUM_BLK):
        x = x_ref[0, :, blk * CUM_BLK:(blk + 1) * CUM_BLK]
        hi = x.astype(BF16)
        r1 = x - hi.astype(F32)
        mid = r1.astype(BF16)
        lo = (r1 - mid.astype(F32)).astype(BF16)
        cs = (jnp.dot(hi, tri, preferred_element_type=F32)
              + jnp.dot(mid, tri, preferred_element_type=F32)
              + jnp.dot(lo, tri, preferred_element_type=F32)) + carry
        o_ref[0, :, blk * CUM_BLK:(blk + 1) * CUM_BLK] = cs
        carry = cs[:, CUM_BLK - 1:CUM_BLK]


def _cumsum_t(logf_t):
    b, h, tp = logf_t.shape
    return pl.pallas_call(
        _cumsum_kernel,
        grid=(b,),
        in_specs=[pl.BlockSpec((1, h, tp), lambda i: (i, 0, 0))],
        out_specs=pl.BlockSpec((1, h, tp), lambda i: (i, 0, 0)),
        out_shape=jax.ShapeDtypeStruct((b, h, tp), F32),
        compiler_params=_params("arbitrary"),
        name="logf_cumsum",
    )(logf_t)


def _head_column(cc, h):
    lane = lax.broadcasted_iota(jnp.int32, cc.shape, 1)
    return jnp.sum(jnp.where(lane == h, cc, 0.0), axis=1, keepdims=True)


def _qk(q, k):
    return lax.dot_general(q, k, (((1,), (1,)), ((), ())), preferred_element_type=F32)


def _fox_kernel(q_ref, k_ref, v_ref, cc_ref, o_ref, vt_sc, ck_sc, *, tq, hb, dh):
    hg = pl.program_id(1)
    i = pl.program_id(2)
    nk = vt_sc.shape[1]
    heads = range(hb)
    hcols = [slice(hh * dh, (hh + 1) * dh) for hh in heads]

    @pl.when(i == 0)
    def _():
        ones = jnp.ones((FOX_PAD, tq), BF16)
        for kb in range(nk):
            rows = slice(kb * tq, (kb + 1) * tq)
            cc = cc_ref[0, rows, :]
            for hh in heads:
                vt_sc[hh, kb, :dh, :] = v_ref[0, rows, hcols[hh]].T
                vt_sc[hh, kb, dh:, :] = ones
                ck_sc[hh, rows, :] = jnp.broadcast_to(_head_column(cc, hg * hb + hh) * (-LOG2E), (tq, LANES))

    qts = [q_ref[0, :, cols].T for cols in hcols]

    def scores(kb):
        off = pl.multiple_of(kb * tq, tq)
        out = []
        for hh in heads:
            s = jnp.dot(k_ref[0, pl.ds(off, tq), hcols[hh]], qts[hh], preferred_element_type=F32)
            ck = ck_sc[hh, pl.ds(off, tq), :]
            out.append(s + jnp.concatenate([ck] * (tq // LANES), axis=1))
        return out

    def step(kb, state, mask=None):
        stats = []
        for s, (m, _) in zip(scores(kb), state):
            if mask is not None:
                s = jnp.where(mask, s, NEG)
            m_new = jnp.maximum(m, jnp.max(s, axis=0, keepdims=True))
            stats.append((m_new, jnp.exp2(m - m_new), jnp.exp2(s - m_new).astype(BF16)))
        return [(m_new, alpha * acc + jnp.dot(vt_sc[hh, kb], p, preferred_element_type=F32))
                for hh, ((m_new, alpha, p), (_, acc)) in enumerate(zip(stats, state))]

    state = [(jnp.full((1, tq), NEG, F32), jnp.zeros((dh + FOX_PAD, tq), F32)) for _ in heads]
    state = lax.fori_loop(0, i, step, state)
    key = lax.broadcasted_iota(jnp.int32, (tq, tq), 0)
    qry = lax.broadcasted_iota(jnp.int32, (tq, tq), 1)
    state = step(i, state, mask=key <= qry)
    for cols, (_, acc) in zip(hcols, state):
        o_ref[0, :, cols] = (acc[:dh] / acc[dh:dh + 1]).T.astype(o_ref.dtype)


def _fox_prompt(q, k, v, cum_col, nh, dh, tq, hb):
    b, t, _ = q.shape
    hb = min(hb, nh)
    assert t % tq == 0 and tq % LANES == 0
    qspec = pl.BlockSpec((1, tq, hb * dh), lambda bb, h, i: (bb, i, h))
    kspec = pl.BlockSpec((1, t, hb * dh), lambda bb, h, i: (bb, 0, h))
    return pl.pallas_call(
        functools.partial(_fox_kernel, tq=tq, hb=hb, dh=dh),
        grid=(b, nh // hb, t // tq),
        in_specs=[qspec, kspec, kspec, pl.BlockSpec((1, t, nh), lambda bb, h, i: (bb, 0, 0))],
        out_specs=qspec,
        out_shape=jax.ShapeDtypeStruct((b, t, nh * dh), BF16),
        scratch_shapes=[pltpu.VMEM((hb, t // tq, dh + FOX_PAD, tq), BF16), pltpu.VMEM((hb, t, LANES), F32)],
        compiler_params=_params("arbitrary", "arbitrary", "arbitrary"),
        name="fox_prompt",
    )(q, k, v, cum_col)


def _head_rows(ref, h, n, nh):
    return ref[0, pl.ds(h, n, stride=nh), :].astype(BF16)


def _fox_s_kernel(q_ref, kc_ref, vc_ref, kn_ref, vn_ref, cc_ref, crc_ref, crn_ref, o_ref, m_sc, l_sc, acc_sc,
                  *, nh, dh, tc, s_len):
    c = pl.program_id(1)
    cc = cc_ref[0][:s_len]
    r = lax.broadcasted_iota(jnp.int32, (s_len, s_len), 0)
    col = lax.broadcasted_iota(jnp.int32, (s_len, s_len), 1)

    @pl.when(c == 0)
    def _():
        hcols = [slice(h * dh, (h + 1) * dh) for h in range(nh)]
        ss = [_qk(q_ref[0, :, cols], kn_ref[0, :, cols]) + (cc[:, h:h + 1] - crn_ref[0, h:h + 1, :s_len])
              for h, cols in enumerate(hcols)]
        ps = []
        for h, s in enumerate(ss):
            s = jnp.where(col <= r, s, NEG)
            m = jnp.max(s, axis=1, keepdims=True)
            p = jnp.exp(s - m)
            m_sc[h] = m
            l_sc[h] = jnp.sum(p, axis=1, keepdims=True)
            ps.append(p.astype(BF16))
        for h, p in enumerate(ps):
            acc_sc[h] = jnp.dot(p, vn_ref[0, :, hcols[h]], preferred_element_type=F32)

    ss = [_qk(q_ref[0, :, h * dh:(h + 1) * dh], _head_rows(kc_ref, h, tc, nh))
          + (cc[:, h:h + 1] - crc_ref[0, h:h + 1, :]) for h in range(nh)]
    ps = []
    for h, s in enumerate(ss):
        m_old = m_sc[h]
        m_new = jnp.maximum(m_old, jnp.max(s, axis=1, keepdims=True))
        alpha = jnp.exp(m_old - m_new)
        p = jnp.exp(s - m_new)
        m_sc[h] = m_new
        l_sc[h] = alpha * l_sc[h] + jnp.sum(p, axis=1, keepdims=True)
        ps.append((alpha, p.astype(BF16)))
    for h, (alpha, p) in enumerate(ps):
        acc_sc[h] = alpha * acc_sc[h] + jnp.dot(p, _head_rows(vc_ref, h, tc, nh), preferred_element_type=F32)

    @pl.when(c == pl.num_programs(1) - 1)
    def _():
        for h in range(nh):
            o_ref[0, :, h * dh:(h + 1) * dh] = (acc_sc[h] / l_sc[h]).astype(o_ref.dtype)


def _fox_sample(q, kc, vc, kn, vn, cum_col, cum_t, nh, dh, tc):
    b, s_len, _ = q.shape
    past = kc.shape[1] // nh
    tp = cum_col.shape[1]
    tc = min(tc, past)
    assert past % CUM_BLK == 0 and tp - past == CUM_BLK and past % tc == 0 and s_len <= CUM_BLK
    new = pl.BlockSpec((1, s_len, nh * dh), lambda bb, c: (bb, 0, 0))
    cache = pl.BlockSpec((1, tc * nh, dh), lambda bb, c: (bb, c, 0))
    return pl.pallas_call(
        functools.partial(_fox_s_kernel, nh=nh, dh=dh, tc=tc, s_len=s_len),
        grid=(b, past // tc),
        in_specs=[new, cache, cache, new, new,
                  pl.BlockSpec((1, CUM_BLK, nh), lambda bb, c: (bb, past // CUM_BLK, 0)),
                  pl.BlockSpec((1, nh, tc), lambda bb, c: (bb, 0, c)),
                  pl.BlockSpec((1, nh, CUM_BLK), lambda bb, c: (bb, 0, past // CUM_BLK))],
        out_specs=new,
        out_shape=jax.ShapeDtypeStruct((b, s_len, nh * dh), BF16),
        scratch_shapes=[pltpu.VMEM((nh, s_len, 1), F32), pltpu.VMEM((nh, s_len, 1), F32),
                        pltpu.VMEM((nh, s_len, dh), F32)],
        compiler_params=_params("arbitrary", "arbitrary"),
        name="fox_sample",
    )(q, kc, vc, kn, vn, cum_col, cum_t, cum_t)


def _band_bias_kernel(gf_ref, o_ref):
    f8 = jnp.broadcast_to(gf_ref[0], (8, BAND_ROLL))
    sub = lax.broadcasted_iota(jnp.int32, (8, BAND_ROLL), 0)
    base = f8
    for bb in range(1, 8):
        base = jnp.where(sub == bb, pltpu.roll(f8, bb, axis=1), base)
    for a in range(o_ref.shape[1] // 8):
        rows = base if a == 0 else pltpu.roll(base, 8 * a, axis=1)
        o_ref[0, 8 * a:8 * a + 8, :] = rows[:, :BAND_WIN]


def _band_bias(rel_table, rows):
    nh = rel_table.shape[0]
    assert rows % 8 == 0 and rows <= CHUNK
    u = jnp.arange(BAND_ROLL)
    v = jnp.where(u < BAND_ROLL - BAND_TQ, u, u - BAND_ROLL)
    idx = jnp.clip(LEFT - v, -MAX_REL, MAX_REL) + MAX_REL
    gf = jnp.take(rel_table.astype(F32), idx, axis=1).reshape(nh, 1, BAND_ROLL)
    return pl.pallas_call(
        _band_bias_kernel,
        grid=(nh,),
        in_specs=[pl.BlockSpec((1, 1, BAND_ROLL), lambda h: (h, 0, 0))],
        out_specs=pl.BlockSpec((1, rows, BAND_WIN), lambda h: (h, 0, 0)),
        out_shape=jax.ShapeDtypeStruct((nh, rows, BAND_WIN), F32),
        compiler_params=_params("arbitrary"),
        name="band_bias",
    )(gf)


def _band_bias_t_kernel(gf_ref, o_ref):
    f8 = jnp.broadcast_to(gf_ref[0], (8, BAND_ROLL))
    sub = lax.broadcasted_iota(jnp.int32, (8, BAND_ROLL), 0)
    base = f8
    for bb in range(1, 8):
        base = jnp.where(sub == bb, pltpu.roll(f8, bb, axis=1), base)
    key8 = lax.broadcasted_iota(jnp.int32, (8, BAND_TQ), 0)
    qry_chunk = lax.broadcasted_iota(jnp.int32, (8, BAND_TQ), 1) // CHUNK
    for a in range(BAND_EXT // 8):
        rows = base if a == 0 else pltpu.roll(base, 8 * a, axis=1)
        cd = qry_chunk - ((key8 + 8 * a) // CHUNK - LEFT_CHUNKS)
        rows = jnp.where((cd >= 0) & (cd <= LEFT_CHUNKS), rows[:, :BAND_TQ], NEG)
        for w in range(3):
            r0 = 8 * a - (LEFT - w * BAND_TQ)
            if 0 <= r0 < BAND_WIN:
                o_ref[0, w, r0:r0 + 8, :] = rows


def _band_bias_t(rel_table):
    nh = rel_table.shape[0]
    u = jnp.arange(BAND_ROLL)
    v = jnp.where(u < 2 * BAND_TQ, u, u - BAND_ROLL)
    idx = jnp.clip(v + LEFT, -MAX_REL, MAX_REL) + MAX_REL
    gf = (jnp.take(rel_table.astype(F32), idx, axis=1) * LOG2E).reshape(nh, 1, BAND_ROLL)
    return pl.pallas_call(
        _band_bias_t_kernel,
        grid=(nh,),
        in_specs=[pl.BlockSpec((1, 1, BAND_ROLL), lambda h: (h, 0, 0))],
        out_specs=pl.BlockSpec((1, 3, BAND_WIN, BAND_TQ), lambda h: (h, 0, 0, 0)),
        out_shape=jax.ShapeDtypeStruct((nh, 3, BAND_WIN, BAND_TQ), F32),
        compiler_params=_params("arbitrary"),
        name="band_bias_t",
    )(gf)


def _softmax_pv(score_parts, value_parts):
    probs = []
    for parts in score_parts:
        m = functools.reduce(jnp.maximum, [jnp.max(s, axis=1, keepdims=True) for s in parts])
        ps = [jnp.exp(s - m) for s in parts]
        l = functools.reduce(jnp.add, [jnp.sum(p, axis=1, keepdims=True) for p in ps])
        probs.append(([p.astype(BF16) for p in ps], l))
    outs = []
    for (ps, l), vals in zip(probs, value_parts):
        o = functools.reduce(jnp.add, [jnp.dot(p, v(), preferred_element_type=F32) for p, v in zip(ps, vals)])
        outs.append(o / l)
    return outs


def _band_kernel(q_ref, k_ref, v_ref, e_ref, o_ref, vt_sc, *, hb, dh):
    g = pl.program_id(2)
    nk = vt_sc.shape[1]
    nwin = BAND_WIN // BAND_TQ
    hcols = [slice(hh * dh, (hh + 1) * dh) for hh in range(hb)]

    @pl.when(g == 0)
    def _():
        ones = jnp.ones((FOX_PAD, BAND_TQ), BF16)
        for kb in range(nk):
            for hh, cols in enumerate(hcols):
                vt_sc[hh, kb, :dh, :] = v_ref[0, kb * BAND_TQ:(kb + 1) * BAND_TQ, cols].T
                vt_sc[hh, kb, dh:, :] = ones

    wb = jnp.maximum(g - LEFT // BAND_TQ, 0)
    ws = pl.multiple_of(wb * BAND_TQ, BAND_TQ)
    w = jnp.minimum(g, LEFT // BAND_TQ)
    ss = [jnp.dot(k_ref[0, pl.ds(ws, BAND_WIN), cols], q_ref[0, :, cols].T, preferred_element_type=F32)
          + e_ref[hh, w] for hh, cols in enumerate(hcols)]
    ps = [jnp.exp2(s - jnp.max(s, axis=0, keepdims=True)).astype(BF16) for s in ss]
    for hh, (cols, p) in enumerate(zip(hcols, ps)):
        acc = functools.reduce(jnp.add, [
            jnp.dot(vt_sc[hh, wb + j], p[j * BAND_TQ:(j + 1) * BAND_TQ], preferred_element_type=F32)
            for j in range(nwin)])
        o_ref[0, :, cols] = (acc[:dh] / acc[dh:dh + 1]).T.astype(o_ref.dtype)


def _band_prompt(q, k, v, bias_t, nh, dh, hb):
    b, t, _ = q.shape
    assert t % BAND_TQ == 0 and t >= BAND_WIN
    hb = min(hb, nh)
    qspec = pl.BlockSpec((1, BAND_TQ, hb * dh), lambda h, bb, g: (bb, g, h))
    kspec = pl.BlockSpec((1, t, hb * dh), lambda h, bb, g: (bb, 0, h))
    return pl.pallas_call(
        functools.partial(_band_kernel, hb=hb, dh=dh),
        grid=(nh // hb, b, t // BAND_TQ),
        in_specs=[qspec, kspec, kspec,
                  pl.BlockSpec((hb, 3, BAND_WIN, BAND_TQ), lambda h, bb, g: (h, 0, 0, 0),
                               pipeline_mode=pl.Buffered(1))],
        out_specs=qspec,
        out_shape=jax.ShapeDtypeStruct((b, t, nh * dh), BF16),
        scratch_shapes=[pltpu.VMEM((hb, t // BAND_TQ, dh + FOX_PAD, BAND_TQ), BF16)],
        compiler_params=_params("arbitrary", "arbitrary", "arbitrary"),
        name="band_prompt",
    )(q, k, v, bias_t)


def _band_s_kernel(q_ref, kc_ref, vc_ref, kn_ref, vn_ref, e_ref, o_ref, *, nh, dh, keep, s_len):
    hcols = [slice(h * dh, (h + 1) * dh) for h in range(nh)]
    scores, values = [], []
    for h, cols in enumerate(hcols):
        q = q_ref[0, :, cols]
        e = e_ref[h]
        scores.append([_qk(q, _head_rows(kc_ref, h, keep, nh)) + e[:, :keep],
                       _qk(q, kn_ref[0, :, cols]) + e[:, keep:keep + s_len]])
        values.append([functools.partial(_head_rows, vc_ref, h, keep, nh),
                       functools.partial(lambda cols: vn_ref[0, :, cols], cols)])
    for cols, o in zip(hcols, _softmax_pv(scores, values)):
        o_ref[0, :, cols] = o.astype(o_ref.dtype)


def _band_sample(q, kc, vc, kn, vn, bias, nh, dh):
    b, s_len, _ = q.shape
    keep = kc.shape[1] // nh
    assert keep == LEFT and s_len <= CHUNK and s_len % 8 == 0
    new = pl.BlockSpec((1, s_len, nh * dh), lambda bb: (bb, 0, 0))
    cache = pl.BlockSpec((1, keep * nh, dh), lambda bb: (bb, 0, 0))
    return pl.pallas_call(
        functools.partial(_band_s_kernel, nh=nh, dh=dh, keep=keep, s_len=s_len),
        grid=(b,),
        in_specs=[new, cache, cache, new, new,
                  pl.BlockSpec((nh, s_len, BAND_WIN), lambda bb: (0, 0, 0))],
        out_specs=new,
        out_shape=jax.ShapeDtypeStruct((b, s_len, nh * dh), BF16),
        compiler_params=_params("arbitrary"),
        name="band_sample",
    )(q, kc, vc, kn, vn, bias)


def _oproj_kernel(oa_ref, ob_ref, wa_ref, wb_ref, x_ref, gate_ref, o_ref, *wcopy_refs):
    ca, cb = wcopy_refs if wcopy_refs else (None, None)
    acc = (jnp.dot(oa_ref[...], _bf16_weights(wa_ref, ca), preferred_element_type=F32)
           + jnp.dot(ob_ref[...], _bf16_weights(wb_ref, cb), preferred_element_type=F32))
    o_ref[...] = x_ref[...] + gate_ref[...] * acc


def _gate_spec(gate, m, tm, tn, order=lambda fn: fn):
    if gate.ndim == 2:
        return pl.BlockSpec((tm, tn), order(lambda i, j: (i, j)))
    tiles_per_seq = m // gate.shape[0] // tm
    return pl.BlockSpec((None, 1, tn), order(lambda i, j: (i // tiles_per_seq, 0, j)))


def _oproj(oa, ob, wa, wb, x, gate, tm, tn):
    m, da = oa.shape
    db = ob.shape[1]
    (wa, ra), (wb, rb) = wa, wb
    d = wa.shape[1]
    tm = min(tm, m)
    tn = _tile(d, tn)
    emit_w = wa.dtype != BF16
    assert da == db and (not emit_w or m == tm)
    out_specs = [pl.BlockSpec((tm, tn), lambda i, j: (i, j))]
    out_shape = [jax.ShapeDtypeStruct((m, d), F32)]
    if emit_w:
        out_specs += [pl.BlockSpec((da, tn), lambda i, j: (0, j))] * 2
        out_shape += [jax.ShapeDtypeStruct((da, d), BF16)] * 2
    return pl.pallas_call(
        _oproj_kernel,
        grid=(m // tm, d // tn),
        in_specs=[pl.BlockSpec((tm, da), lambda i, j: (i, 0)),
                  pl.BlockSpec((tm, db), lambda i, j: (i, 0)),
                  pl.BlockSpec((da, tn), lambda i, j: (ra, j)),
                  pl.BlockSpec((db, tn), lambda i, j: (rb, j)),
                  pl.BlockSpec((tm, tn), lambda i, j: (i, j)),
                  _gate_spec(gate, m, tm, tn)],
        out_specs=out_specs,
        out_shape=out_shape,
        compiler_params=_params("arbitrary", "arbitrary"),
        name="out_proj",
    )(oa, ob, wa, wb, x, gate)


def _ffn_up_kernel(a_ref, wg_ref, wu_ref, o_ref, *wcopy_refs):
    cg, cu = wcopy_refs if wcopy_refs else (None, None)
    a = a_ref[...]
    g = jnp.dot(a, _bf16_weights(wg_ref, cg), preferred_element_type=F32)
    u = jnp.dot(a, _bf16_weights(wu_ref, cu), preferred_element_type=F32)
    o_ref[...] = (g * jax.nn.sigmoid(g) * u).astype(o_ref.dtype)


def _ffn_up(a, wg, wu, f, tm, tn):
    m, k = a.shape
    (wg, cg), (wu, cu) = wg, wu
    tm = min(tm, m)
    tn = _tile(f, tn)
    jg, ju = cg // tn, cu // tn
    emit_w = wg.dtype != BF16
    assert cg % tn == 0 and cu % tn == 0 and (not emit_w or m == tm)
    out_specs = [pl.BlockSpec((tm, tn), lambda i, j: (i, j))]
    out_shape = [jax.ShapeDtypeStruct((m, f), BF16)]
    if emit_w:
        out_specs += [pl.BlockSpec((k, tn), lambda i, j: (0, j))] * 2
        out_shape += [jax.ShapeDtypeStruct((k, f), BF16)] * 2
    return pl.pallas_call(
        _ffn_up_kernel,
        grid=(m // tm, f // tn),
        in_specs=[pl.BlockSpec((tm, k), lambda i, j: (i, 0)),
                  pl.BlockSpec((k, tn), lambda i, j: (0, jg + j)),
                  pl.BlockSpec((k, tn), lambda i, j: (0, ju + j))],
        out_specs=out_specs,
        out_shape=out_shape,
        compiler_params=_params("arbitrary", "arbitrary"),
        name="ffn_up",
    )(a, wg, wu)


def _ffn_down_kernel(a_ref, w_ref, x_ref, gate_ref, o_ref, *wcopy_refs):
    w = _bf16_weights(w_ref, wcopy_refs[0] if wcopy_refs else None)
    acc = jnp.dot(a_ref[...], w, preferred_element_type=F32)
    o_ref[...] = x_ref[...] + gate_ref[...] * acc


def _ffn_down(a, w_down, x, gate, tm, tn):
    m, f = a.shape
    d = w_down.shape[1]
    tm = min(tm, m)
    tn = _tile(d, tn)
    emit_w = w_down.dtype != BF16
    assert not emit_w or m == tm
    order = lambda fn: (lambda j, i: fn(i, j))
    out_specs = [pl.BlockSpec((tm, tn), order(lambda i, j: (i, j)))]
    out_shape = [jax.ShapeDtypeStruct((m, d), F32)]
    if emit_w:
        out_specs.append(pl.BlockSpec((f, tn), order(lambda i, j: (0, j))))
        out_shape.append(jax.ShapeDtypeStruct((f, d), BF16))
    return pl.pallas_call(
        _ffn_down_kernel,
        grid=(d // tn, m // tm),
        in_specs=[pl.BlockSpec((tm, f), order(lambda i, j: (i, 0))),
                  pl.BlockSpec((f, tn), order(lambda i, j: (0, j))),
                  pl.BlockSpec((tm, tn), order(lambda i, j: (i, j))),
                  _gate_spec(gate, m, tm, tn, order)],
        out_specs=out_specs,
        out_shape=out_shape,
        compiler_params=_params("arbitrary", "arbitrary"),
        name="ffn_down",
    )(a, w_down, x, gate)


def _layer(x, mod, cache, w, bias, *, nb, tr):
    b, t, d = x.shape
    m = b * t
    shift_a, scale_a, gate_a, shift_f, scale_f, gate_f = mod
    dh = w["g_q_a"].shape[0]
    nh_a = w["b_f"].shape[0]
    d_a = d_b = w["d_a"]
    nh_b = d_b // dh
    f = w["f"]
    scale = dh ** -0.5
    prompt = cache is None
    tm = 1024
    w16 = {}

    def gates(gt):
        if prompt:
            return gt.reshape(b, 1, d)
        return jnp.broadcast_to(gt[:, None, :], (b, t, d)).reshape(m, d)

    h = _modnorm(x, scale_a, shift_a, w["g_attn"], nb, tr).reshape(m, d)
    keep_p = min(LEFT, t)
    full32 = dict(tm=tm, tn=512, f32_rows=t, seq_len=t) if prompt else dict(tm=tm, tn=512, f32_rows=m, seq_len=m)
    tail32 = dict(tm=tm, tn=1024, f32_rows=keep_p, seq_len=t) if prompt else full32

    def pj(name, n, g, **kw):
        arr, col0 = w[name]
        kw.setdefault("tn", 1024 if prompt else 512)
        outs = _proj(h, arr, col0, n, g, dh=dh, **kw)
        if not prompt:
            *outs, wcopy = outs
            w16[name] = (wcopy, 0)
        return outs

    (q_a,) = pj("w_q_a", d_a, w["g_q_a"] * (scale * LOG2E if prompt else scale), do_rms=True, tm=tm)
    k_a32, k_a = pj("w_k_a", d_a, w["g_k_a"], do_rms=True, **full32)
    v_a32, v_a = pj("w_v_a", d_a, w["g_k_a"], do_rms=False, **full32)
    (q_b,) = pj("w_q_b", d_b, w["g_q_b"] * (scale * LOG2E if prompt else scale), do_rms=True, tm=tm)
    k_b32, k_b = pj("w_k_b", d_b, w["g_k_b"], do_rms=True, **tail32)
    v_b32, v_b = pj("w_v_b", d_b, w["g_k_b"], do_rms=False, **tail32)
    logf = _forget_gate(h, *w["w_f"], w["b_f"], tm).reshape(b, t, nh_a)

    r3 = lambda z: z.reshape(b, t, -1)
    if prompt:
        cum_t = _cumsum_t(logf.transpose(0, 2, 1))
        tq = 256
        o_a = _fox_prompt(r3(q_a), r3(k_a), r3(v_a), cum_t.transpose(0, 2, 1), nh_a, dh, tq, 8)
        o_b = _band_prompt(r3(q_b), r3(k_b), r3(v_b), bias[1], nh_b, dh, 8)
        new_k_b, new_v_b = k_b32.reshape(b, keep_p, nh_b, dh), v_b32.reshape(b, keep_p, nh_b, dh)
    else:
        ck_a, cv_a, clogf_a, ck_b, cv_b = cache
        past = ck_a.shape[1]
        tp = past + CUM_BLK
        lf_all = jnp.concatenate([clogf_a.astype(F32), logf, jnp.zeros((b, tp - past - t, nh_a), F32)], axis=1)
        cum_t = _cumsum_t(lf_all.transpose(0, 2, 1))
        o_a = _fox_sample(r3(q_a), ck_a.reshape(b, past * nh_a, dh), cv_a.reshape(b, past * nh_a, dh),
                          r3(k_a), r3(v_a), cum_t.transpose(0, 2, 1), cum_t, nh_a, dh, 1024)
        keep = ck_b.shape[1]
        o_b = _band_sample(r3(q_b), ck_b.reshape(b, keep * nh_b, dh), cv_b.reshape(b, keep * nh_b, dh),
                           r3(k_b), r3(v_b), bias[0], nh_b, dh)
        new_b = lambda ck, z: jnp.concatenate([ck, z.reshape(b, t, nh_b, dh).astype(ck.dtype)], axis=1)[:, -keep:]
        new_k_b, new_v_b = new_b(ck_b, k_b32), new_b(cv_b, v_b32)

    x1, *wo16 = _oproj(o_a.reshape(m, d_a), o_b.reshape(m, d_b), w["w_oa"], w["w_ob"], x.reshape(m, d),
                       gates(gate_a), tm, 1024 if prompt else 512)
    h2 = _modnorm(x1.reshape(b, t, d), scale_f, shift_f, w["g_ffn"], nb, tr).reshape(m, d)
    act, *wgu16 = _ffn_up(h2, w["w_g"], w["w_u"], f, 2 * tm, 256)
    y, *wd16 = _ffn_down(act, w["w_down"], x1, gates(gate_f), 512, 512 if prompt else 128)
    if not prompt:
        w16.update(w_oa=(wo16[0], 0), w_ob=(wo16[1], 0), w_g=(wgu16[0], 0), w_u=(wgu16[1], 0), w_down=wd16[0])
    return (y.reshape(b, t, d), k_a32.reshape(b, t, nh_a, dh), v_a32.reshape(b, t, nh_a, dh), logf,
            new_k_b, new_v_b, w16)


def kernel(x_prompt, x_sample, cache_k_a, cache_v_a, cache_logf_a, cache_k_b, cache_v_b, c_prompt, c_sample,
           w_ada, b_ada, g_attn, g_ffn, w_in, b_f, g_q_a, g_k_a, g_q_b, g_k_b, rel_table, w_o, w_gu, w_down):
    depth = w_ada.shape[0]
    d = x_prompt.shape[-1]
    nb_p, nb_s = c_prompt.shape[0], c_sample.shape[0]
    nh_a = b_f.shape[1]
    n_qkv = w_in.shape[2] - nh_a
    d_a = n_qkv // 6
    f = w_down.shape[1]
    assert 2 * d_a == w_o.shape[1] and w_gu.shape[2] == 2 * f
    y_p, y_s = x_prompt, x_sample
    outs_p, outs_s = [], []
    for l in range(depth):
        w_in_t = jnp.swapaxes(w_in[l], 0, 1)
        small = dict(w_f=(w_in_t, n_qkv), b_f=b_f[l], g_attn=g_attn[l], g_ffn=g_ffn[l], g_q_a=g_q_a[l], g_k_a=g_k_a[l],
                     g_q_b=g_q_b[l], g_k_b=g_k_b[l], d_a=d_a, f=f)
        names = ("w_q_a", "w_k_a", "w_v_a", "w_q_b", "w_k_b", "w_v_b")
        w32 = dict(small, w_oa=(w_o[l], 0), w_ob=(w_o[l], 1), w_g=(w_gu[l], 0), w_u=(w_gu[l], f), w_down=w_down[l],
                   **{name: (w_in_t, i * d_a) for i, name in enumerate(names)})
        c_all = jnp.concatenate([c_prompt, c_sample], axis=0)
        pad = (-c_all.shape[0]) % 16
        mod = _ada(jnp.pad(c_all, ((0, pad), (0, 0))), w_ada[l], b_ada[l])
        mod_p = [mod[:nb_p, i * d:(i + 1) * d] for i in range(6)]
        mod_s = [mod[nb_p:nb_p + nb_s, i * d:(i + 1) * d] for i in range(6)]
        bias = (_band_bias(rel_table[l], x_sample.shape[1]), _band_bias_t(rel_table[l]))
        cache = (cache_k_a[l], cache_v_a[l], cache_logf_a[l], cache_k_b[l], cache_v_b[l])
        y_s, *rest_s, w16 = _layer(y_s, mod_s, cache, w32, bias, nb=nb_s, tr=x_sample.shape[1])
        y_p, *rest_p, _ = _layer(y_p, mod_p, None, dict(small, **w16), bias, nb=1, tr=min(512, x_prompt.shape[1]))
        outs_p.append(rest_p)
        outs_s.append(rest_s)
    stack = lambda outs, i: jnp.stack([o[i] for o in outs])
    return (y_p, y_s, *[stack(outs_p, i) for i in range(5)], *[stack(outs_s, i) for i in range(5)])
```

```python
import functools

import jax
import jax.numpy as jnp
from jax import lax
from jax.experimental import pallas as pl
from jax.experimental.pallas import tpu as pltpu

CHUNK = 64
LEFT_CHUNKS = 8
LEFT = LEFT_CHUNKS * CHUNK
MAX_REL = 128
EPS = 1e-6
NEG = -1e30
LOG2E = 1.4426950408889634

LANES = 128
BAND_TQ = 256
BAND_WIN = LEFT + BAND_TQ
BAND_EXT = BAND_WIN + LEFT
BAND_ROLL = 2048
CUM_BLK = 256
FOX_PAD = 16
VMEM_LIMIT_BYTES = 56 * 1024 * 1024

F32 = jnp.float32
BF16 = jnp.bfloat16


def _params(*semantics):
    return pltpu.CompilerParams(dimension_semantics=semantics, vmem_limit_bytes=VMEM_LIMIT_BYTES)


def _tile(n, want):
    if n <= want:
        return n
    t = (want // LANES) * LANES
    while t >= LANES:
        if n % t == 0:
            return t
        t -= LANES
    raise ValueError(f"no lane-aligned tile for {n}")


def _ada_kernel(c_ref, w_ref, b_ref, o_ref):
    c = c_ref[...]
    a = (c * jax.nn.sigmoid(c)).astype(BF16)
    o_ref[...] = jnp.dot(a, w_ref[...].astype(BF16), preferred_element_type=F32) + b_ref[...]


def _ada(c, w_ada, b_ada):
    m, d = c.shape
    n = w_ada.shape[1]
    tn = _tile(n, 512)
    return pl.pallas_call(
        _ada_kernel,
        grid=(n // tn,),
        in_specs=[pl.BlockSpec((m, d), lambda j: (0, 0)),
                  pl.BlockSpec((d, tn), lambda j: (0, j)),
                  pl.BlockSpec((1, tn), lambda j: (0, j))],
        out_specs=pl.BlockSpec((m, tn), lambda j: (0, j)),
        out_shape=jax.ShapeDtypeStruct((m, n), F32),
        compiler_params=_params("arbitrary"),
        name="ada_mod",
    )(c, w_ada, b_ada.reshape(1, n))


def _modnorm_kernel(x_ref, sc_ref, sh_ref, g_ref, o_ref):
    x = x_ref[...]
    ms = jnp.mean(x * x, axis=-1, keepdims=True)
    y = x * lax.rsqrt(ms + EPS) * g_ref[...]
    o_ref[...] = (y * (1.0 + sc_ref[...]) + sh_ref[...]).astype(o_ref.dtype)


def _modnorm(x, scale, shift, g, nb, tr):
    b, t, d = x.shape
    bs = pl.BlockSpec((nb, 1, d), lambda i, j: (i, 0, 0))
    return pl.pallas_call(
        _modnorm_kernel,
        grid=(b // nb, t // tr),
        in_specs=[pl.BlockSpec((nb, tr, d), lambda i, j: (i, j, 0)), bs, bs,
                  pl.BlockSpec((1, 1, d), lambda i, j: (0, 0, 0))],
        out_specs=pl.BlockSpec((nb, tr, d), lambda i, j: (i, j, 0)),
        out_shape=jax.ShapeDtypeStruct((b, t, d), BF16),
        compiler_params=_params("arbitrary", "arbitrary"),
        name="modnorm",
    )(x, scale.reshape(b, 1, d), shift.reshape(b, 1, d), g.reshape(1, 1, d))


def _bf16_weights(w_ref, copy_ref, transposed=False):
    if copy_ref is None:
        return w_ref[...]
    w = w_ref[...]
    w = (w.T if transposed else w).astype(BF16)
    copy_ref[...] = w
    return w


def _proj_kernel(a_ref, w_ref, g_ref, *o_refs, do_rms, dh, nh, rows32, tiles_per_seq, emit_w):
    if emit_w:
        *o_refs, wcopy_ref = o_refs
    w = _bf16_weights(w_ref, wcopy_ref if emit_w else None, transposed=True)
    acc = jnp.dot(a_ref[...], w, preferred_element_type=F32)
    o16_ref = o_refs[-1]
    tm, tn = acc.shape
    hpt = tn // dh
    g = g_ref[...]
    ys = []
    for hh in range(hpt):
        y = acc[:, hh * dh:(hh + 1) * dh]
        if do_rms:
            ms = jnp.mean(y * y, axis=-1, keepdims=True)
            y = y * lax.rsqrt(ms + EPS) * g
        o16_ref[:, hh * dh:(hh + 1) * dh] = y.astype(o16_ref.dtype)
        ys.append(y)
    if rows32:
        o32_ref = o_refs[0]
        h0 = pl.program_id(1) * hpt

        def store32():
            for hh in range(hpt):
                o32_ref[pl.ds(h0 + hh, rows32, stride=nh), :] = ys[hh][tm - rows32:, :]

        if rows32 == tm:
            store32()
        else:
            pl.when(pl.program_id(0) % tiles_per_seq == tiles_per_seq - 1)(store32)


def _proj(a, w, col0, n, g, *, do_rms, dh, tm, tn, f32_rows=0, seq_len=None):
    m, k = a.shape
    tm = min(tm, m)
    tn = _tile(n, tn)
    nh = n // dh
    j0 = col0 // tn
    emit_w = w.dtype != BF16
    assert col0 % tn == 0 and m % tm == 0 and (not emit_w or m == tm)
    out_specs = [pl.BlockSpec((tm, tn), lambda i, j: (i, j))]
    out_shape = [jax.ShapeDtypeStruct((m, n), BF16)]
    rows32, tps = 0, 1
    if f32_rows:
        tps = seq_len // tm
        assert seq_len % tm == 0
        if f32_rows == seq_len:
            rows32 = tm
            out_specs.insert(0, pl.BlockSpec((tm * nh, dh), lambda i, j: (i, 0)))
            out_shape.insert(0, jax.ShapeDtypeStruct((m * nh, dh), F32))
        else:
            rows32 = f32_rows
            assert rows32 <= tm
            out_specs.insert(0, pl.BlockSpec((rows32 * nh, dh), lambda i, j: (i // tps, 0)))
            out_shape.insert(0, jax.ShapeDtypeStruct((m // seq_len * rows32 * nh, dh), F32))
    if emit_w:
        out_specs.append(pl.BlockSpec((k, tn), lambda i, j: (0, j)))
        out_shape.append(jax.ShapeDtypeStruct((k, n), BF16))
    return pl.pallas_call(
        functools.partial(_proj_kernel, do_rms=do_rms, dh=dh, nh=nh, rows32=rows32, tiles_per_seq=tps,
                          emit_w=emit_w),
        grid=(m // tm, n // tn),
        in_specs=[pl.BlockSpec((tm, k), lambda i, j: (i, 0)),
                  pl.BlockSpec((tn, k), lambda i, j: (j0 + j, 0)) if emit_w else
                  pl.BlockSpec((k, tn), lambda i, j: (0, j0 + j)),
                  pl.BlockSpec((1, dh), lambda i, j: (0, 0))],
        out_specs=out_specs,
        out_shape=out_shape,
        compiler_params=_params("arbitrary", "arbitrary"),
        name="proj",
    )(a, w, g.reshape(1, dh).astype(F32))


def _gate_kernel(a_ref, w_ref, b_ref, o_ref):
    z = _qk(a_ref[...], w_ref[...].astype(BF16)) + b_ref[...]
    o_ref[...] = jnp.minimum(z, 0.0) - jnp.log1p(jnp.exp(-jnp.abs(z)))


def _forget_gate(a, w_t, row0, b_f, tm):
    m, k = a.shape
    nh = b_f.shape[0]
    tm = min(tm, m)
    assert row0 % nh == 0
    return pl.pallas_call(
        _gate_kernel,
        grid=(m // tm,),
        in_specs=[pl.BlockSpec((tm, k), lambda i: (i, 0)),
                  pl.BlockSpec((nh, k), lambda i: (row0 // nh, 0)),
                  pl.BlockSpec((1, nh), lambda i: (0, 0))],
        out_specs=pl.BlockSpec((tm, nh), lambda i: (i, 0)),
        out_shape=jax.ShapeDtypeStruct((m, nh), F32),
        compiler_params=_params("arbitrary"),
        name="forget_gate",
    )(a, w_t, b_f.reshape(1, nh).astype(F32))


def _cumsum_kernel(x_ref, o_ref):
    h, tp = x_ref.shape[1], x_ref.shape[2]
    r = lax.broadcasted_iota(jnp.int32, (CUM_BLK, CUM_BLK), 0)
    c = lax.broadcasted_iota(jnp.int32, (CUM_BLK, CUM_BLK), 1)
    tri = (r <= c).astype(BF16)
    carry = jnp.zeros((h, 1), F32)
    for blk in range(tp // CUM_BLK):
        x = x_ref[0, :, blk * CUM_BLK:(blk + 1) * CUM_BLK]
        hi = x.astype(BF16)
        r1 = x - hi.astype(F32)
        mid = r1.astype(BF16)
        lo = (r1 - mid.astype(F32)).astype(BF16)
        cs = (jnp.dot(hi, tri, preferred_element_type=F32)
              + jnp.dot(mid, tri, preferred_element_type=F32)
              + jnp.dot(lo, tri, preferred_element_type=F32)) + carry
        o_ref[0, :, blk * CUM_BLK:(blk + 1) * CUM_BLK] = cs
        carry = cs[:, CUM_BLK - 1:CUM_BLK]


def _cumsum_t(logf_t):
    b, h, tp = logf_t.shape
    return pl.pallas_call(
        _cumsum_kernel,
        grid=(b,),
        in_specs=[pl.BlockSpec((1, h, tp), lambda i: (i, 0, 0))],
        out_specs=pl.BlockSpec((1, h, tp), lambda i: (i, 0, 0)),
        out_shape=jax.ShapeDtypeStruct((b, h, tp), F32),
        compiler_params=_params("arbitrary"),
        name="logf_cumsum",
    )(logf_t)


def _head_column(cc, h):
    lane = lax.broadcasted_iota(jnp.int32, cc.shape, 1)
    return jnp.sum(jnp.where(lane == h, cc, 0.0), axis=1, keepdims=True)


def _qk(q, k):
    return lax.dot_general(q, k, (((1,), (1,)), ((), ())), preferred_element_type=F32)


def _fox_kernel(q_ref, k_ref, v_ref, cc_ref, o_ref, vt_sc, ck_sc, *, tq, hb, dh):
    hg = pl.program_id(1)
    i = pl.program_id(2)
    nk = vt_sc.shape[1]
    heads = range(hb)
    hcols = [slice(hh * dh, (hh + 1) * dh) for hh in heads]

    @pl.when(i == 0)
    def _():
        ones = jnp.ones((FOX_PAD, tq), BF16)
        for kb in range(nk):
            rows = slice(kb * tq, (kb + 1) * tq)
            cc = cc_ref[0, rows, :]
            for hh in heads:
                vt_sc[hh, kb, :dh, :] = v_ref[0, rows, hcols[hh]].T
                vt_sc[hh, kb, dh:, :] = ones
                ck_sc[hh, rows, :] = jnp.broadcast_to(_head_column(cc, hg * hb + hh) * (-LOG2E), (tq, LANES))

    qts = [q_ref[0, :, cols].T for cols in hcols]

    def scores(kb):
        off = pl.multiple_of(kb * tq, tq)
        out = []
        for hh in heads:
            s = jnp.dot(k_ref[0, pl.ds(off, tq), hcols[hh]], qts[hh], preferred_element_type=F32)
            ck = ck_sc[hh, pl.ds(off, tq), :]
            out.append(s + jnp.concatenate([ck] * (tq // LANES), axis=1))
        return out

    def step(kb, state, mask=None):
        stats = []
        for s, (m, _) in zip(scores(kb), state):
            if mask is not None:
                s = jnp.where(mask, s, NEG)
            m_new = jnp.maximum(m, jnp.max(s, axis=0, keepdims=True))
            stats.append((m_new, jnp.exp2(m - m_new), jnp.exp2(s - m_new).astype(BF16)))
        return [(m_new, alpha * acc + jnp.dot(vt_sc[hh, kb], p, preferred_element_type=F32))
                for hh, ((m_new, alpha, p), (_, acc)) in enumerate(zip(stats, state))]

    state = [(jnp.full((1, tq), NEG, F32), jnp.zeros((dh + FOX_PAD, tq), F32)) for _ in heads]
    state = lax.fori_loop(0, i, step, state)
    key = lax.broadcasted_iota(jnp.int32, (tq, tq), 0)
    qry = lax.broadcasted_iota(jnp.int32, (tq, tq), 1)
    state = step(i, state, mask=key <= qry)
    for cols, (_, acc) in zip(hcols, state):
        o_ref[0, :, cols] = (acc[:dh] / acc[dh:dh + 1]).T.astype(o_ref.dtype)


def _fox_prompt(q, k, v, cum_col, nh, dh, tq, hb):
    b, t, _ = q.shape
    hb = min(hb, nh)
    assert t % tq == 0 and tq % LANES == 0
    qspec = pl.BlockSpec((1, tq, hb * dh), lambda bb, h, i: (bb, i, h))
    kspec = pl.BlockSpec((1, t, hb * dh), lambda bb, h, i: (bb, 0, h))
    return pl.pallas_call(
        functools.partial(_fox_kernel, tq=tq, hb=hb, dh=dh),
        grid=(b, nh // hb, t // tq),
        in_specs=[qspec, kspec, kspec, pl.BlockSpec((1, t, nh), lambda bb, h, i: (bb, 0, 0))],
        out_specs=qspec,
        out_shape=jax.ShapeDtypeStruct((b, t, nh * dh), BF16),
        scratch_shapes=[pltpu.VMEM((hb, t // tq, dh + FOX_PAD, tq), BF16), pltpu.VMEM((hb, t, LANES), F32)],
        compiler_params=_params("arbitrary", "arbitrary", "arbitrary"),
        name="fox_prompt",
    )(q, k, v, cum_col)


def _head_rows(ref, h, n, nh):
    return ref[0, pl.ds(h, n, stride=nh), :].astype(BF16)


def _fox_s_kernel(q_ref, kc_ref, vc_ref, kn_ref, vn_ref, cc_ref, crc_ref, crn_ref, o_ref, m_sc, l_sc, acc_sc,
                  *, nh, dh, tc, s_len):
    c = pl.program_id(1)
    cc = cc_ref[0][:s_len]
    r = lax.broadcasted_iota(jnp.int32, (s_len, s_len), 0)
    col = lax.broadcasted_iota(jnp.int32, (s_len, s_len), 1)

    @pl.when(c == 0)
    def _():
        hcols = [slice(h * dh, (h + 1) * dh) for h in range(nh)]
        ss = [_qk(q_ref[0, :, cols], kn_ref[0, :, cols]) + (cc[:, h:h + 1] - crn_ref[0, h:h + 1, :s_len])
              for h, cols in enumerate(hcols)]
        ps = []
        for h, s in enumerate(ss):
            s = jnp.where(col <= r, s, NEG)
            m = jnp.max(s, axis=1, keepdims=True)
            p = jnp.exp(s - m)
            m_sc[h] = m
            l_sc[h] = jnp.sum(p, axis=1, keepdims=True)
            ps.append(p.astype(BF16))
        for h, p in enumerate(ps):
            acc_sc[h] = jnp.dot(p, vn_ref[0, :, hcols[h]], preferred_element_type=F32)

    ss = [_qk(q_ref[0, :, h * dh:(h + 1) * dh], _head_rows(kc_ref, h, tc, nh))
          + (cc[:, h:h + 1] - crc_ref[0, h:h + 1, :]) for h in range(nh)]
    ps = []
    for h, s in enumerate(ss):
        m_old = m_sc[h]
        m_new = jnp.maximum(m_old, jnp.max(s, axis=1, keepdims=True))
        alpha = jnp.exp(m_old - m_new)
        p = jnp.exp(s - m_new)
        m_sc[h] = m_new
        l_sc[h] = alpha * l_sc[h] + jnp.sum(p, axis=1, keepdims=True)
        ps.append((alpha, p.astype(BF16)))
    for h, (alpha, p) in enumerate(ps):
        acc_sc[h] = alpha * acc_sc[h] + jnp.dot(p, _head_rows(vc_ref, h, tc, nh), preferred_element_type=F32)

    @pl.when(c == pl.num_programs(1) - 1)
    def _():
        for h in range(nh):
            o_ref[0, :, h * dh:(h + 1) * dh] = (acc_sc[h] / l_sc[h]).astype(o_ref.dtype)


def _fox_sample(q, kc, vc, kn, vn, cum_col, cum_t, nh, dh, tc):
    b, s_len, _ = q.shape
    past = kc.shape[1] // nh
    tp = cum_col.shape[1]
    tc = min(tc, past)
    assert past % CUM_BLK == 0 and tp - past == CUM_BLK and past % tc == 0 and s_len <= CUM_BLK
    new = pl.BlockSpec((1, s_len, nh * dh), lambda bb, c: (bb, 0, 0))
    cache = pl.BlockSpec((1, tc * nh, dh), lambda bb, c: (bb, c, 0))
    return pl.pallas_call(
        functools.partial(_fox_s_kernel, nh=nh, dh=dh, tc=tc, s_len=s_len),
        grid=(b, past // tc),
        in_specs=[new, cache, cache, new, new,
                  pl.BlockSpec((1, CUM_BLK, nh), lambda bb, c: (bb, past // CUM_BLK, 0)),
                  pl.BlockSpec((1, nh, tc), lambda bb, c: (bb, 0, c)),
                  pl.BlockSpec((1, nh, CUM_BLK), lambda bb, c: (bb, 0, past // CUM_BLK))],
        out_specs=new,
        out_shape=jax.ShapeDtypeStruct((b, s_len, nh * dh), BF16),
        scratch_shapes=[pltpu.VMEM((nh, s_len, 1), F32), pltpu.VMEM((nh, s_len, 1), F32),
                        pltpu.VMEM((nh, s_len, dh), F32)],
        compiler_params=_params("arbitrary", "arbitrary"),
        name="fox_sample",
    )(q, kc, vc, kn, vn, cum_col, cum_t, cum_t)


def _band_bias_kernel(gf_ref, o_ref):
    f8 = jnp.broadcast_to(gf_ref[0], (8, BAND_ROLL))
    sub = lax.broadcasted_iota(jnp.int32, (8, BAND_ROLL), 0)
    base = f8
    for bb in range(1, 8):
        base = jnp.where(sub == bb, pltpu.roll(f8, bb, axis=1), base)
    for a in range(o_ref.shape[1] // 8):
        rows = base if a == 0 else pltpu.roll(base, 8 * a, axis=1)
        o_ref[0, 8 * a:8 * a + 8, :] = rows[:, :BAND_WIN]


def _band_bias(rel_table, rows):
    nh = rel_table.shape[0]
    assert rows % 8 == 0 and rows <= CHUNK
    u = jnp.arange(BAND_ROLL)
    v = jnp.where(u < BAND_ROLL - BAND_TQ, u, u - BAND_ROLL)
    idx = jnp.clip(LEFT - v, -MAX_REL, MAX_REL) + MAX_REL
    gf = jnp.take(rel_table.astype(F32), idx, axis=1).reshape(nh, 1, BAND_ROLL)
    return pl.pallas_call(
        _band_bias_kernel,
        grid=(nh,),
        in_specs=[pl.BlockSpec((1, 1, BAND_ROLL), lambda h: (h, 0, 0))],
        out_specs=pl.BlockSpec((1, rows, BAND_WIN), lambda h: (h, 0, 0)),
        out_shape=jax.ShapeDtypeStruct((nh, rows, BAND_WIN), F32),
        compiler_params=_params("arbitrary"),
        name="band_bias",
    )(gf)


def _band_bias_t_kernel(gf_ref, o_ref):
    f8 = jnp.broadcast_to(gf_ref[0], (8, BAND_ROLL))
    sub = lax.broadcasted_iota(jnp.int32, (8, BAND_ROLL), 0)
    base = f8
    for bb in range(1, 8):
        base = jnp.where(sub == bb, pltpu.roll(f8, bb, axis=1), base)
    key8 = lax.broadcasted_iota(jnp.int32, (8, BAND_TQ), 0)
    qry_chunk = lax.broadcasted_iota(jnp.int32, (8, BAND_TQ), 1) // CHUNK
    for a in range(BAND_EXT // 8):
        rows = base if a == 0 else pltpu.roll(base, 8 * a, axis=1)
        cd = qry_chunk - ((key8 + 8 * a) // CHUNK - LEFT_CHUNKS)
        rows = jnp.where((cd >= 0) & (cd <= LEFT_CHUNKS), rows[:, :BAND_TQ], NEG)
        for w in range(3):
            r0 = 8 * a - (LEFT - w * BAND_TQ)
            if 0 <= r0 < BAND_WIN:
                o_ref[0, w, r0:r0 + 8, :] = rows


def _band_bias_t(rel_table):
    nh = rel_table.shape[0]
    u = jnp.arange(BAND_ROLL)
    v = jnp.where(u < 2 * BAND_TQ, u, u - BAND_ROLL)
    idx = jnp.clip(v + LEFT, -MAX_REL, MAX_REL) + MAX_REL
    gf = (jnp.take(rel_table.astype(F32), idx, axis=1) * LOG2E).reshape(nh, 1, BAND_ROLL)
    return pl.pallas_call(
        _band_bias_t_kernel,
        grid=(nh,),
        in_specs=[pl.BlockSpec((1, 1, BAND_ROLL), lambda h: (h, 0, 0))],
        out_specs=pl.BlockSpec((1, 3, BAND_WIN, BAND_TQ), lambda h: (h, 0, 0, 0)),
        out_shape=jax.ShapeDtypeStruct((nh, 3, BAND_WIN, BAND_TQ), F32),
        compiler_params=_params("arbitrary"),
        name="band_bias_t",
    )(gf)


def _softmax_pv(score_parts, value_parts):
    probs = []
    for parts in score_parts:
        m = functools.reduce(jnp.maximum, [jnp.max(s, axis=1, keepdims=True) for s in parts])
        ps = [jnp.exp(s - m) for s in parts]
        l = functools.reduce(jnp.add, [jnp.sum(p, axis=1, keepdims=True) for p in ps])
        probs.append(([p.astype(BF16) for p in ps], l))
    outs = []
    for (ps, l), vals in zip(probs, value_parts):
        o = functools.reduce(jnp.add, [jnp.dot(p, v(), preferred_element_type=F32) for p, v in zip(ps, vals)])
        outs.append(o / l)
    return outs


def _band_kernel(q_ref, k_ref, v_ref, e_ref, o_ref, vt_sc, *, hb, dh):
    g = pl.program_id(2)
    nk = vt_sc.shape[1]
    nwin = BAND_WIN // BAND_TQ
    hcols = [slice(hh * dh, (hh + 1) * dh) for hh in range(hb)]

    @pl.when(g == 0)
    def _():
        ones = jnp.ones((FOX_PAD, BAND_TQ), BF16)
        for kb in range(nk):
            for hh, cols in enumerate(hcols):
                vt_sc[hh, kb, :dh, :] = v_ref[0, kb * BAND_TQ:(kb + 1) * BAND_TQ, cols].T
                vt_sc[hh, kb, dh:, :] = ones

    wb = jnp.maximum(g - LEFT // BAND_TQ, 0)
    ws = pl.multiple_of(wb * BAND_TQ, BAND_TQ)
    w = jnp.minimum(g, LEFT // BAND_TQ)
    ss = [jnp.dot(k_ref[0, pl.ds(ws, BAND_WIN), cols], q_ref[0, :, cols].T, preferred_element_type=F32)
          + e_ref[hh, w] for hh, cols in enumerate(hcols)]
    ps = [jnp.exp2(s - jnp.max(s, axis=0, keepdims=True)).astype(BF16) for s in ss]
    for hh, (cols, p) in enumerate(zip(hcols, ps)):
        acc = functools.reduce(jnp.add, [
            jnp.dot(vt_sc[hh, wb + j], p[j * BAND_TQ:(j + 1) * BAND_TQ], preferred_element_type=F32)
            for j in range(nwin)])
        o_ref[0, :, cols] = (acc[:dh] / acc[dh:dh + 1]).T.astype(o_ref.dtype)


def _band_prompt(q, k, v, bias_t, nh, dh, hb):
    b, t, _ = q.shape
    assert t % BAND_TQ == 0 and t >= BAND_WIN
    hb = min(hb, nh)
    qspec = pl.BlockSpec((1, BAND_TQ, hb * dh), lambda h, bb, g: (bb, g, h))
    kspec = pl.BlockSpec((1, t, hb * dh), lambda h, bb, g: (bb, 0, h))
    return pl.pallas_call(
        functools.partial(_band_kernel, hb=hb, dh=dh),
        grid=(nh // hb, b, t // BAND_TQ),
        in_specs=[qspec, kspec, kspec,
                  pl.BlockSpec((hb, 3, BAND_WIN, BAND_TQ), lambda h, bb, g: (h, 0, 0, 0),
                               pipeline_mode=pl.Buffered(1))],
        out_specs=qspec,
        out_shape=jax.ShapeDtypeStruct((b, t, nh * dh), BF16),
        scratch_shapes=[pltpu.VMEM((hb, t // BAND_TQ, dh + FOX_PAD, BAND_TQ), BF16)],
        compiler_params=_params("arbitrary", "arbitrary", "arbitrary"),
        name="band_prompt",
    )(q, k, v, bias_t)


def _band_s_kernel(q_ref, kc_ref, vc_ref, kn_ref, vn_ref, e_ref, o_ref, *, nh, dh, keep, s_len):
    hcols = [slice(h * dh, (h + 1) * dh) for h in range(nh)]
    scores, values = [], []
    for h, cols in enumerate(hcols):
        q = q_ref[0, :, cols]
        e = e_ref[h]
        scores.append([_qk(q, _head_rows(kc_ref, h, keep, nh)) + e[:, :keep],
                       _qk(q, kn_ref[0, :, cols]) + e[:, keep:keep + s_len]])
        values.append([functools.partial(_head_rows, vc_ref, h, keep, nh),
                       functools.partial(lambda cols: vn_ref[0, :, cols], cols)])
    for cols, o in zip(hcols, _softmax_pv(scores, values)):
        o_ref[0, :, cols] = o.astype(o_ref.dtype)


def _band_sample(q, kc, vc, kn, vn, bias, nh, dh):
    b, s_len, _ = q.shape
    keep = kc.shape[1] // nh
    assert keep == LEFT and s_len <= CHUNK and s_len % 8 == 0
    new = pl.BlockSpec((1, s_len, nh * dh), lambda bb: (bb, 0, 0))
    cache = pl.BlockSpec((1, keep * nh, dh), lambda bb: (bb, 0, 0))
    return pl.pallas_call(
        functools.partial(_band_s_kernel, nh=nh, dh=dh, keep=keep, s_len=s_len),
        grid=(b,),
        in_specs=[new, cache, cache, new, new,
                  pl.BlockSpec((nh, s_len, BAND_WIN), lambda bb: (0, 0, 0))],
        out_specs=new,
        out_shape=jax.ShapeDtypeStruct((b, s_len, nh * dh), BF16),
        compiler_params=_params("arbitrary"),
        name="band_sample",
    )(q, kc, vc, kn, vn, bias)


def _oproj_kernel(oa_ref, ob_ref, wa_ref, wb_ref, x_ref, gate_ref, o_ref, *wcopy_refs):
    ca, cb = wcopy_refs if wcopy_refs else (None, None)
    acc = (jnp.dot(oa_ref[...], _bf16_weights(wa_ref, ca), preferred_element_type=F32)
           + jnp.dot(ob_ref[...], _bf16_weights(wb_ref, cb), preferred_element_type=F32))
    o_ref[...] = x_ref[...] + gate_ref[...] * acc


def _gate_spec(gate, m, tm, tn, order=lambda fn: fn):
    if gate.ndim == 2:
        return pl.BlockSpec((tm, tn), order(lambda i, j: (i, j)))
    tiles_per_seq = m // gate.shape[0] // tm
    return pl.BlockSpec((None, 1, tn), order(lambda i, j: (i // tiles_per_seq, 0, j)))


def _oproj(oa, ob, wa, wb, x, gate, tm, tn):
    m, da = oa.shape
    db = ob.shape[1]
    (wa, ra), (wb, rb) = wa, wb
    d = wa.shape[1]
    tm = min(tm, m)
    tn = _tile(d, tn)
    emit_w = wa.dtype != BF16
    assert da == db and (not emit_w or m == tm)
    out_specs = [pl.BlockSpec((tm, tn), lambda i, j: (i, j))]
    out_shape = [jax.ShapeDtypeStruct((m, d), F32)]
    if emit_w:
        out_specs += [pl.BlockSpec((da, tn), lambda i, j: (0, j))] * 2
        out_shape += [jax.ShapeDtypeStruct((da, d), BF16)] * 2
    return pl.pallas_call(
        _oproj_kernel,
        grid=(m // tm, d // tn),
        in_specs=[pl.BlockSpec((tm, da), lambda i, j: (i, 0)),
                  pl.BlockSpec((tm, db), lambda i, j: (i, 0)),
                  pl.BlockSpec((da, tn), lambda i, j: (ra, j)),
                  pl.BlockSpec((db, tn), lambda i, j: (rb, j)),
                  pl.BlockSpec((tm, tn), lambda i, j: (i, j)),
                  _gate_spec(gate, m, tm, tn)],
        out_specs=out_specs,
        out_shape=out_shape,
        compiler_params=_params("arbitrary", "arbitrary"),
        name="out_proj",
    )(oa, ob, wa, wb, x, gate)


def _ffn_up_kernel(a_ref, wg_ref, wu_ref, o_ref, *wcopy_refs):
    cg, cu = wcopy_refs if wcopy_refs else (None, None)
    a = a_ref[...]
    g = jnp.dot(a, _bf16_weights(wg_ref, cg), preferred_element_type=F32)
    u = jnp.dot(a, _bf16_weights(wu_ref, cu), preferred_element_type=F32)
    o_ref[...] = (g * jax.nn.sigmoid(g) * u).astype(o_ref.dtype)


def _ffn_up(a, wg, wu, f, tm, tn):
    m, k = a.shape
    (wg, cg), (wu, cu) = wg, wu
    tm = min(tm, m)
    tn = _tile(f, tn)
    jg, ju = cg // tn, cu // tn
    emit_w = wg.dtype != BF16
    assert cg % tn == 0 and cu % tn == 0 and (not emit_w or m == tm)
    out_specs = [pl.BlockSpec((tm, tn), lambda i, j: (i, j))]
    out_shape = [jax.ShapeDtypeStruct((m, f), BF16)]
    if emit_w:
        out_specs += [pl.BlockSpec((k, tn), lambda i, j: (0, j))] * 2
        out_shape += [jax.ShapeDtypeStruct((k, f), BF16)] * 2
    return pl.pallas_call(
        _ffn_up_kernel,
        grid=(m // tm, f // tn),
        in_specs=[pl.BlockSpec((tm, k), lambda i, j: (i, 0)),
                  pl.BlockSpec((k, tn), lambda i, j: (0, jg + j)),
                  pl.BlockSpec((k, tn), lambda i, j: (0, ju + j))],
        out_specs=out_specs,
        out_shape=out_shape,
        compiler_params=_params("arbitrary", "arbitrary"),
        name="ffn_up",
    )(a, wg, wu)


def _ffn_down_kernel(a_ref, w_ref, x_ref, gate_ref, o_ref, *wcopy_refs):
    w = _bf16_weights(w_ref, wcopy_refs[0] if wcopy_refs else None)
    acc = jnp.dot(a_ref[...], w, preferred_element_type=F32)
    o_ref[...] = x_ref[...] + gate_ref[...] * acc


def _ffn_down(a, w_down, x, gate, tm, tn):
    m, f = a.shape
    d = w_down.shape[1]
    tm = min(tm, m)
    tn = _tile(d, tn)
    emit_w = w_down.dtype != BF16
    assert not emit_w or m == tm
    order = lambda fn: (lambda j, i: fn(i, j))
    w_mode = dict(pipeline_mode=pl.Buffered(1)) if m > tm else {}
    a_mode = dict(pipeline_mode=pl.Buffered(1)) if m == tm else {}
    out_specs = [pl.BlockSpec((tm, tn), order(lambda i, j: (i, j)))]
    out_shape = [jax.ShapeDtypeStruct((m, d), F32)]
    if emit_w:
        out_specs.append(pl.BlockSpec((f, tn), order(lambda i, j: (0, j))))
        out_shape.append(jax.ShapeDtypeStruct((f, d), BF16))
    return pl.pallas_call(
        _ffn_down_kernel,
        grid=(d // tn, m // tm),
        in_specs=[pl.BlockSpec((tm, f), order(lambda i, j: (i, 0)), **a_mode),
                  pl.BlockSpec((f, tn), order(lambda i, j: (0, j)), **w_mode),
                  pl.BlockSpec((tm, tn), order(lambda i, j: (i, j))),
                  _gate_spec(gate, m, tm, tn, order)],
        out_specs=out_specs,
        out_shape=out_shape,
        compiler_params=_params("arbitrary", "arbitrary"),
        name="ffn_down",
    )(a, w_down, x, gate)


def _layer(x, mod, cache, w, bias, *, nb, tr):
    b, t, d = x.shape
    m = b * t
    shift_a, scale_a, gate_a, shift_f, scale_f, gate_f = mod
    dh = w["g_q_a"].shape[0]
    nh_a = w["b_f"].shape[0]
    d_a = d_b = w["d_a"]
    nh_b = d_b // dh
    f = w["f"]
    scale = dh ** -0.5
    prompt = cache is None
    tm = 1024
    w16 = {}

    def gates(gt):
        if prompt:
            return gt.reshape(b, 1, d)
        return jnp.broadcast_to(gt[:, None, :], (b, t, d)).reshape(m, d)

    h = _modnorm(x, scale_a, shift_a, w["g_attn"], nb, tr).reshape(m, d)
    keep_p = min(LEFT, t)
    full32 = dict(tm=tm, tn=512, f32_rows=t, seq_len=t) if prompt else dict(tm=tm, tn=512, f32_rows=m, seq_len=m)
    tail32 = dict(tm=tm, tn=1024, f32_rows=keep_p, seq_len=t) if prompt else full32

    def pj(name, n, g, **kw):
        arr, col0 = w[name]
        kw.setdefault("tn", 1024 if prompt else 512)
        outs = _proj(h, arr, col0, n, g, dh=dh, **kw)
        if not prompt:
            *outs, wcopy = outs
            w16[name] = (wcopy, 0)
        return outs

    (q_a,) = pj("w_q_a", d_a, w["g_q_a"] * (scale * LOG2E if prompt else scale), do_rms=True, tm=tm)
    k_a32, k_a = pj("w_k_a", d_a, w["g_k_a"], do_rms=True, **full32)
    v_a32, v_a = pj("w_v_a", d_a, w["g_k_a"], do_rms=False, **full32)
    (q_b,) = pj("w_q_b", d_b, w["g_q_b"] * (scale * LOG2E if prompt else scale), do_rms=True, tm=tm)
    k_b32, k_b = pj("w_k_b", d_b, w["g_k_b"], do_rms=True, **tail32)
    v_b32, v_b = pj("w_v_b", d_b, w["g_k_b"], do_rms=False, **tail32)
    logf = _forget_gate(h, *w["w_f"], w["b_f"], tm).reshape(b, t, nh_a)

    r3 = lambda z: z.reshape(b, t, -1)
    if prompt:
        cum_t = _cumsum_t(logf.transpose(0, 2, 1))
        tq = 256
        o_a = _fox_prompt(r3(q_a), r3(k_a), r3(v_a), cum_t.transpose(0, 2, 1), nh_a, dh, tq, 8)
        o_b = _band_prompt(r3(q_b), r3(k_b), r3(v_b), bias[1], nh_b, dh, 8)
        new_k_b, new_v_b = k_b32.reshape(b, keep_p, nh_b, dh), v_b32.reshape(b, keep_p, nh_b, dh)
    else:
        ck_a, cv_a, clogf_a, ck_b, cv_b = cache
        past = ck_a.shape[1]
        tp = past + CUM_BLK
        lf_all = jnp.concatenate([clogf_a.astype(F32), logf, jnp.zeros((b, tp - past - t, nh_a), F32)], axis=1)
        cum_t = _cumsum_t(lf_all.transpose(0, 2, 1))
        o_a = _fox_sample(r3(q_a), ck_a.reshape(b, past * nh_a, dh), cv_a.reshape(b, past * nh_a, dh),
                          r3(k_a), r3(v_a), cum_t.transpose(0, 2, 1), cum_t, nh_a, dh, 1024)
        keep = ck_b.shape[1]
        o_b = _band_sample(r3(q_b), ck_b.reshape(b, keep * nh_b, dh), cv_b.reshape(b, keep * nh_b, dh),
                           r3(k_b), r3(v_b), bias[0], nh_b, dh)
        new_b = lambda ck, z: jnp.concatenate([ck, z.reshape(b, t, nh_b, dh).astype(ck.dtype)], axis=1)[:, -keep:]
        new_k_b, new_v_b = new_b(ck_b, k_b32), new_b(cv_b, v_b32)

    x1, *wo16 = _oproj(o_a.reshape(m, d_a), o_b.reshape(m, d_b), w["w_oa"], w["w_ob"], x.reshape(m, d),
                       gates(gate_a), tm, 1024 if prompt else 512)
    h2 = _modnorm(x1.reshape(b, t, d), scale_f, shift_f, w["g_ffn"], nb, tr).reshape(m, d)
    act, *wgu16 = _ffn_up(h2, w["w_g"], w["w_u"], f, 2 * tm, 256)
    y, *wd16 = _ffn_down(act, w["w_down"], x1, gates(gate_f), 512, 1024 if prompt else 256)
    if not prompt:
        w16.update(w_oa=(wo16[0], 0), w_ob=(wo16[1], 0), w_g=(wgu16[0], 0), w_u=(wgu16[1], 0), w_down=wd16[0])
    return (y.reshape(b, t, d), k_a32.reshape(b, t, nh_a, dh), v_a32.reshape(b, t, nh_a, dh), logf,
            new_k_b, new_v_b, w16)


def kernel(x_prompt, x_sample, cache_k_a, cache_v_a, cache_logf_a, cache_k_b, cache_v_b, c_prompt, c_sample,
           w_ada, b_ada, g_attn, g_ffn, w_in, b_f, g_q_a, g_k_a, g_q_b, g_k_b, rel_table, w_o, w_gu, w_down):
    depth = w_ada.shape[0]
    d = x_prompt.shape[-1]
    nb_p, nb_s = c_prompt.shape[0], c_sample.shape[0]
    nh_a = b_f.shape[1]
    n_qkv = w_in.shape[2] - nh_a
    d_a = n_qkv // 6
    f = w_down.shape[1]
    assert 2 * d_a == w_o.shape[1] and w_gu.shape[2] == 2 * f
    y_p, y_s = x_prompt, x_sample
    outs_p, outs_s = [], []
    for l in range(depth):
        w_in_t = jnp.swapaxes(w_in[l], 0, 1)
        small = dict(w_f=(w_in_t, n_qkv), b_f=b_f[l], g_attn=g_attn[l], g_ffn=g_ffn[l], g_q_a=g_q_a[l], g_k_a=g_k_a[l],
                     g_q_b=g_q_b[l], g_k_b=g_k_b[l], d_a=d_a, f=f)
        names = ("w_q_a", "w_k_a", "w_v_a", "w_q_b", "w_k_b", "w_v_b")
        w32 = dict(small, w_oa=(w_o[l], 0), w_ob=(w_o[l], 1), w_g=(w_gu[l], 0), w_u=(w_gu[l], f), w_down=w_down[l],
                   **{name: (w_in_t, i * d_a) for i, name in enumerate(names)})
        c_all = jnp.concatenate([c_prompt, c_sample], axis=0)
        pad = (-c_all.shape[0]) % 16
        mod = _ada(jnp.pad(c_all, ((0, pad), (0, 0))), w_ada[l], b_ada[l])
        mod_p = [mod[:nb_p, i * d:(i + 1) * d] for i in range(6)]
        mod_s = [mod[nb_p:nb_p + nb_s, i * d:(i + 1) * d] for i in range(6)]
        bias = (_band_bias(rel_table[l], x_sample.shape[1]), _band_bias_t(rel_table[l]))
        cache = (cache_k_a[l], cache_v_a[l], cache_logf_a[l], cache_k_b[l], cache_v_b[l])
        y_s, *rest_s, w16 = _layer(y_s, mod_s, cache, w32, bias, nb=nb_s, tr=x_sample.shape[1])
        y_p, *rest_p, _ = _layer(y_p, mod_p, None, dict(small, **w16), bias, nb=1, tr=min(512, x_prompt.shape[1]))
        outs_p.append(rest_p)
        outs_s.append(rest_s)
    stack = lambda outs, i: jnp.stack([o[i] for o in outs])
    return (y_p, y_s, *[stack(outs_p, i) for i in range(5)], *[stack(outs_s, i) for i in range(5)])
```

```python
import functools

import jax
import jax.numpy as jnp
from jax import lax
from jax.experimental import pallas as pl
from jax.experimental.pallas import tpu as pltpu

CHUNK = 64
LEFT_CHUNKS = 8
LEFT = LEFT_CHUNKS * CHUNK
MAX_REL = 128
EPS = 1e-6
NEG = -1e30
LOG2E = 1.4426950408889634

LANES = 128
BAND_TQ = 256
BAND_WIN = LEFT + BAND_TQ
BAND_EXT = BAND_WIN + LEFT
BAND_ROLL = 2048
CUM_BLK = 256
FOX_PAD = 16
VMEM_LIMIT_BYTES = 56 * 1024 * 1024

F32 = jnp.float32
BF16 = jnp.bfloat16


def _params(*semantics):
    return pltpu.CompilerParams(dimension_semantics=semantics, vmem_limit_bytes=VMEM_LIMIT_BYTES)


def _tile(n, want):
    if n <= want:
        return n
    t = (want // LANES) * LANES
    while t >= LANES:
        if n % t == 0:
            return t
        t -= LANES
    raise ValueError(f"no lane-aligned tile for {n}")


def _ada_kernel(c_ref, w_ref, b_ref, o_ref):
    c = c_ref[...]
    a = (c * jax.nn.sigmoid(c)).astype(BF16)
    o_ref[...] = jnp.dot(a, w_ref[...].astype(BF16), preferred_element_type=F32) + b_ref[...]


def _ada(c, w_ada, b_ada):
    m, d = c.shape
    n = w_ada.shape[1]
    tn = _tile(n, 512)
    return pl.pallas_call(
        _ada_kernel,
        grid=(n // tn,),
        in_specs=[pl.BlockSpec((m, d), lambda j: (0, 0)),
                  pl.BlockSpec((d, tn), lambda j: (0, j)),
                  pl.BlockSpec((1, tn), lambda j: (0, j))],
        out_specs=pl.BlockSpec((m, tn), lambda j: (0, j)),
        out_shape=jax.ShapeDtypeStruct((m, n), F32),
        compiler_params=_params("arbitrary"),
        name="ada_mod",
    )(c, w_ada, b_ada.reshape(1, n))


def _modnorm_kernel(x_ref, sc_ref, sh_ref, g_ref, o_ref):
    x = x_ref[...]
    ms = jnp.mean(x * x, axis=-1, keepdims=True)
    y = x * lax.rsqrt(ms + EPS) * g_ref[...]
    o_ref[...] = (y * (1.0 + sc_ref[...]) + sh_ref[...]).astype(o_ref.dtype)


def _modnorm(x, scale, shift, g, nb, tr):
    b, t, d = x.shape
    bs = pl.BlockSpec((nb, 1, d), lambda i, j: (i, 0, 0))
    return pl.pallas_call(
        _modnorm_kernel,
        grid=(b // nb, t // tr),
        in_specs=[pl.BlockSpec((nb, tr, d), lambda i, j: (i, j, 0)), bs, bs,
                  pl.BlockSpec((1, 1, d), lambda i, j: (0, 0, 0))],
        out_specs=pl.BlockSpec((nb, tr, d), lambda i, j: (i, j, 0)),
        out_shape=jax.ShapeDtypeStruct((b, t, d), BF16),
        compiler_params=_params("arbitrary", "arbitrary"),
        name="modnorm",
    )(x, scale.reshape(b, 1, d), shift.reshape(b, 1, d), g.reshape(1, 1, d))


def _bf16_weights(w_ref, copy_ref, transposed=False):
    if copy_ref is None:
        return w_ref[...]
    w = w_ref[...]
    w = (w.T if transposed else w).astype(BF16)
    copy_ref[...] = w
    return w


def _proj_kernel(a_ref, w_ref, g_ref, *o_refs, do_rms, dh, nh, rows32, tiles_per_seq, emit_w):
    if emit_w:
        *o_refs, wcopy_ref = o_refs
    w = _bf16_weights(w_ref, wcopy_ref if emit_w else None, transposed=True)
    acc = jnp.dot(a_ref[...], w, preferred_element_type=F32)
    o16_ref = o_refs[-1]
    tm, tn = acc.shape
    hpt = tn // dh
    g = g_ref[...]
    ys = []
    for hh in range(hpt):
        y = acc[:, hh * dh:(hh + 1) * dh]
        if do_rms:
            ms = jnp.mean(y * y, axis=-1, keepdims=True)
            y = y * lax.rsqrt(ms + EPS) * g
        o16_ref[:, hh * dh:(hh + 1) * dh] = y.astype(o16_ref.dtype)
        ys.append(y)
    if rows32:
        o32_ref = o_refs[0]
        h0 = pl.program_id(1) * hpt

        def store32():
            for hh in range(hpt):
                o32_ref[pl.ds(h0 + hh, rows32, stride=nh), :] = ys[hh][tm - rows32:, :]

        if rows32 == tm:
            store32()
        else:
            pl.when(pl.program_id(0) % tiles_per_seq == tiles_per_seq - 1)(store32)


def _proj(a, w, col0, n, g, *, do_rms, dh, tm, tn, f32_rows=0, seq_len=None):
    m, k = a.shape
    tm = min(tm, m)
    tn = _tile(n, tn)
    nh = n // dh
    j0 = col0 // tn
    emit_w = w.dtype != BF16
    assert col0 % tn == 0 and m % tm == 0 and (not emit_w or m == tm)
    out_specs = [pl.BlockSpec((tm, tn), lambda i, j: (i, j))]
    out_shape = [jax.ShapeDtypeStruct((m, n), BF16)]
    rows32, tps = 0, 1
    if f32_rows:
        tps = seq_len // tm
        assert seq_len % tm == 0
        if f32_rows == seq_len:
            rows32 = tm
            out_specs.insert(0, pl.BlockSpec((tm * nh, dh), lambda i, j: (i, 0)))
            out_shape.insert(0, jax.ShapeDtypeStruct((m * nh, dh), F32))
        else:
            rows32 = f32_rows
            assert rows32 <= tm
            out_specs.insert(0, pl.BlockSpec((rows32 * nh, dh), lambda i, j: (i // tps, 0)))
            out_shape.insert(0, jax.ShapeDtypeStruct((m // seq_len * rows32 * nh, dh), F32))
    if emit_w:
        out_specs.append(pl.BlockSpec((k, tn), lambda i, j: (0, j)))
        out_shape.append(jax.ShapeDtypeStruct((k, n), BF16))
    return pl.pallas_call(
        functools.partial(_proj_kernel, do_rms=do_rms, dh=dh, nh=nh, rows32=rows32, tiles_per_seq=tps,
                          emit_w=emit_w),
        grid=(m // tm, n // tn),
        in_specs=[pl.BlockSpec((tm, k), lambda i, j: (i, 0)),
                  pl.BlockSpec((tn, k), lambda i, j: (j0 + j, 0)) if emit_w else
                  pl.BlockSpec((k, tn), lambda i, j: (0, j0 + j)),
                  pl.BlockSpec((1, dh), lambda i, j: (0, 0))],
        out_specs=out_specs,
        out_shape=out_shape,
        compiler_params=_params("arbitrary", "arbitrary"),
        name="proj",
    )(a, w, g.reshape(1, dh).astype(F32))


def _gate_kernel(a_ref, w_ref, b_ref, o_ref):
    z = _qk(a_ref[...], w_ref[...].astype(BF16)) + b_ref[...]
    o_ref[...] = jnp.minimum(z, 0.0) - jnp.log1p(jnp.exp(-jnp.abs(z)))


def _forget_gate(a, w_t, row0, b_f, tm):
    m, k = a.shape
    nh = b_f.shape[0]
    tm = min(tm, m)
    assert row0 % nh == 0
    return pl.pallas_call(
        _gate_kernel,
        grid=(m // tm,),
        in_specs=[pl.BlockSpec((tm, k), lambda i: (i, 0)),
                  pl.BlockSpec((nh, k), lambda i: (row0 // nh, 0)),
                  pl.BlockSpec((1, nh), lambda i: (0, 0))],
        out_specs=pl.BlockSpec((tm, nh), lambda i: (i, 0)),
        out_shape=jax.ShapeDtypeStruct((m, nh), F32),
        compiler_params=_params("arbitrary"),
        name="forget_gate",
    )(a, w_t, b_f.reshape(1, nh).astype(F32))


def _cumsum_kernel(x_ref, o_ref):
    h, tp = x_ref.shape[1], x_ref.shape[2]
    r = lax.broadcasted_iota(jnp.int32, (CUM_BLK, CUM_BLK), 0)
    c = lax.broadcasted_iota(jnp.int32, (CUM_BLK, CUM_BLK), 1)
    tri = (r <= c).astype(BF16)
    carry = jnp.zeros((h, 1), F32)
    for blk in range(tp // CUM_BLK):
        x = x_ref[0, :, blk * CUM_BLK:(blk + 1) * CUM_BLK]
        hi = x.astype(BF16)
        r1 = x - hi.astype(F32)
        mid = r1.astype(BF16)
        lo = (r1 - mid.astype(F32)).astype(BF16)
        cs = (jnp.dot(hi, tri, preferred_element_type=F32)
              + jnp.dot(mid, tri, preferred_element_type=F32)
              + jnp.dot(lo, tri, preferred_element_type=F32)) + carry
        o_ref[0, :, blk * CUM_BLK:(blk + 1) * CUM_BLK] = cs
        carry = cs[:, CUM_BLK - 1:CUM_BLK]


def _cumsum_t(logf_t):
    b, h, tp = logf_t.shape
    return pl.pallas_call(
        _cumsum_kernel,
        grid=(b,),
        in_specs=[pl.BlockSpec((1, h, tp), lambda i: (i, 0, 0))],
        out_specs=pl.BlockSpec((1, h, tp), lambda i: (i, 0, 0)),
        out_shape=jax.ShapeDtypeStruct((b, h, tp), F32),
        compiler_params=_params("arbitrary"),
        name="logf_cumsum",
    )(logf_t)


def _head_column(cc, h):
    lane = lax.broadcasted_iota(jnp.int32, cc.shape, 1)
    return jnp.sum(jnp.where(lane == h, cc, 0.0), axis=1, keepdims=True)


def _qk(q, k):
    return lax.dot_general(q, k, (((1,), (1,)), ((), ())), preferred_element_type=F32)


def _fox_kernel(q_ref, k_ref, v_ref, cc_ref, o_ref, vt_sc, ck_sc, *, tq, hb, dh):
    hg = pl.program_id(1)
    i = pl.program_id(2)
    nk = vt_sc.shape[1]
    heads = range(hb)
    hcols = [slice(hh * dh, (hh + 1) * dh) for hh in heads]

    @pl.when(i == 0)
    def _():
        ones = jnp.ones((FOX_PAD, tq), BF16)
        for kb in range(nk):
            rows = slice(kb * tq, (kb + 1) * tq)
            cc = cc_ref[0, rows, :]
            for hh in heads:
                vt_sc[hh, kb, :dh, :] = v_ref[0, rows, hcols[hh]].T
                vt_sc[hh, kb, dh:, :] = ones
                ck_sc[hh, rows, :] = jnp.broadcast_to(_head_column(cc, hg * hb + hh) * (-LOG2E), (tq, LANES))

    qts = [q_ref[0, :, cols].T for cols in hcols]

    def scores(kb):
        off = pl.multiple_of(kb * tq, tq)
        out = []
        for hh in heads:
            s = jnp.dot(k_ref[0, pl.ds(off, tq), hcols[hh]], qts[hh], preferred_element_type=F32)
            ck = ck_sc[hh, pl.ds(off, tq), :]
            out.append(s + jnp.concatenate([ck] * (tq // LANES), axis=1))
        return out

    def step(kb, state, mask=None):
        stats = []
        for s, (m, _) in zip(scores(kb), state):
            if mask is not None:
                s = jnp.where(mask, s, NEG)
            m_new = jnp.maximum(m, jnp.max(s, axis=0, keepdims=True))
            stats.append((m_new, jnp.exp2(m - m_new), jnp.exp2(s - m_new).astype(BF16)))
        return [(m_new, alpha * acc + jnp.dot(vt_sc[hh, kb], p, preferred_element_type=F32))
                for hh, ((m_new, alpha, p), (_, acc)) in enumerate(zip(stats, state))]

    state = [(jnp.full((1, tq), NEG, F32), jnp.zeros((dh + FOX_PAD, tq), F32)) for _ in heads]
    state = lax.fori_loop(0, i, step, state)
    key = lax.broadcasted_iota(jnp.int32, (tq, tq), 0)
    qry = lax.broadcasted_iota(jnp.int32, (tq, tq), 1)
    state = step(i, state, mask=key <= qry)
    for cols, (_, acc) in zip(hcols, state):
        o_ref[0, :, cols] = (acc[:dh] / acc[dh:dh + 1]).T.astype(o_ref.dtype)


def _fox_prompt(q, k, v, cum_col, nh, dh, tq, hb):
    b, t, _ = q.shape
    hb = min(hb, nh)
    assert t % tq == 0 and tq % LANES == 0
    qspec = pl.BlockSpec((1, tq, hb * dh), lambda bb, h, i: (bb, i, h))
    kspec = pl.BlockSpec((1, t, hb * dh), lambda bb, h, i: (bb, 0, h))
    return pl.pallas_call(
        functools.partial(_fox_kernel, tq=tq, hb=hb, dh=dh),
        grid=(b, nh // hb, t // tq),
        in_specs=[qspec, kspec, kspec, pl.BlockSpec((1, t, nh), lambda bb, h, i: (bb, 0, 0))],
        out_specs=qspec,
        out_shape=jax.ShapeDtypeStruct((b, t, nh * dh), BF16),
        scratch_shapes=[pltpu.VMEM((hb, t // tq, dh + FOX_PAD, tq), BF16), pltpu.VMEM((hb, t, LANES), F32)],
        compiler_params=_params("arbitrary", "arbitrary", "arbitrary"),
        name="fox_prompt",
    )(q, k, v, cum_col)


def _head_major(ref, n, nh):
    return pltpu.einshape("tjd->jtd", ref[0].reshape(n, nh, ref.shape[2]))


def _fox_s_kernel(q_ref, kc_ref, vc_ref, kn_ref, vn_ref, cc_ref, crc_ref, crn_ref, o_ref, m_sc, l_sc, acc_sc,
                  *, nh, dh, tc, s_len):
    c = pl.program_id(1)
    cc = cc_ref[0][:s_len]
    r = lax.broadcasted_iota(jnp.int32, (s_len, s_len), 0)
    col = lax.broadcasted_iota(jnp.int32, (s_len, s_len), 1)

    @pl.when(c == 0)
    def _():
        hcols = [slice(h * dh, (h + 1) * dh) for h in range(nh)]
        ss = [_qk(q_ref[0, :, cols], kn_ref[0, :, cols]) + (cc[:, h:h + 1] - crn_ref[0, h:h + 1, :s_len])
              for h, cols in enumerate(hcols)]
        ps = []
        for h, s in enumerate(ss):
            s = jnp.where(col <= r, s, NEG)
            m = jnp.max(s, axis=1, keepdims=True)
            p = jnp.exp(s - m)
            m_sc[h] = m
            l_sc[h] = jnp.sum(p, axis=1, keepdims=True)
            ps.append(p.astype(BF16))
        for h, p in enumerate(ps):
            acc_sc[h] = jnp.dot(p, vn_ref[0, :, hcols[h]], preferred_element_type=F32)

    kc = _head_major(kc_ref, tc, nh)
    vc = _head_major(vc_ref, tc, nh)
    ss = [_qk(q_ref[0, :, h * dh:(h + 1) * dh], kc[h].astype(BF16))
          + (cc[:, h:h + 1] - crc_ref[0, h:h + 1, :]) for h in range(nh)]
    ps = []
    for h, s in enumerate(ss):
        m_old = m_sc[h]
        m_new = jnp.maximum(m_old, jnp.max(s, axis=1, keepdims=True))
        alpha = jnp.exp(m_old - m_new)
        p = jnp.exp(s - m_new)
        m_sc[h] = m_new
        l_sc[h] = alpha * l_sc[h] + jnp.sum(p, axis=1, keepdims=True)
        ps.append((alpha, p.astype(BF16)))
    for h, (alpha, p) in enumerate(ps):
        acc_sc[h] = alpha * acc_sc[h] + jnp.dot(p, vc[h].astype(BF16), preferred_element_type=F32)

    @pl.when(c == pl.num_programs(1) - 1)
    def _():
        for h in range(nh):
            o_ref[0, :, h * dh:(h + 1) * dh] = (acc_sc[h] / l_sc[h]).astype(o_ref.dtype)


def _fox_sample(q, kc, vc, kn, vn, cum_col, cum_t, nh, dh, tc):
    b, s_len, _ = q.shape
    past = kc.shape[1] // nh
    tp = cum_col.shape[1]
    tc = min(tc, past)
    assert past % CUM_BLK == 0 and tp - past == CUM_BLK and past % tc == 0 and s_len <= CUM_BLK
    new = pl.BlockSpec((1, s_len, nh * dh), lambda bb, c: (bb, 0, 0))
    cache = pl.BlockSpec((1, tc * nh, dh), lambda bb, c: (bb, c, 0))
    return pl.pallas_call(
        functools.partial(_fox_s_kernel, nh=nh, dh=dh, tc=tc, s_len=s_len),
        grid=(b, past // tc),
        in_specs=[new, cache, cache, new, new,
                  pl.BlockSpec((1, CUM_BLK, nh), lambda bb, c: (bb, past // CUM_BLK, 0)),
                  pl.BlockSpec((1, nh, tc), lambda bb, c: (bb, 0, c)),
                  pl.BlockSpec((1, nh, CUM_BLK), lambda bb, c: (bb, 0, past // CUM_BLK))],
        out_specs=new,
        out_shape=jax.ShapeDtypeStruct((b, s_len, nh * dh), BF16),
        scratch_shapes=[pltpu.VMEM((nh, s_len, 1), F32), pltpu.VMEM((nh, s_len, 1), F32),
                        pltpu.VMEM((nh, s_len, dh), F32)],
        compiler_params=_params("arbitrary", "arbitrary"),
        name="fox_sample",
    )(q, kc, vc, kn, vn, cum_col, cum_t, cum_t)


def _band_bias_kernel(gf_ref, o_ref):
    f8 = jnp.broadcast_to(gf_ref[0], (8, BAND_ROLL))
    sub = lax.broadcasted_iota(jnp.int32, (8, BAND_ROLL), 0)
    base = f8
    for bb in range(1, 8):
        base = jnp.where(sub == bb, pltpu.roll(f8, bb, axis=1), base)
    for a in range(o_ref.shape[1] // 8):
        rows = base if a == 0 else pltpu.roll(base, 8 * a, axis=1)
        o_ref[0, 8 * a:8 * a + 8, :] = rows[:, :BAND_WIN]


def _band_bias(rel_table, rows):
    nh = rel_table.shape[0]
    assert rows % 8 == 0 and rows <= CHUNK
    u = jnp.arange(BAND_ROLL)
    v = jnp.where(u < BAND_ROLL - BAND_TQ, u, u - BAND_ROLL)
    idx = jnp.clip(LEFT - v, -MAX_REL, MAX_REL) + MAX_REL
    gf = jnp.take(rel_table.astype(F32), idx, axis=1).reshape(nh, 1, BAND_ROLL)
    return pl.pallas_call(
        _band_bias_kernel,
        grid=(nh,),
        in_specs=[pl.BlockSpec((1, 1, BAND_ROLL), lambda h: (h, 0, 0))],
        out_specs=pl.BlockSpec((1, rows, BAND_WIN), lambda h: (h, 0, 0)),
        out_shape=jax.ShapeDtypeStruct((nh, rows, BAND_WIN), F32),
        compiler_params=_params("arbitrary"),
        name="band_bias",
    )(gf)


def _band_bias_t_kernel(gf_ref, o_ref):
    f8 = jnp.broadcast_to(gf_ref[0], (8, BAND_ROLL))
    sub = lax.broadcasted_iota(jnp.int32, (8, BAND_ROLL), 0)
    base = f8
    for bb in range(1, 8):
        base = jnp.where(sub == bb, pltpu.roll(f8, bb, axis=1), base)
    key8 = lax.broadcasted_iota(jnp.int32, (8, BAND_TQ), 0)
    qry_chunk = lax.broadcasted_iota(jnp.int32, (8, BAND_TQ), 1) // CHUNK
    for a in range(BAND_EXT // 8):
        rows = base if a == 0 else pltpu.roll(base, 8 * a, axis=1)
        cd = qry_chunk - ((key8 + 8 * a) // CHUNK - LEFT_CHUNKS)
        rows = jnp.where((cd >= 0) & (cd <= LEFT_CHUNKS), rows[:, :BAND_TQ], NEG)
        for w in range(3):
            r0 = 8 * a - (LEFT - w * BAND_TQ)
            if 0 <= r0 < BAND_WIN:
                o_ref[0, w, r0:r0 + 8, :] = rows


def _band_bias_t(rel_table):
    nh = rel_table.shape[0]
    u = jnp.arange(BAND_ROLL)
    v = jnp.where(u < 2 * BAND_TQ, u, u - BAND_ROLL)
    idx = jnp.clip(v + LEFT, -MAX_REL, MAX_REL) + MAX_REL
    gf = (jnp.take(rel_table.astype(F32), idx, axis=1) * LOG2E).reshape(nh, 1, BAND_ROLL)
    return pl.pallas_call(
        _band_bias_t_kernel,
        grid=(nh,),
        in_specs=[pl.BlockSpec((1, 1, BAND_ROLL), lambda h: (h, 0, 0))],
        out_specs=pl.BlockSpec((1, 3, BAND_WIN, BAND_TQ), lambda h: (h, 0, 0, 0)),
        out_shape=jax.ShapeDtypeStruct((nh, 3, BAND_WIN, BAND_TQ), F32),
        compiler_params=_params("arbitrary"),
        name="band_bias_t",
    )(gf)


def _softmax_pv(score_parts, value_parts):
    probs = []
    for parts in score_parts:
        m = functools.reduce(jnp.maximum, [jnp.max(s, axis=1, keepdims=True) for s in parts])
        ps = [jnp.exp(s - m) for s in parts]
        l = functools.reduce(jnp.add, [jnp.sum(p, axis=1, keepdims=True) for p in ps])
        probs.append(([p.astype(BF16) for p in ps], l))
    outs = []
    for (ps, l), vals in zip(probs, value_parts):
        o = functools.reduce(jnp.add, [jnp.dot(p, v(), preferred_element_type=F32) for p, v in zip(ps, vals)])
        outs.append(o / l)
    return outs


def _band_kernel(q_ref, k_ref, v_ref, e_ref, o_ref, vt_sc, *, hb, dh):
    g = pl.program_id(2)
    nk = vt_sc.shape[1]
    nwin = BAND_WIN // BAND_TQ
    hcols = [slice(hh * dh, (hh + 1) * dh) for hh in range(hb)]

    @pl.when(g == 0)
    def _():
        ones = jnp.ones((FOX_PAD, BAND_TQ), BF16)
        for kb in range(nk):
            for hh, cols in enumerate(hcols):
                vt_sc[hh, kb, :dh, :] = v_ref[0, kb * BAND_TQ:(kb + 1) * BAND_TQ, cols].T
                vt_sc[hh, kb, dh:, :] = ones

    wb = jnp.maximum(g - LEFT // BAND_TQ, 0)
    ws = pl.multiple_of(wb * BAND_TQ, BAND_TQ)
    w = jnp.minimum(g, LEFT // BAND_TQ)
    ss = [jnp.dot(k_ref[0, pl.ds(ws, BAND_WIN), cols], q_ref[0, :, cols].T, preferred_element_type=F32)
          + e_ref[hh, w] for hh, cols in enumerate(hcols)]
    ps = [jnp.exp2(s - jnp.max(s, axis=0, keepdims=True)).astype(BF16) for s in ss]
    for hh, (cols, p) in enumerate(zip(hcols, ps)):
        acc = functools.reduce(jnp.add, [
            jnp.dot(vt_sc[hh, wb + j], p[j * BAND_TQ:(j + 1) * BAND_TQ], preferred_element_type=F32)
            for j in range(nwin)])
        o_ref[0, :, cols] = (acc[:dh] / acc[dh:dh + 1]).T.astype(o_ref.dtype)


def _band_prompt(q, k, v, bias_t, nh, dh, hb):
    b, t, _ = q.shape
    assert t % BAND_TQ == 0 and t >= BAND_WIN
    hb = min(hb, nh)
    qspec = pl.BlockSpec((1, BAND_TQ, hb * dh), lambda h, bb, g: (bb, g, h))
    kspec = pl.BlockSpec((1, t, hb * dh), lambda h, bb, g: (bb, 0, h))
    return pl.pallas_call(
        functools.partial(_band_kernel, hb=hb, dh=dh),
        grid=(nh // hb, b, t // BAND_TQ),
        in_specs=[qspec, kspec, kspec,
                  pl.BlockSpec((hb, 3, BAND_WIN, BAND_TQ), lambda h, bb, g: (h, 0, 0, 0),
                               pipeline_mode=pl.Buffered(1))],
        out_specs=qspec,
        out_shape=jax.ShapeDtypeStruct((b, t, nh * dh), BF16),
        scratch_shapes=[pltpu.VMEM((hb, t // BAND_TQ, dh + FOX_PAD, BAND_TQ), BF16)],
        compiler_params=_params("arbitrary", "arbitrary", "arbitrary"),
        name="band_prompt",
    )(q, k, v, bias_t)


def _band_s_kernel(q_ref, kc_ref, vc_ref, kn_ref, vn_ref, e_ref, o_ref, *, nh, dh, keep, s_len):
    hcols = [slice(h * dh, (h + 1) * dh) for h in range(nh)]
    scores, values = [], []
    kc = _head_major(kc_ref, keep, nh)
    vc = _head_major(vc_ref, keep, nh)
    for h, cols in enumerate(hcols):
        q = q_ref[0, :, cols]
        e = e_ref[h]
        scores.append([_qk(q, kc[h].astype(BF16)) + e[:, :keep],
                       _qk(q, kn_ref[0, :, cols]) + e[:, keep:keep + s_len]])
        values.append([functools.partial(lambda h: vc[h].astype(BF16), h),
                       functools.partial(lambda cols: vn_ref[0, :, cols], cols)])
    for cols, o in zip(hcols, _softmax_pv(scores, values)):
        o_ref[0, :, cols] = o.astype(o_ref.dtype)


def _band_sample(q, kc, vc, kn, vn, bias, nh, dh):
    b, s_len, _ = q.shape
    keep = kc.shape[1] // nh
    assert keep == LEFT and s_len <= CHUNK and s_len % 8 == 0
    new = pl.BlockSpec((1, s_len, nh * dh), lambda bb: (bb, 0, 0))
    cache = pl.BlockSpec((1, keep * nh, dh), lambda bb: (bb, 0, 0))
    return pl.pallas_call(
        functools.partial(_band_s_kernel, nh=nh, dh=dh, keep=keep, s_len=s_len),
        grid=(b,),
        in_specs=[new, cache, cache, new, new,
                  pl.BlockSpec((nh, s_len, BAND_WIN), lambda bb: (0, 0, 0))],
        out_specs=new,
        out_shape=jax.ShapeDtypeStruct((b, s_len, nh * dh), BF16),
        compiler_params=_params("arbitrary"),
        name="band_sample",
    )(q, kc, vc, kn, vn, bias)


def _oproj_kernel(oa_ref, ob_ref, wa_ref, wb_ref, x_ref, gate_ref, o_ref, *wcopy_refs):
    ca, cb = wcopy_refs if wcopy_refs else (None, None)
    acc = (jnp.dot(oa_ref[...], _bf16_weights(wa_ref, ca), preferred_element_type=F32)
           + jnp.dot(ob_ref[...], _bf16_weights(wb_ref, cb), preferred_element_type=F32))
    o_ref[...] = x_ref[...] + gate_ref[...] * acc


def _gate_spec(gate, m, tm, tn, order=lambda fn: fn):
    if gate.ndim == 2:
        return pl.BlockSpec((tm, tn), order(lambda i, j: (i, j)))
    tiles_per_seq = m // gate.shape[0] // tm
    return pl.BlockSpec((None, 1, tn), order(lambda i, j: (i // tiles_per_seq, 0, j)))


def _oproj(oa, ob, wa, wb, x, gate, tm, tn):
    m, da = oa.shape
    db = ob.shape[1]
    (wa, ra), (wb, rb) = wa, wb
    d = wa.shape[1]
    tm = min(tm, m)
    tn = _tile(d, tn)
    emit_w = wa.dtype != BF16
    assert da == db and (not emit_w or m == tm)
    out_specs = [pl.BlockSpec((tm, tn), lambda i, j: (i, j))]
    out_shape = [jax.ShapeDtypeStruct((m, d), F32)]
    if emit_w:
        out_specs += [pl.BlockSpec((da, tn), lambda i, j: (0, j))] * 2
        out_shape += [jax.ShapeDtypeStruct((da, d), BF16)] * 2
    return pl.pallas_call(
        _oproj_kernel,
        grid=(m // tm, d // tn),
        in_specs=[pl.BlockSpec((tm, da), lambda i, j: (i, 0)),
                  pl.BlockSpec((tm, db), lambda i, j: (i, 0)),
                  pl.BlockSpec((da, tn), lambda i, j: (ra, j)),
                  pl.BlockSpec((db, tn), lambda i, j: (rb, j)),
                  pl.BlockSpec((tm, tn), lambda i, j: (i, j)),
                  _gate_spec(gate, m, tm, tn)],
        out_specs=out_specs,
        out_shape=out_shape,
        compiler_params=_params("arbitrary", "arbitrary"),
        name="out_proj",
    )(oa, ob, wa, wb, x, gate)


def _ffn_up_kernel(a_ref, wg_ref, wu_ref, o_ref, *wcopy_refs):
    cg, cu = wcopy_refs if wcopy_refs else (None, None)
    a = a_ref[...]
    g = jnp.dot(a, _bf16_weights(wg_ref, cg), preferred_element_type=F32)
    u = jnp.dot(a, _bf16_weights(wu_ref, cu), preferred_element_type=F32)
    o_ref[...] = (g * jax.nn.sigmoid(g) * u).astype(o_ref.dtype)


def _ffn_up(a, wg, wu, f, tm, tn):
    m, k = a.shape
    (wg, cg), (wu, cu) = wg, wu
    tm = min(tm, m)
    tn = _tile(f, tn)
    jg, ju = cg // tn, cu // tn
    emit_w = wg.dtype != BF16
    assert cg % tn == 0 and cu % tn == 0 and (not emit_w or m == tm)
    out_specs = [pl.BlockSpec((tm, tn), lambda i, j: (i, j))]
    out_shape = [jax.ShapeDtypeStruct((m, f), BF16)]
    if emit_w:
        out_specs += [pl.BlockSpec((k, tn), lambda i, j: (0, j))] * 2
        out_shape += [jax.ShapeDtypeStruct((k, f), BF16)] * 2
    return pl.pallas_call(
        _ffn_up_kernel,
        grid=(m // tm, f // tn),
        in_specs=[pl.BlockSpec((tm, k), lambda i, j: (i, 0)),
                  pl.BlockSpec((k, tn), lambda i, j: (0, jg + j)),
                  pl.BlockSpec((k, tn), lambda i, j: (0, ju + j))],
        out_specs=out_specs,
        out_shape=out_shape,
        compiler_params=_params("arbitrary", "arbitrary"),
        name="ffn_up",
    )(a, wg, wu)


def _ffn_down_kernel(a_ref, w_ref, x_ref, gate_ref, o_ref, *wcopy_refs):
    w = _bf16_weights(w_ref, wcopy_refs[0] if wcopy_refs else None)
    acc = jnp.dot(a_ref[...], w, preferred_element_type=F32)
    o_ref[...] = x_ref[...] + gate_ref[...] * acc


def _ffn_down(a, w_down, x, gate, tm, tn):
    m, f = a.shape
    d = w_down.shape[1]
    tm = min(tm, m)
    tn = _tile(d, tn)
    emit_w = w_down.dtype != BF16
    assert not emit_w or m == tm
    order = lambda fn: (lambda j, i: fn(i, j))
    w_mode = dict(pipeline_mode=pl.Buffered(1)) if m > tm else {}
    a_mode = dict(pipeline_mode=pl.Buffered(1)) if m == tm else {}
    out_specs = [pl.BlockSpec((tm, tn), order(lambda i, j: (i, j)))]
    out_shape = [jax.ShapeDtypeStruct((m, d), F32)]
    if emit_w:
        out_specs.append(pl.BlockSpec((f, tn), order(lambda i, j: (0, j))))
        out_shape.append(jax.ShapeDtypeStruct((f, d), BF16))
    return pl.pallas_call(
        _ffn_down_kernel,
        grid=(d // tn, m // tm),
        in_specs=[pl.BlockSpec((tm, f), order(lambda i, j: (i, 0)), **a_mode),
                  pl.BlockSpec((f, tn), order(lambda i, j: (0, j)), **w_mode),
                  pl.BlockSpec((tm, tn), order(lambda i, j: (i, j))),
                  _gate_spec(gate, m, tm, tn, order)],
        out_specs=out_specs,
        out_shape=out_shape,
        compiler_params=_params("arbitrary", "arbitrary"),
        name="ffn_down",
    )(a, w_down, x, gate)


def _layer(x, mod, cache, w, bias, *, nb, tr):
    b, t, d = x.shape
    m = b * t
    shift_a, scale_a, gate_a, shift_f, scale_f, gate_f = mod
    dh = w["g_q_a"].shape[0]
    nh_a = w["b_f"].shape[0]
    d_a = d_b = w["d_a"]
    nh_b = d_b // dh
    f = w["f"]
    scale = dh ** -0.5
    prompt = cache is None
    tm = 1024
    w16 = {}

    def gates(gt):
        if prompt:
            return gt.reshape(b, 1, d)
        return jnp.broadcast_to(gt[:, None, :], (b, t, d)).reshape(m, d)

    h = _modnorm(x, scale_a, shift_a, w["g_attn"], nb, tr).reshape(m, d)
    keep_p = min(LEFT, t)
    full32 = dict(tm=tm, tn=512, f32_rows=t, seq_len=t) if prompt else dict(tm=tm, tn=512, f32_rows=m, seq_len=m)
    tail32 = dict(tm=tm, tn=1024, f32_rows=keep_p, seq_len=t) if prompt else full32

    def pj(name, n, g, **kw):
        arr, col0 = w[name]
        kw.setdefault("tn", 1024 if prompt else 512)
        outs = _proj(h, arr, col0, n, g, dh=dh, **kw)
        if not prompt:
            *outs, wcopy = outs
            w16[name] = (wcopy, 0)
        return outs

    (q_a,) = pj("w_q_a", d_a, w["g_q_a"] * (scale * LOG2E if prompt else scale), do_rms=True, tm=tm)
    k_a32, k_a = pj("w_k_a", d_a, w["g_k_a"], do_rms=True, **full32)
    v_a32, v_a = pj("w_v_a", d_a, w["g_k_a"], do_rms=False, **full32)
    (q_b,) = pj("w_q_b", d_b, w["g_q_b"] * (scale * LOG2E if prompt else scale), do_rms=True, tm=tm)
    k_b32, k_b = pj("w_k_b", d_b, w["g_k_b"], do_rms=True, **tail32)
    v_b32, v_b = pj("w_v_b", d_b, w["g_k_b"], do_rms=False, **tail32)
    logf = _forget_gate(h, *w["w_f"], w["b_f"], tm).reshape(b, t, nh_a)

    r3 = lambda z: z.reshape(b, t, -1)
    if prompt:
        cum_t = _cumsum_t(logf.transpose(0, 2, 1))
        tq = 256
        o_a = _fox_prompt(r3(q_a), r3(k_a), r3(v_a), cum_t.transpose(0, 2, 1), nh_a, dh, tq, 8)
        o_b = _band_prompt(r3(q_b), r3(k_b), r3(v_b), bias[1], nh_b, dh, 8)
        new_k_b, new_v_b = k_b32.reshape(b, keep_p, nh_b, dh), v_b32.reshape(b, keep_p, nh_b, dh)
    else:
        ck_a, cv_a, clogf_a, ck_b, cv_b = cache
        past = ck_a.shape[1]
        tp = past + CUM_BLK
        lf_all = jnp.concatenate([clogf_a.astype(F32), logf, jnp.zeros((b, tp - past - t, nh_a), F32)], axis=1)
        cum_t = _cumsum_t(lf_all.transpose(0, 2, 1))
        o_a = _fox_sample(r3(q_a), ck_a.reshape(b, past * nh_a, dh), cv_a.reshape(b, past * nh_a, dh),
                          r3(k_a), r3(v_a), cum_t.transpose(0, 2, 1), cum_t, nh_a, dh, 1024)
        keep = ck_b.shape[1]
        o_b = _band_sample(r3(q_b), ck_b.reshape(b, keep * nh_b, dh), cv_b.reshape(b, keep * nh_b, dh),
                           r3(k_b), r3(v_b), bias[0], nh_b, dh)
        new_b = lambda ck, z: jnp.concatenate([ck, z.reshape(b, t, nh_b, dh).astype(ck.dtype)], axis=1)[:, -keep:]
        new_k_b, new_v_b = new_b(ck_b, k_b32), new_b(cv_b, v_b32)

    x1, *wo16 = _oproj(o_a.reshape(m, d_a), o_b.reshape(m, d_b), w["w_oa"], w["w_ob"], x.reshape(m, d),
                       gates(gate_a), tm, 1024 if prompt else 512)
    h2 = _modnorm(x1.reshape(b, t, d), scale_f, shift_f, w["g_ffn"], nb, tr).reshape(m, d)
    act, *wgu16 = _ffn_up(h2, w["w_g"], w["w_u"], f, 2 * tm, 256)
    y, *wd16 = _ffn_down(act, w["w_down"], x1, gates(gate_f), 512, 1024 if prompt else 256)
    if not prompt:
        w16.update(w_oa=(wo16[0], 0), w_ob=(wo16[1], 0), w_g=(wgu16[0], 0), w_u=(wgu16[1], 0), w_down=wd16[0])
    return (y.reshape(b, t, d), k_a32.reshape(b, t, nh_a, dh), v_a32.reshape(b, t, nh_a, dh), logf,
            new_k_b, new_v_b, w16)


def kernel(x_prompt, x_sample, cache_k_a, cache_v_a, cache_logf_a, cache_k_b, cache_v_b, c_prompt, c_sample,
           w_ada, b_ada, g_attn, g_ffn, w_in, b_f, g_q_a, g_k_a, g_q_b, g_k_b, rel_table, w_o, w_gu, w_down):
    depth = w_ada.shape[0]
    d = x_prompt.shape[-1]
    nb_p, nb_s = c_prompt.shape[0], c_sample.shape[0]
    nh_a = b_f.shape[1]
    n_qkv = w_in.shape[2] - nh_a
    d_a = n_qkv // 6
    f = w_down.shape[1]
    assert 2 * d_a == w_o.shape[1] and w_gu.shape[2] == 2 * f
    y_p, y_s = x_prompt, x_sample
    outs_p, outs_s = [], []
    for l in range(depth):
        w_in_t = jnp.swapaxes(w_in[l], 0, 1)
        small = dict(w_f=(w_in_t, n_qkv), b_f=b_f[l], g_attn=g_attn[l], g_ffn=g_ffn[l], g_q_a=g_q_a[l], g_k_a=g_k_a[l],
                     g_q_b=g_q_b[l], g_k_b=g_k_b[l], d_a=d_a, f=f)
        names = ("w_q_a", "w_k_a", "w_v_a", "w_q_b", "w_k_b", "w_v_b")
        w32 = dict(small, w_oa=(w_o[l], 0), w_ob=(w_o[l], 1), w_g=(w_gu[l], 0), w_u=(w_gu[l], f), w_down=w_down[l],
                   **{name: (w_in_t, i * d_a) for i, name in enumerate(names)})
        c_all = jnp.concatenate([c_prompt, c_sample], axis=0)
        pad = (-c_all.shape[0]) % 16
        mod = _ada(jnp.pad(c_all, ((0, pad), (0, 0))), w_ada[l], b_ada[l])
        mod_p = [mod[:nb_p, i * d:(i + 1) * d] for i in range(6)]
        mod_s = [mod[nb_p:nb_p + nb_s, i * d:(i + 1) * d] for i in range(6)]
        bias = (_band_bias(rel_table[l], x_sample.shape[1]), _band_bias_t(rel_table[l]))
        cache = (cache_k_a[l], cache_v_a[l], cache_logf_a[l], cache_k_b[l], cache_v_b[l])
        y_s, *rest_s, w16 = _layer(y_s, mod_s, cache, w32, bias, nb=nb_s, tr=x_sample.shape[1])
        y_p, *rest_p, _ = _layer(y_p, mod_p, None, dict(small, **w16), bias, nb=1, tr=min(512, x_prompt.shape[1]))
        outs_p.append(rest_p)
        outs_s.append(rest_s)
    stack = lambda outs, i: jnp.stack([o[i] for o in outs])
    return (y_p, y_s, *[stack(outs_p, i) for i in range(5)], *[stack(outs_s, i) for i in range(5)])
```

```python
import functools

import jax
import jax.numpy as jnp
from jax import lax
from jax.experimental import pallas as pl
from jax.experimental.pallas import tpu as pltpu

CHUNK = 64
LEFT_CHUNKS = 8
LEFT = LEFT_CHUNKS * CHUNK
MAX_REL = 128
EPS = 1e-6
NEG = -1e30
LOG2E = 1.4426950408889634

LANES = 128
BAND_TQ = 256
BAND_WIN = LEFT + BAND_TQ
BAND_EXT = BAND_WIN + LEFT
BAND_ROLL = 2048
CUM_BLK = 256
FOX_PAD = 16
VMEM_LIMIT_BYTES = 56 * 1024 * 1024

F32 = jnp.float32
BF16 = jnp.bfloat16


def _params(*semantics):
    return pltpu.CompilerParams(dimension_semantics=semantics, vmem_limit_bytes=VMEM_LIMIT_BYTES)


def _tile(n, want):
    if n <= want:
        return n
    t = (want // LANES) * LANES
    while t >= LANES:
        if n % t == 0:
            return t
        t -= LANES
    raise ValueError(f"no lane-aligned tile for {n}")


def _ada_kernel(c_ref, w_ref, b_ref, o_ref):
    c = c_ref[...]
    a = (c * jax.nn.sigmoid(c)).astype(BF16)
    o_ref[...] = jnp.dot(a, w_ref[...].astype(BF16), preferred_element_type=F32) + b_ref[...]


def _ada(c, w_ada, b_ada):
    m, d = c.shape
    n = w_ada.shape[1]
    tn = _tile(n, 512)
    return pl.pallas_call(
        _ada_kernel,
        grid=(n // tn,),
        in_specs=[pl.BlockSpec((m, d), lambda j: (0, 0)),
                  pl.BlockSpec((d, tn), lambda j: (0, j)),
                  pl.BlockSpec((1, tn), lambda j: (0, j))],
        out_specs=pl.BlockSpec((m, tn), lambda j: (0, j)),
        out_shape=jax.ShapeDtypeStruct((m, n), F32),
        compiler_params=_params("arbitrary"),
        name="ada_mod",
    )(c, w_ada, b_ada.reshape(1, n))


def _modnorm_kernel(x_ref, sc_ref, sh_ref, g_ref, o_ref):
    x = x_ref[...]
    ms = jnp.mean(x * x, axis=-1, keepdims=True)
    y = x * lax.rsqrt(ms + EPS) * g_ref[...]
    o_ref[...] = (y * (1.0 + sc_ref[...]) + sh_ref[...]).astype(o_ref.dtype)


def _modnorm(x, scale, shift, g, nb, tr):
    b, t, d = x.shape
    bs = pl.BlockSpec((nb, 1, d), lambda i, j: (i, 0, 0))
    return pl.pallas_call(
        _modnorm_kernel,
        grid=(b // nb, t // tr),
        in_specs=[pl.BlockSpec((nb, tr, d), lambda i, j: (i, j, 0)), bs, bs,
                  pl.BlockSpec((1, 1, d), lambda i, j: (0, 0, 0))],
        out_specs=pl.BlockSpec((nb, tr, d), lambda i, j: (i, j, 0)),
        out_shape=jax.ShapeDtypeStruct((b, t, d), BF16),
        compiler_params=_params("arbitrary", "arbitrary"),
        name="modnorm",
    )(x, scale.reshape(b, 1, d), shift.reshape(b, 1, d), g.reshape(1, 1, d))


def _bf16_weights(w_ref, copy_ref, transposed=False):
    if copy_ref is None:
        return w_ref[...]
    w = w_ref[...]
    w = (w.T if transposed else w).astype(BF16)
    copy_ref[...] = w
    return w


def _proj_kernel(a_ref, w_ref, g_ref, *o_refs, do_rms, dh, nh, rows32, tiles_per_seq, emit_w):
    if emit_w:
        *o_refs, wcopy_ref = o_refs
    w = _bf16_weights(w_ref, wcopy_ref if emit_w else None, transposed=True)
    acc = jnp.dot(a_ref[...], w, preferred_element_type=F32)
    o16_ref = o_refs[-1]
    tm, tn = acc.shape
    hpt = tn // dh
    g = g_ref[...]
    ys = []
    for hh in range(hpt):
        y = acc[:, hh * dh:(hh + 1) * dh]
        if do_rms:
            ms = jnp.mean(y * y, axis=-1, keepdims=True)
            y = y * lax.rsqrt(ms + EPS) * g
        o16_ref[:, hh * dh:(hh + 1) * dh] = y.astype(o16_ref.dtype)
        ys.append(y)
    if rows32:
        o32_ref = o_refs[0]
        h0 = pl.program_id(1) * hpt

        def store32():
            for hh in range(hpt):
                o32_ref[pl.ds(h0 + hh, rows32, stride=nh), :] = ys[hh][tm - rows32:, :]

        if rows32 == tm:
            store32()
        else:
            pl.when(pl.program_id(0) % tiles_per_seq == tiles_per_seq - 1)(store32)


def _proj(a, w, col0, n, g, *, do_rms, dh, tm, tn, f32_rows=0, seq_len=None):
    m, k = a.shape
    tm = min(tm, m)
    tn = _tile(n, tn)
    nh = n // dh
    j0 = col0 // tn
    emit_w = w.dtype != BF16
    assert col0 % tn == 0 and m % tm == 0 and (not emit_w or m == tm)
    out_specs = [pl.BlockSpec((tm, tn), lambda i, j: (i, j))]
    out_shape = [jax.ShapeDtypeStruct((m, n), BF16)]
    rows32, tps = 0, 1
    if f32_rows:
        tps = seq_len // tm
        assert seq_len % tm == 0
        if f32_rows == seq_len:
            rows32 = tm
            out_specs.insert(0, pl.BlockSpec((tm * nh, dh), lambda i, j: (i, 0)))
            out_shape.insert(0, jax.ShapeDtypeStruct((m * nh, dh), F32))
        else:
            rows32 = f32_rows
            assert rows32 <= tm
            out_specs.insert(0, pl.BlockSpec((rows32 * nh, dh), lambda i, j: (i // tps, 0)))
            out_shape.insert(0, jax.ShapeDtypeStruct((m // seq_len * rows32 * nh, dh), F32))
    if emit_w:
        out_specs.append(pl.BlockSpec((k, tn), lambda i, j: (0, j)))
        out_shape.append(jax.ShapeDtypeStruct((k, n), BF16))
    return pl.pallas_call(
        functools.partial(_proj_kernel, do_rms=do_rms, dh=dh, nh=nh, rows32=rows32, tiles_per_seq=tps,
                          emit_w=emit_w),
        grid=(m // tm, n // tn),
        in_specs=[pl.BlockSpec((tm, k), lambda i, j: (i, 0)),
                  pl.BlockSpec((tn, k), lambda i, j: (j0 + j, 0)) if emit_w else
                  pl.BlockSpec((k, tn), lambda i, j: (0, j0 + j)),
                  pl.BlockSpec((1, dh), lambda i, j: (0, 0))],
        out_specs=out_specs,
        out_shape=out_shape,
        compiler_params=_params("arbitrary", "arbitrary"),
        name="proj",
    )(a, w, g.reshape(1, dh).astype(F32))


def _gate_kernel(a_ref, w_ref, b_ref, o_ref):
    z = _qk(a_ref[...], w_ref[...].astype(BF16)) + b_ref[...]
    o_ref[...] = jnp.minimum(z, 0.0) - jnp.log1p(jnp.exp(-jnp.abs(z)))


def _forget_gate(a, w_t, row0, b_f, tm):
    m, k = a.shape
    nh = b_f.shape[0]
    tm = min(tm, m)
    assert row0 % nh == 0
    return pl.pallas_call(
        _gate_kernel,
        grid=(m // tm,),
        in_specs=[pl.BlockSpec((tm, k), lambda i: (i, 0)),
                  pl.BlockSpec((nh, k), lambda i: (row0 // nh, 0)),
                  pl.BlockSpec((1, nh), lambda i: (0, 0))],
        out_specs=pl.BlockSpec((tm, nh), lambda i: (i, 0)),
        out_shape=jax.ShapeDtypeStruct((m, nh), F32),
        compiler_params=_params("arbitrary"),
        name="forget_gate",
    )(a, w_t, b_f.reshape(1, nh).astype(F32))


def _cumsum_kernel(x_ref, o_ref):
    h, tp = x_ref.shape[1], x_ref.shape[2]
    r = lax.broadcasted_iota(jnp.int32, (CUM_BLK, CUM_BLK), 0)
    c = lax.broadcasted_iota(jnp.int32, (CUM_BLK, CUM_BLK), 1)
    tri = (r <= c).astype(BF16)
    carry = jnp.zeros((h, 1), F32)
    for blk in range(tp // CUM_BLK):
        x = x_ref[0, :, blk * CUM_BLK:(blk + 1) * CUM_BLK]
        hi = x.astype(BF16)
        r1 = x - hi.astype(F32)
        mid = r1.astype(BF16)
        lo = (r1 - mid.astype(F32)).astype(BF16)
        cs = (jnp.dot(hi, tri, preferred_element_type=F32)
              + jnp.dot(mid, tri, preferred_element_type=F32)
              + jnp.dot(lo, tri, preferred_element_type=F32)) + carry
        o_ref[0, :, blk * CUM_BLK:(blk + 1) * CUM_BLK] = cs
        carry = cs[:, CUM_BLK - 1:CUM_BLK]


def _cumsum_t(logf_t):
    b, h, tp = logf_t.shape
    return pl.pallas_call(
        _cumsum_kernel,
        grid=(b,),
        in_specs=[pl.BlockSpec((1, h, tp), lambda i: (i, 0, 0))],
        out_specs=pl.BlockSpec((1, h, tp), lambda i: (i, 0, 0)),
        out_shape=jax.ShapeDtypeStruct((b, h, tp), F32),
        compiler_params=_params("arbitrary"),
        name="logf_cumsum",
    )(logf_t)


def _head_column(cc, h):
    lane = lax.broadcasted_iota(jnp.int32, cc.shape, 1)
    return jnp.sum(jnp.where(lane == h, cc, 0.0), axis=1, keepdims=True)


def _qk(q, k):
    return lax.dot_general(q, k, (((1,), (1,)), ((), ())), preferred_element_type=F32)


def _fox_kernel(q_ref, k_ref, v_ref, cc_ref, o_ref, vt_sc, ck_sc, *, tq, hb, dh):
    hg = pl.program_id(1)
    i = pl.program_id(2)
    nk = vt_sc.shape[1]
    heads = range(hb)
    hcols = [slice(hh * dh, (hh + 1) * dh) for hh in heads]

    @pl.when(i == 0)
    def _():
        ones = jnp.ones((FOX_PAD, tq), BF16)
        for kb in range(nk):
            rows = slice(kb * tq, (kb + 1) * tq)
            cc = cc_ref[0, rows, :]
            for hh in heads:
                vt_sc[hh, kb, :dh, :] = v_ref[0, rows, hcols[hh]].T
                vt_sc[hh, kb, dh:, :] = ones
                ck_sc[hh, rows, :] = jnp.broadcast_to(_head_column(cc, hg * hb + hh) * (-LOG2E), (tq, LANES))

    qts = [q_ref[0, :, cols].T for cols in hcols]

    def scores(kb):
        off = pl.multiple_of(kb * tq, tq)
        out = []
        for hh in heads:
            s = jnp.dot(k_ref[0, pl.ds(off, tq), hcols[hh]], qts[hh], preferred_element_type=F32)
            ck = ck_sc[hh, pl.ds(off, tq), :]
            out.append(s + jnp.concatenate([ck] * (tq // LANES), axis=1))
        return out

    def step(kb, state, mask=None):
        stats = []
        for s, (m, _) in zip(scores(kb), state):
            if mask is not None:
                s = jnp.where(mask, s, NEG)
            m_new = jnp.maximum(m, jnp.max(s, axis=0, keepdims=True))
            stats.append((m_new, jnp.exp2(m - m_new), jnp.exp2(s - m_new).astype(BF16)))
        return [(m_new, alpha * acc + jnp.dot(vt_sc[hh, kb], p, preferred_element_type=F32))
                for hh, ((m_new, alpha, p), (_, acc)) in enumerate(zip(stats, state))]

    state = [(jnp.full((1, tq), NEG, F32), jnp.zeros((dh + FOX_PAD, tq), F32)) for _ in heads]
    state = lax.fori_loop(0, i, step, state)
    key = lax.broadcasted_iota(jnp.int32, (tq, tq), 0)
    qry = lax.broadcasted_iota(jnp.int32, (tq, tq), 1)
    state = step(i, state, mask=key <= qry)
    for cols, (_, acc) in zip(hcols, state):
        o_ref[0, :, cols] = (acc[:dh] / acc[dh:dh + 1]).T.astype(o_ref.dtype)


def _fox_prompt(q, k, v, cum_col, nh, dh, tq, hb):
    b, t, _ = q.shape
    hb = min(hb, nh)
    assert t % tq == 0 and tq % LANES == 0
    qspec = pl.BlockSpec((1, tq, hb * dh), lambda bb, h, i: (bb, i, h))
    kspec = pl.BlockSpec((1, t, hb * dh), lambda bb, h, i: (bb, 0, h))
    return pl.pallas_call(
        functools.partial(_fox_kernel, tq=tq, hb=hb, dh=dh),
        grid=(b, nh // hb, t // tq),
        in_specs=[qspec, kspec, kspec, pl.BlockSpec((1, t, nh), lambda bb, h, i: (bb, 0, 0))],
        out_specs=qspec,
        out_shape=jax.ShapeDtypeStruct((b, t, nh * dh), BF16),
        scratch_shapes=[pltpu.VMEM((hb, t // tq, dh + FOX_PAD, tq), BF16), pltpu.VMEM((hb, t, LANES), F32)],
        compiler_params=_params("arbitrary", "arbitrary", "arbitrary"),
        name="fox_prompt",
    )(q, k, v, cum_col)


def _head_major(ref, n, nh):
    return pltpu.einshape("tjd->jtd", ref[0].astype(BF16).reshape(n, nh, ref.shape[2]))


def _fox_s_kernel(q_ref, kc_ref, vc_ref, kn_ref, vn_ref, cc_ref, crc_ref, crn_ref, o_ref, m_sc, l_sc, acc_sc,
                  *, nh, dh, tc, s_len):
    c = pl.program_id(1)
    cc = cc_ref[0][:s_len]
    r = lax.broadcasted_iota(jnp.int32, (s_len, s_len), 0)
    col = lax.broadcasted_iota(jnp.int32, (s_len, s_len), 1)

    @pl.when(c == 0)
    def _():
        hcols = [slice(h * dh, (h + 1) * dh) for h in range(nh)]
        ss = [_qk(q_ref[0, :, cols], kn_ref[0, :, cols]) + (cc[:, h:h + 1] - crn_ref[0, h:h + 1, :s_len])
              for h, cols in enumerate(hcols)]
        ps = []
        for h, s in enumerate(ss):
            s = jnp.where(col <= r, s, NEG)
            m = jnp.max(s, axis=1, keepdims=True)
            p = jnp.exp(s - m)
            m_sc[h] = m
            l_sc[h] = jnp.sum(p, axis=1, keepdims=True)
            ps.append(p.astype(BF16))
        for h, p in enumerate(ps):
            acc_sc[h] = jnp.dot(p, vn_ref[0, :, hcols[h]], preferred_element_type=F32)

    kc = _head_major(kc_ref, tc, nh)
    vc = _head_major(vc_ref, tc, nh)
    ss = [_qk(q_ref[0, :, h * dh:(h + 1) * dh], kc[h])
          + (cc[:, h:h + 1] - crc_ref[0, h:h + 1, :]) for h in range(nh)]
    ps = []
    for h, s in enumerate(ss):
        m_old = m_sc[h]
        m_new = jnp.maximum(m_old, jnp.max(s, axis=1, keepdims=True))
        alpha = jnp.exp(m_old - m_new)
        p = jnp.exp(s - m_new)
        m_sc[h] = m_new
        l_sc[h] = alpha * l_sc[h] + jnp.sum(p, axis=1, keepdims=True)
        ps.append((alpha, p.astype(BF16)))
    for h, (alpha, p) in enumerate(ps):
        acc_sc[h] = alpha * acc_sc[h] + jnp.dot(p, vc[h], preferred_element_type=F32)

    @pl.when(c == pl.num_programs(1) - 1)
    def _():
        for h in range(nh):
            o_ref[0, :, h * dh:(h + 1) * dh] = (acc_sc[h] / l_sc[h]).astype(o_ref.dtype)


def _fox_sample(q, kc, vc, kn, vn, cum_col, cum_t, nh, dh, tc):
    b, s_len, _ = q.shape
    past = kc.shape[1] // nh
    tp = cum_col.shape[1]
    tc = min(tc, past)
    assert past % CUM_BLK == 0 and tp - past == CUM_BLK and past % tc == 0 and s_len <= CUM_BLK
    new = pl.BlockSpec((1, s_len, nh * dh), lambda bb, c: (bb, 0, 0))
    cache = pl.BlockSpec((1, tc * nh, dh), lambda bb, c: (bb, c, 0))
    return pl.pallas_call(
        functools.partial(_fox_s_kernel, nh=nh, dh=dh, tc=tc, s_len=s_len),
        grid=(b, past // tc),
        in_specs=[new, cache, cache, new, new,
                  pl.BlockSpec((1, CUM_BLK, nh), lambda bb, c: (bb, past // CUM_BLK, 0)),
                  pl.BlockSpec((1, nh, tc), lambda bb, c: (bb, 0, c)),
                  pl.BlockSpec((1, nh, CUM_BLK), lambda bb, c: (bb, 0, past // CUM_BLK))],
        out_specs=new,
        out_shape=jax.ShapeDtypeStruct((b, s_len, nh * dh), BF16),
        scratch_shapes=[pltpu.VMEM((nh, s_len, 1), F32), pltpu.VMEM((nh, s_len, 1), F32),
                        pltpu.VMEM((nh, s_len, dh), F32)],
        compiler_params=_params("arbitrary", "arbitrary"),
        name="fox_sample",
    )(q, kc, vc, kn, vn, cum_col, cum_t, cum_t)


def _band_bias_kernel(gf_ref, o_ref):
    f8 = jnp.broadcast_to(gf_ref[0], (8, BAND_ROLL))
    sub = lax.broadcasted_iota(jnp.int32, (8, BAND_ROLL), 0)
    base = f8
    for bb in range(1, 8):
        base = jnp.where(sub == bb, pltpu.roll(f8, bb, axis=1), base)
    for a in range(o_ref.shape[1] // 8):
        rows = base if a == 0 else pltpu.roll(base, 8 * a, axis=1)
        o_ref[0, 8 * a:8 * a + 8, :] = rows[:, :BAND_WIN]


def _band_bias(rel_table, rows):
    nh = rel_table.shape[0]
    assert rows % 8 == 0 and rows <= CHUNK
    u = jnp.arange(BAND_ROLL)
    v = jnp.where(u < BAND_ROLL - BAND_TQ, u, u - BAND_ROLL)
    idx = jnp.clip(LEFT - v, -MAX_REL, MAX_REL) + MAX_REL
    gf = jnp.take(rel_table.astype(F32), idx, axis=1).reshape(nh, 1, BAND_ROLL)
    return pl.pallas_call(
        _band_bias_kernel,
        grid=(nh,),
        in_specs=[pl.BlockSpec((1, 1, BAND_ROLL), lambda h: (h, 0, 0))],
        out_specs=pl.BlockSpec((1, rows, BAND_WIN), lambda h: (h, 0, 0)),
        out_shape=jax.ShapeDtypeStruct((nh, rows, BAND_WIN), F32),
        compiler_params=_params("arbitrary"),
        name="band_bias",
    )(gf)


def _band_bias_t_kernel(gf_ref, o_ref):
    f8 = jnp.broadcast_to(gf_ref[0], (8, BAND_ROLL))
    sub = lax.broadcasted_iota(jnp.int32, (8, BAND_ROLL), 0)
    base = f8
    for bb in range(1, 8):
        base = jnp.where(sub == bb, pltpu.roll(f8, bb, axis=1), base)
    key8 = lax.broadcasted_iota(jnp.int32, (8, BAND_TQ), 0)
    qry_chunk = lax.broadcasted_iota(jnp.int32, (8, BAND_TQ), 1) // CHUNK
    for a in range(BAND_EXT // 8):
        rows = base if a == 0 else pltpu.roll(base, 8 * a, axis=1)
        cd = qry_chunk - ((key8 + 8 * a) // CHUNK - LEFT_CHUNKS)
        rows = jnp.where((cd >= 0) & (cd <= LEFT_CHUNKS), rows[:, :BAND_TQ], NEG)
        for w in range(3):
            r0 = 8 * a - (LEFT - w * BAND_TQ)
            if 0 <= r0 < BAND_WIN:
                o_ref[0, w, r0:r0 + 8, :] = rows


def _band_bias_t(rel_table):
    nh = rel_table.shape[0]
    u = jnp.arange(BAND_ROLL)
    v = jnp.where(u < 2 * BAND_TQ, u, u - BAND_ROLL)
    idx = jnp.clip(v + LEFT, -MAX_REL, MAX_REL) + MAX_REL
    gf = (jnp.take(rel_table.astype(F32), idx, axis=1) * LOG2E).reshape(nh, 1, BAND_ROLL)
    return pl.pallas_call(
        _band_bias_t_kernel,
        grid=(nh,),
        in_specs=[pl.BlockSpec((1, 1, BAND_ROLL), lambda h: (h, 0, 0))],
        out_specs=pl.BlockSpec((1, 3, BAND_WIN, BAND_TQ), lambda h: (h, 0, 0, 0)),
        out_shape=jax.ShapeDtypeStruct((nh, 3, BAND_WIN, BAND_TQ), F32),
        compiler_params=_params("arbitrary"),
        name="band_bias_t",
    )(gf)


def _softmax_pv(score_parts, value_parts):
    probs = []
    for parts in score_parts:
        m = functools.reduce(jnp.maximum, [jnp.max(s, axis=1, keepdims=True) for s in parts])
        ps = [jnp.exp(s - m) for s in parts]
        l = functools.reduce(jnp.add, [jnp.sum(p, axis=1, keepdims=True) for p in ps])
        probs.append(([p.astype(BF16) for p in ps], l))
    outs = []
    for (ps, l), vals in zip(probs, value_parts):
        o = functools.reduce(jnp.add, [jnp.dot(p, v(), preferred_element_type=F32) for p, v in zip(ps, vals)])
        outs.append(o / l)
    return outs


def _band_kernel(q_ref, k_ref, v_ref, e_ref, o_ref, vt_sc, *, hb, dh):
    g = pl.program_id(2)
    nk = vt_sc.shape[1]
    nwin = BAND_WIN // BAND_TQ
    hcols = [slice(hh * dh, (hh + 1) * dh) for hh in range(hb)]

    @pl.when(g == 0)
    def _():
        ones = jnp.ones((FOX_PAD, BAND_TQ), BF16)
        for kb in range(nk):
            for hh, cols in enumerate(hcols):
                vt_sc[hh, kb, :dh, :] = v_ref[0, kb * BAND_TQ:(kb + 1) * BAND_TQ, cols].T
                vt_sc[hh, kb, dh:, :] = ones

    wb = jnp.maximum(g - LEFT // BAND_TQ, 0)
    ws = pl.multiple_of(wb * BAND_TQ, BAND_TQ)
    w = jnp.minimum(g, LEFT // BAND_TQ)
    ss = [jnp.dot(k_ref[0, pl.ds(ws, BAND_WIN), cols], q_ref[0, :, cols].T, preferred_element_type=F32)
          + e_ref[hh, w] for hh, cols in enumerate(hcols)]
    ps = [jnp.exp2(s - jnp.max(s, axis=0, keepdims=True)).astype(BF16) for s in ss]
    for hh, (cols, p) in enumerate(zip(hcols, ps)):
        acc = functools.reduce(jnp.add, [
            jnp.dot(vt_sc[hh, wb + j], p[j * BAND_TQ:(j + 1) * BAND_TQ], preferred_element_type=F32)
            for j in range(nwin)])
        o_ref[0, :, cols] = (acc[:dh] / acc[dh:dh + 1]).T.astype(o_ref.dtype)


def _band_prompt(q, k, v, bias_t, nh, dh, hb):
    b, t, _ = q.shape
    assert t % BAND_TQ == 0 and t >= BAND_WIN
    hb = min(hb, nh)
    qspec = pl.BlockSpec((1, BAND_TQ, hb * dh), lambda h, bb, g: (bb, g, h))
    kspec = pl.BlockSpec((1, t, hb * dh), lambda h, bb, g: (bb, 0, h))
    return pl.pallas_call(
        functools.partial(_band_kernel, hb=hb, dh=dh),
        grid=(nh // hb, b, t // BAND_TQ),
        in_specs=[qspec, kspec, kspec,
                  pl.BlockSpec((hb, 3, BAND_WIN, BAND_TQ), lambda h, bb, g: (h, 0, 0, 0),
                               pipeline_mode=pl.Buffered(1))],
        out_specs=qspec,
        out_shape=jax.ShapeDtypeStruct((b, t, nh * dh), BF16),
        scratch_shapes=[pltpu.VMEM((hb, t // BAND_TQ, dh + FOX_PAD, BAND_TQ), BF16)],
        compiler_params=_params("arbitrary", "arbitrary", "arbitrary"),
        name="band_prompt",
    )(q, k, v, bias_t)


def _band_s_kernel(q_ref, kc_ref, vc_ref, kn_ref, vn_ref, kn32_ref, vn32_ref, e_ref, o_ref, nk_ref, nv_ref,
                   *, nh, dh, keep, s_len):
    for new_ref, cache_ref, n32_ref in ((nk_ref, kc_ref, kn32_ref), (nv_ref, vc_ref, vn32_ref)):
        new_ref[0, :(keep - s_len) * nh, :] = cache_ref[0, s_len * nh:, :]
        new_ref[0, (keep - s_len) * nh:, :] = n32_ref[0]
    hcols = [slice(h * dh, (h + 1) * dh) for h in range(nh)]
    scores, values = [], []
    kc = _head_major(kc_ref, keep, nh)
    vc = _head_major(vc_ref, keep, nh)
    for h, cols in enumerate(hcols):
        q = q_ref[0, :, cols]
        e = e_ref[h]
        scores.append([_qk(q, kc[h]) + e[:, :keep],
                       _qk(q, kn_ref[0, :, cols]) + e[:, keep:keep + s_len]])
        values.append([functools.partial(lambda h: vc[h], h),
                       functools.partial(lambda cols: vn_ref[0, :, cols], cols)])
    for cols, o in zip(hcols, _softmax_pv(scores, values)):
        o_ref[0, :, cols] = o.astype(o_ref.dtype)


def _band_sample(q, kc, vc, kn, vn, kn32, vn32, bias, nh, dh):
    b, s_len, _ = q.shape
    keep = kc.shape[1] // nh
    assert keep == LEFT and s_len <= CHUNK and s_len % 8 == 0
    new = pl.BlockSpec((1, s_len, nh * dh), lambda bb: (bb, 0, 0))
    new32 = pl.BlockSpec((1, s_len * nh, dh), lambda bb: (bb, 0, 0))
    cache = pl.BlockSpec((1, keep * nh, dh), lambda bb: (bb, 0, 0))
    return pl.pallas_call(
        functools.partial(_band_s_kernel, nh=nh, dh=dh, keep=keep, s_len=s_len),
        grid=(b,),
        in_specs=[new, cache, cache, new, new, new32, new32,
                  pl.BlockSpec((nh, s_len, BAND_WIN), lambda bb: (0, 0, 0))],
        out_specs=[new, cache, cache],
        out_shape=[jax.ShapeDtypeStruct((b, s_len, nh * dh), BF16)] + [jax.ShapeDtypeStruct(kc.shape, F32)] * 2,
        compiler_params=_params("arbitrary"),
        name="band_sample",
    )(q, kc, vc, kn, vn, kn32, vn32, bias)


def _oproj_kernel(oa_ref, ob_ref, wa_ref, wb_ref, x_ref, gate_ref, o_ref, *wcopy_refs):
    ca, cb = wcopy_refs if wcopy_refs else (None, None)
    acc = (jnp.dot(oa_ref[...], _bf16_weights(wa_ref, ca), preferred_element_type=F32)
           + jnp.dot(ob_ref[...], _bf16_weights(wb_ref, cb), preferred_element_type=F32))
    o_ref[...] = x_ref[...] + gate_ref[...] * acc


def _gate_spec(gate, m, tm, tn, order=lambda fn: fn):
    if gate.ndim == 2:
        return pl.BlockSpec((tm, tn), order(lambda i, j: (i, j)))
    tiles_per_seq = m // gate.shape[0] // tm
    return pl.BlockSpec((None, 1, tn), order(lambda i, j: (i // tiles_per_seq, 0, j)))


def _oproj(oa, ob, wa, wb, x, gate, tm, tn):
    m, da = oa.shape
    db = ob.shape[1]
    (wa, ra), (wb, rb) = wa, wb
    d = wa.shape[1]
    tm = min(tm, m)
    tn = _tile(d, tn)
    emit_w = wa.dtype != BF16
    assert da == db and (not emit_w or m == tm)
    out_specs = [pl.BlockSpec((tm, tn), lambda i, j: (i, j))]
    out_shape = [jax.ShapeDtypeStruct((m, d), F32)]
    if emit_w:
        out_specs += [pl.BlockSpec((da, tn), lambda i, j: (0, j))] * 2
        out_shape += [jax.ShapeDtypeStruct((da, d), BF16)] * 2
    return pl.pallas_call(
        _oproj_kernel,
        grid=(m // tm, d // tn),
        in_specs=[pl.BlockSpec((tm, da), lambda i, j: (i, 0)),
                  pl.BlockSpec((tm, db), lambda i, j: (i, 0)),
                  pl.BlockSpec((da, tn), lambda i, j: (ra, j)),
                  pl.BlockSpec((db, tn), lambda i, j: (rb, j)),
                  pl.BlockSpec((tm, tn), lambda i, j: (i, j)),
                  _gate_spec(gate, m, tm, tn)],
        out_specs=out_specs,
        out_shape=out_shape,
        compiler_params=_params("arbitrary", "arbitrary"),
        name="out_proj",
    )(oa, ob, wa, wb, x, gate)


def _ffn_up_kernel(a_ref, wg_ref, wu_ref, o_ref, *wcopy_refs):
    cg, cu = wcopy_refs if wcopy_refs else (None, None)
    a = a_ref[...]
    g = jnp.dot(a, _bf16_weights(wg_ref, cg), preferred_element_type=F32)
    u = jnp.dot(a, _bf16_weights(wu_ref, cu), preferred_element_type=F32)
    o_ref[...] = (g * jax.nn.sigmoid(g) * u).astype(o_ref.dtype)


def _ffn_up(a, wg, wu, f, tm, tn):
    m, k = a.shape
    (wg, cg), (wu, cu) = wg, wu
    tm = min(tm, m)
    tn = _tile(f, tn)
    jg, ju = cg // tn, cu // tn
    emit_w = wg.dtype != BF16
    assert cg % tn == 0 and cu % tn == 0 and (not emit_w or m == tm)
    out_specs = [pl.BlockSpec((tm, tn), lambda i, j: (i, j))]
    out_shape = [jax.ShapeDtypeStruct((m, f), BF16)]
    if emit_w:
        out_specs += [pl.BlockSpec((k, tn), lambda i, j: (0, j))] * 2
        out_shape += [jax.ShapeDtypeStruct((k, f), BF16)] * 2
    return pl.pallas_call(
        _ffn_up_kernel,
        grid=(m // tm, f // tn),
        in_specs=[pl.BlockSpec((tm, k), lambda i, j: (i, 0)),
                  pl.BlockSpec((k, tn), lambda i, j: (0, jg + j)),
                  pl.BlockSpec((k, tn), lambda i, j: (0, ju + j))],
        out_specs=out_specs,
        out_shape=out_shape,
        compiler_params=_params("arbitrary", "arbitrary"),
        name="ffn_up",
    )(a, wg, wu)


def _ffn_down_kernel(a_ref, w_ref, x_ref, gate_ref, o_ref, *wcopy_refs):
    w = _bf16_weights(w_ref, wcopy_refs[0] if wcopy_refs else None)
    acc = jnp.dot(a_ref[...], w, preferred_element_type=F32)
    o_ref[...] = x_ref[...] + gate_ref[...] * acc


def _ffn_down(a, w_down, x, gate, tm, tn):
    m, f = a.shape
    d = w_down.shape[1]
    tm = min(tm, m)
    tn = _tile(d, tn)
    emit_w = w_down.dtype != BF16
    assert not emit_w or m == tm
    order = lambda fn: (lambda j, i: fn(i, j))
    w_mode = dict(pipeline_mode=pl.Buffered(1)) if m > tm else {}
    a_mode = dict(pipeline_mode=pl.Buffered(1)) if m == tm else {}
    out_specs = [pl.BlockSpec((tm, tn), order(lambda i, j: (i, j)))]
    out_shape = [jax.ShapeDtypeStruct((m, d), F32)]
    if emit_w:
        out_specs.append(pl.BlockSpec((f, tn), order(lambda i, j: (0, j))))
        out_shape.append(jax.ShapeDtypeStruct((f, d), BF16))
    return pl.pallas_call(
        _ffn_down_kernel,
        grid=(d // tn, m // tm),
        in_specs=[pl.BlockSpec((tm, f), order(lambda i, j: (i, 0)), **a_mode),
                  pl.BlockSpec((f, tn), order(lambda i, j: (0, j)), **w_mode),
                  pl.BlockSpec((tm, tn), order(lambda i, j: (i, j))),
                  _gate_spec(gate, m, tm, tn, order)],
        out_specs=out_specs,
        out_shape=out_shape,
        compiler_params=_params("arbitrary", "arbitrary"),
        name="ffn_down",
    )(a, w_down, x, gate)


def _layer(x, mod, cache, w, bias, *, nb, tr):
    b, t, d = x.shape
    m = b * t
    shift_a, scale_a, gate_a, shift_f, scale_f, gate_f = mod
    dh = w["g_q_a"].shape[0]
    nh_a = w["b_f"].shape[0]
    d_a = d_b = w["d_a"]
    nh_b = d_b // dh
    f = w["f"]
    scale = dh ** -0.5
    prompt = cache is None
    tm = 1024
    w16 = {}

    def gates(gt):
        if prompt:
            return gt.reshape(b, 1, d)
        return jnp.broadcast_to(gt[:, None, :], (b, t, d)).reshape(m, d)

    h = _modnorm(x, scale_a, shift_a, w["g_attn"], nb, tr).reshape(m, d)
    keep_p = min(LEFT, t)
    full32 = dict(tm=tm, tn=512, f32_rows=t, seq_len=t) if prompt else dict(tm=tm, tn=512, f32_rows=m, seq_len=m)
    tail32 = dict(tm=tm, tn=1024, f32_rows=keep_p, seq_len=t) if prompt else full32

    def pj(name, n, g, **kw):
        arr, col0 = w[name]
        kw.setdefault("tn", 1024 if prompt else 512)
        outs = _proj(h, arr, col0, n, g, dh=dh, **kw)
        if not prompt:
            *outs, wcopy = outs
            w16[name] = (wcopy, 0)
        return outs

    (q_a,) = pj("w_q_a", d_a, w["g_q_a"] * (scale * LOG2E if prompt else scale), do_rms=True, tm=tm)
    k_a32, k_a = pj("w_k_a", d_a, w["g_k_a"], do_rms=True, **full32)
    v_a32, v_a = pj("w_v_a", d_a, w["g_k_a"], do_rms=False, **full32)
    (q_b,) = pj("w_q_b", d_b, w["g_q_b"] * (scale * LOG2E if prompt else scale), do_rms=True, tm=tm)
    k_b32, k_b = pj("w_k_b", d_b, w["g_k_b"], do_rms=True, **tail32)
    v_b32, v_b = pj("w_v_b", d_b, w["g_k_b"], do_rms=False, **tail32)
    logf = _forget_gate(h, *w["w_f"], w["b_f"], tm).reshape(b, t, nh_a)

    r3 = lambda z: z.reshape(b, t, -1)
    if prompt:
        cum_t = _cumsum_t(logf.transpose(0, 2, 1))
        tq = 256
        o_a = _fox_prompt(r3(q_a), r3(k_a), r3(v_a), cum_t.transpose(0, 2, 1), nh_a, dh, tq, 8)
        o_b = _band_prompt(r3(q_b), r3(k_b), r3(v_b), bias[1], nh_b, dh, 8)
        new_k_b, new_v_b = k_b32.reshape(b, keep_p, nh_b, dh), v_b32.reshape(b, keep_p, nh_b, dh)
    else:
        ck_a, cv_a, clogf_a, ck_b, cv_b = cache
        past = ck_a.shape[1]
        tp = past + CUM_BLK
        lf_all = jnp.concatenate([clogf_a.astype(F32), logf, jnp.zeros((b, tp - past - t, nh_a), F32)], axis=1)
        cum_t = _cumsum_t(lf_all.transpose(0, 2, 1))
        o_a = _fox_sample(r3(q_a), ck_a.reshape(b, past * nh_a, dh), cv_a.reshape(b, past * nh_a, dh),
                          r3(k_a), r3(v_a), cum_t.transpose(0, 2, 1), cum_t, nh_a, dh, 1024)
        keep = ck_b.shape[1]
        o_b, new_k_b, new_v_b = _band_sample(
            r3(q_b), ck_b.reshape(b, keep * nh_b, dh).astype(F32), cv_b.reshape(b, keep * nh_b, dh).astype(F32),
            r3(k_b), r3(v_b), k_b32.reshape(b, t * nh_b, dh), v_b32.reshape(b, t * nh_b, dh), bias[0], nh_b, dh)
        new_k_b = new_k_b.reshape(b, keep, nh_b, dh).astype(ck_b.dtype)
        new_v_b = new_v_b.reshape(b, keep, nh_b, dh).astype(cv_b.dtype)

    x1, *wo16 = _oproj(o_a.reshape(m, d_a), o_b.reshape(m, d_b), w["w_oa"], w["w_ob"], x.reshape(m, d),
                       gates(gate_a), tm, 1024 if prompt else 512)
    h2 = _modnorm(x1.reshape(b, t, d), scale_f, shift_f, w["g_ffn"], nb, tr).reshape(m, d)
    act, *wgu16 = _ffn_up(h2, w["w_g"], w["w_u"], f, 2 * tm, 256)
    y, *wd16 = _ffn_down(act, w["w_down"], x1, gates(gate_f), 512, 1024 if prompt else 256)
    if not prompt:
        w16.update(w_oa=(wo16[0], 0), w_ob=(wo16[1], 0), w_g=(wgu16[0], 0), w_u=(wgu16[1], 0), w_down=wd16[0])
    return (y.reshape(b, t, d), k_a32.reshape(b, t, nh_a, dh), v_a32.reshape(b, t, nh_a, dh), logf,
            new_k_b, new_v_b, w16)


def kernel(x_prompt, x_sample, cache_k_a, cache_v_a, cache_logf_a, cache_k_b, cache_v_b, c_prompt, c_sample,
           w_ada, b_ada, g_attn, g_ffn, w_in, b_f, g_q_a, g_k_a, g_q_b, g_k_b, rel_table, w_o, w_gu, w_down):
    depth = w_ada.shape[0]
    d = x_prompt.shape[-1]
    nb_p, nb_s = c_prompt.shape[0], c_sample.shape[0]
    nh_a = b_f.shape[1]
    n_qkv = w_in.shape[2] - nh_a
    d_a = n_qkv // 6
    f = w_down.shape[1]
    assert 2 * d_a == w_o.shape[1] and w_gu.shape[2] == 2 * f
    y_p, y_s = x_prompt, x_sample
    outs_p, outs_s = [], []
    for l in range(depth):
        w_in_t = jnp.swapaxes(w_in[l], 0, 1)
        small = dict(w_f=(w_in_t, n_qkv), b_f=b_f[l], g_attn=g_attn[l], g_ffn=g_ffn[l], g_q_a=g_q_a[l], g_k_a=g_k_a[l],
                     g_q_b=g_q_b[l], g_k_b=g_k_b[l], d_a=d_a, f=f)
        names = ("w_q_a", "w_k_a", "w_v_a", "w_q_b", "w_k_b", "w_v_b")
        w32 = dict(small, w_oa=(w_o[l], 0), w_ob=(w_o[l], 1), w_g=(w_gu[l], 0), w_u=(w_gu[l], f), w_down=w_down[l],
                   **{name: (w_in_t, i * d_a) for i, name in enumerate(names)})
        c_all = jnp.concatenate([c_prompt, c_sample], axis=0)
        pad = (-c_all.shape[0]) % 16
        mod = _ada(jnp.pad(c_all, ((0, pad), (0, 0))), w_ada[l], b_ada[l])
        mod_p = [mod[:nb_p, i * d:(i + 1) * d] for i in range(6)]
        mod_s = [mod[nb_p:nb_p + nb_s, i * d:(i + 1) * d] for i in range(6)]
        bias = (_band_bias(rel_table[l], x_sample.shape[1]), _band_bias_t(rel_table[l]))
        cache = (cache_k_a[l], cache_v_a[l], cache_logf_a[l], cache_k_b[l], cache_v_b[l])
        y_s, *rest_s, w16 = _layer(y_s, mod_s, cache, w32, bias, nb=nb_s, tr=x_sample.shape[1])
        y_p, *rest_p, _ = _layer(y_p, mod_p, None, dict(small, **w16), bias, nb=1, tr=min(512, x_prompt.shape[1]))
        outs_p.append(rest_p)
        outs_s.append(rest_s)
    stack = lambda outs, i: jnp.stack([o[i] for o in outs])
    return (y_p, y_s, *[stack(outs_p, i) for i in range(5)], *[stack(outs_s, i) for i in range(5)])
```

```python
import functools

import jax
import jax.numpy as jnp
from jax import lax
from jax.experimental import pallas as pl
from jax.experimental.pallas import tpu as pltpu

CHUNK = 64
LEFT_CHUNKS = 8
LEFT = LEFT_CHUNKS * CHUNK
MAX_REL = 128
EPS = 1e-6
NEG = -1e30
LOG2E = 1.4426950408889634

LANES = 128
BAND_TQ = 256
BAND_WIN = LEFT + BAND_TQ
BAND_EXT = BAND_WIN + LEFT
BAND_ROLL = 2048
CUM_BLK = 256
FOX_PAD = 16
VMEM_LIMIT_BYTES = 56 * 1024 * 1024

F32 = jnp.float32
BF16 = jnp.bfloat16


def _params(*semantics):
    return pltpu.CompilerParams(dimension_semantics=semantics, vmem_limit_bytes=VMEM_LIMIT_BYTES)


def _tile(n, want):
    if n <= want:
        return n
    t = (want // LANES) * LANES
    while t >= LANES:
        if n % t == 0:
            return t
        t -= LANES
    raise ValueError(f"no lane-aligned tile for {n}")


def _ada_kernel(c_ref, w_ref, b_ref, o_ref):
    c = c_ref[...]
    a = (c * jax.nn.sigmoid(c)).astype(BF16)
    o_ref[...] = jnp.dot(a, w_ref[...].astype(BF16), preferred_element_type=F32) + b_ref[...]


def _ada(c, w_ada, b_ada):
    m, d = c.shape
    n = w_ada.shape[1]
    tn = _tile(n, 512)
    return pl.pallas_call(
        _ada_kernel,
        grid=(n // tn,),
        in_specs=[pl.BlockSpec((m, d), lambda j: (0, 0)),
                  pl.BlockSpec((d, tn), lambda j: (0, j)),
                  pl.BlockSpec((1, tn), lambda j: (0, j))],
        out_specs=pl.BlockSpec((m, tn), lambda j: (0, j)),
        out_shape=jax.ShapeDtypeStruct((m, n), F32),
        compiler_params=_params("arbitrary"),
        name="ada_mod",
    )(c, w_ada, b_ada.reshape(1, n))


def _modnorm_kernel(x_ref, sc_ref, sh_ref, g_ref, o_ref):
    x = x_ref[...]
    ms = jnp.mean(x * x, axis=-1, keepdims=True)
    y = x * lax.rsqrt(ms + EPS) * g_ref[...]
    o_ref[...] = (y * (1.0 + sc_ref[...]) + sh_ref[...]).astype(o_ref.dtype)


def _modnorm(x, scale, shift, g, nb, tr):
    b, t, d = x.shape
    bs = pl.BlockSpec((nb, 1, d), lambda i, j: (i, 0, 0))
    return pl.pallas_call(
        _modnorm_kernel,
        grid=(b // nb, t // tr),
        in_specs=[pl.BlockSpec((nb, tr, d), lambda i, j: (i, j, 0)), bs, bs,
                  pl.BlockSpec((1, 1, d), lambda i, j: (0, 0, 0))],
        out_specs=pl.BlockSpec((nb, tr, d), lambda i, j: (i, j, 0)),
        out_shape=jax.ShapeDtypeStruct((b, t, d), BF16),
        compiler_params=_params("arbitrary", "arbitrary"),
        name="modnorm",
    )(x, scale.reshape(b, 1, d), shift.reshape(b, 1, d), g.reshape(1, 1, d))


def _bf16_weights(w_ref, copy_ref, transposed=False):
    if copy_ref is None:
        return w_ref[...]
    w = w_ref[...]
    w = (w.T if transposed else w).astype(BF16)
    copy_ref[...] = w
    return w


def _proj_kernel(a_ref, w_ref, g_ref, *o_refs, do_rms, dh, nh, rows32, tiles_per_seq, emit_w):
    if emit_w:
        *o_refs, wcopy_ref = o_refs
    w = _bf16_weights(w_ref, wcopy_ref if emit_w else None, transposed=True)
    acc = jnp.dot(a_ref[...], w, preferred_element_type=F32)
    o16_ref = o_refs[-1]
    tm, tn = acc.shape
    hpt = tn // dh
    g = g_ref[...]
    ys = []
    for hh in range(hpt):
        y = acc[:, hh * dh:(hh + 1) * dh]
        if do_rms:
            ms = jnp.mean(y * y, axis=-1, keepdims=True)
            y = y * lax.rsqrt(ms + EPS) * g
        o16_ref[:, hh * dh:(hh + 1) * dh] = y.astype(o16_ref.dtype)
        ys.append(y)
    if rows32:
        o32_ref = o_refs[0]
        h0 = pl.program_id(1) * hpt

        def store32():
            for hh in range(hpt):
                o32_ref[pl.ds(h0 + hh, rows32, stride=nh), :] = ys[hh][tm - rows32:, :]

        if rows32 == tm:
            store32()
        else:
            pl.when(pl.program_id(0) % tiles_per_seq == tiles_per_seq - 1)(store32)


def _proj(a, w, col0, n, g, *, do_rms, dh, tm, tn, f32_rows=0, seq_len=None):
    m, k = a.shape
    tm = min(tm, m)
    tn = _tile(n, tn)
    nh = n // dh
    j0 = col0 // tn
    emit_w = w.dtype != BF16
    assert col0 % tn == 0 and m % tm == 0 and (not emit_w or m == tm)
    out_specs = [pl.BlockSpec((tm, tn), lambda i, j: (i, j))]
    out_shape = [jax.ShapeDtypeStruct((m, n), BF16)]
    rows32, tps = 0, 1
    if f32_rows:
        tps = seq_len // tm
        assert seq_len % tm == 0
        if f32_rows == seq_len:
            rows32 = tm
            out_specs.insert(0, pl.BlockSpec((tm * nh, dh), lambda i, j: (i, 0)))
            out_shape.insert(0, jax.ShapeDtypeStruct((m * nh, dh), F32))
        else:
            rows32 = f32_rows
            assert rows32 <= tm
            out_specs.insert(0, pl.BlockSpec((rows32 * nh, dh), lambda i, j: (i // tps, 0)))
            out_shape.insert(0, jax.ShapeDtypeStruct((m // seq_len * rows32 * nh, dh), F32))
    if emit_w:
        out_specs.append(pl.BlockSpec((k, tn), lambda i, j: (0, j)))
        out_shape.append(jax.ShapeDtypeStruct((k, n), BF16))
    return pl.pallas_call(
        functools.partial(_proj_kernel, do_rms=do_rms, dh=dh, nh=nh, rows32=rows32, tiles_per_seq=tps,
                          emit_w=emit_w),
        grid=(m // tm, n // tn),
        in_specs=[pl.BlockSpec((tm, k), lambda i, j: (i, 0)),
                  pl.BlockSpec((tn, k), lambda i, j: (j0 + j, 0)) if emit_w else
                  pl.BlockSpec((k, tn), lambda i, j: (0, j0 + j)),
                  pl.BlockSpec((1, dh), lambda i, j: (0, 0))],
        out_specs=out_specs,
        out_shape=out_shape,
        compiler_params=_params("arbitrary", "arbitrary"),
        name="proj",
    )(a, w, g.reshape(1, dh).astype(F32))


def _gate_kernel(a_ref, w_ref, b_ref, o_ref):
    z = _qk(a_ref[...], w_ref[...].astype(BF16)) + b_ref[...]
    o_ref[...] = jnp.minimum(z, 0.0) - jnp.log1p(jnp.exp(-jnp.abs(z)))


def _forget_gate(a, w_t, row0, b_f, tm):
    m, k = a.shape
    nh = b_f.shape[0]
    tm = min(tm, m)
    assert row0 % nh == 0
    return pl.pallas_call(
        _gate_kernel,
        grid=(m // tm,),
        in_specs=[pl.BlockSpec((tm, k), lambda i: (i, 0)),
                  pl.BlockSpec((nh, k), lambda i: (row0 // nh, 0)),
                  pl.BlockSpec((1, nh), lambda i: (0, 0))],
        out_specs=pl.BlockSpec((tm, nh), lambda i: (i, 0)),
        out_shape=jax.ShapeDtypeStruct((m, nh), F32),
        compiler_params=_params("arbitrary"),
        name="forget_gate",
    )(a, w_t, b_f.reshape(1, nh).astype(F32))


def _cumsum_kernel(x_ref, o_ref):
    h, tp = x_ref.shape[1], x_ref.shape[2]
    r = lax.broadcasted_iota(jnp.int32, (CUM_BLK, CUM_BLK), 0)
    c = lax.broadcasted_iota(jnp.int32, (CUM_BLK, CUM_BLK), 1)
    tri = (r <= c).astype(BF16)
    carry = jnp.zeros((h, 1), F32)
    for blk in range(tp // CUM_BLK):
        x = x_ref[0, :, blk * CUM_BLK:(blk + 1) * CUM_BLK]
        hi = x.astype(BF16)
        r1 = x - hi.astype(F32)
        mid = r1.astype(BF16)
        lo = (r1 - mid.astype(F32)).astype(BF16)
        cs = (jnp.dot(hi, tri, preferred_element_type=F32)
              + jnp.dot(mid, tri, preferred_element_type=F32)
              + jnp.dot(lo, tri, preferred_element_type=F32)) + carry
        o_ref[0, :, blk * CUM_BLK:(blk + 1) * CUM_BLK] = cs
        carry = cs[:, CUM_BLK - 1:CUM_BLK]


def _cumsum_t(logf_t):
    b, h, tp = logf_t.shape
    return pl.pallas_call(
        _cumsum_kernel,
        grid=(b,),
        in_specs=[pl.BlockSpec((1, h, tp), lambda i: (i, 0, 0))],
        out_specs=pl.BlockSpec((1, h, tp), lambda i: (i, 0, 0)),
        out_shape=jax.ShapeDtypeStruct((b, h, tp), F32),
        compiler_params=_params("arbitrary"),
        name="logf_cumsum",
    )(logf_t)


def _head_column(cc, h):
    lane = lax.broadcasted_iota(jnp.int32, cc.shape, 1)
    return jnp.sum(jnp.where(lane == h, cc, 0.0), axis=1, keepdims=True)


def _qk(q, k):
    return lax.dot_general(q, k, (((1,), (1,)), ((), ())), preferred_element_type=F32)


def _fox_kernel(q_ref, k_ref, v_ref, cc_ref, o_ref, vt_sc, ka_sc, *, tq, hb, dh, nh):
    hg = pl.program_id(1)
    i = pl.program_id(2)
    nk = vt_sc.shape[1]
    heads = range(hb)
    hcols = [slice(hh * dh, (hh + 1) * dh) for hh in heads]

    @pl.when(i == 0)
    def _():
        ones = jnp.ones((FOX_PAD, tq), BF16)
        r = lax.broadcasted_iota(jnp.int32, (LANES, dh), 0)
        lane = lax.broadcasted_iota(jnp.int32, (LANES, dh), 1)
        sels = [((r % nh == hg * hb + hh) & (r // nh == lane) & (r < 3 * nh)).astype(BF16) for hh in heads]
        for kb in range(nk):
            rows = slice(kb * tq, (kb + 1) * tq)
            ck = cc_ref[0, rows, :] * (-LOG2E)
            hi = ck.astype(BF16).astype(F32)
            mid = (ck - hi).astype(BF16).astype(F32)
            lo = ck - hi - mid
            packed = (hi + pltpu.roll(mid, nh, axis=1) + pltpu.roll(lo, 2 * nh, axis=1)).astype(BF16)
            for hh in heads:
                vt_sc[hh, kb, :dh, :] = v_ref[0, rows, hcols[hh]].T
                vt_sc[hh, kb, dh:, :] = ones
                ka_sc[hh, rows, :dh] = k_ref[0, rows, hcols[hh]]
                ka_sc[hh, rows, dh:] = jnp.dot(packed, sels[hh], preferred_element_type=F32).astype(BF16)

    ones_rows = (lax.broadcasted_iota(jnp.int32, (dh, tq), 0) < 3).astype(BF16)
    qts = [jnp.concatenate([q_ref[0, :, cols].T, ones_rows], axis=0) for cols in hcols]

    def scores(kb):
        off = pl.multiple_of(kb * tq, tq)
        out = []
        for hh in heads:
            out.append(jnp.dot(ka_sc[hh, pl.ds(off, tq), :], qts[hh], preferred_element_type=F32))
        return out

    def step(kb, state, mask=None):
        stats = []
        for s, (m, _) in zip(scores(kb), state):
            if mask is not None:
                s = jnp.where(mask, s, NEG)
            m_new = jnp.maximum(m, jnp.max(s, axis=0, keepdims=True))
            stats.append((m_new, jnp.exp2(m - m_new), jnp.exp2(s - m_new).astype(BF16)))
        return [(m_new, alpha * acc + jnp.dot(vt_sc[hh, kb], p, preferred_element_type=F32))
                for hh, ((m_new, alpha, p), (_, acc)) in enumerate(zip(stats, state))]

    state = [(jnp.full((1, tq), NEG, F32), jnp.zeros((dh + FOX_PAD, tq), F32)) for _ in heads]
    state = lax.fori_loop(0, i, step, state)
    key = lax.broadcasted_iota(jnp.int32, (tq, tq), 0)
    qry = lax.broadcasted_iota(jnp.int32, (tq, tq), 1)
    state = step(i, state, mask=key <= qry)
    for cols, (_, acc) in zip(hcols, state):
        o_ref[0, :, cols] = (acc[:dh] / acc[dh:dh + 1]).T.astype(o_ref.dtype)


def _fox_prompt(q, k, v, cum_col, nh, dh, tq, hb):
    b, t, _ = q.shape
    hb = min(hb, nh)
    assert t % tq == 0 and tq % LANES == 0 and 3 * nh <= LANES and dh == LANES
    cum_col = jnp.pad(cum_col, ((0, 0), (0, 0), (0, LANES - nh)))
    qspec = pl.BlockSpec((1, tq, hb * dh), lambda bb, h, i: (bb, i, h))
    kspec = pl.BlockSpec((1, t, hb * dh), lambda bb, h, i: (bb, 0, h))
    return pl.pallas_call(
        functools.partial(_fox_kernel, tq=tq, hb=hb, dh=dh, nh=nh),
        grid=(b, nh // hb, t // tq),
        in_specs=[qspec, kspec, kspec, pl.BlockSpec((1, t, LANES), lambda bb, h, i: (bb, 0, 0))],
        out_specs=qspec,
        out_shape=jax.ShapeDtypeStruct((b, t, nh * dh), BF16),
        scratch_shapes=[pltpu.VMEM((hb, t // tq, dh + FOX_PAD, tq), BF16), pltpu.VMEM((hb, t, 2 * dh), BF16)],
        compiler_params=_params("arbitrary", "arbitrary", "arbitrary"),
        name="fox_prompt",
    )(q, k, v, cum_col)


def _head_major(ref, n, nh):
    return pltpu.einshape("tjd->jtd", ref[0].astype(BF16).reshape(n, nh, ref.shape[2]))


def _fox_s_kernel(q_ref, kc_ref, vc_ref, kn_ref, vn_ref, cc_ref, crc_ref, crn_ref, o_ref, m_sc, l_sc, acc_sc,
                  *, nh, dh, tc, s_len):
    c = pl.program_id(1)
    cc = cc_ref[0][:s_len]
    r = lax.broadcasted_iota(jnp.int32, (s_len, s_len), 0)
    col = lax.broadcasted_iota(jnp.int32, (s_len, s_len), 1)

    @pl.when(c == 0)
    def _():
        hcols = [slice(h * dh, (h + 1) * dh) for h in range(nh)]
        ss = [_qk(q_ref[0, :, cols], kn_ref[0, :, cols]) + (cc[:, h:h + 1] - crn_ref[0, h:h + 1, :s_len])
              for h, cols in enumerate(hcols)]
        ps = []
        for h, s in enumerate(ss):
            s = jnp.where(col <= r, s, NEG)
            m = jnp.max(s, axis=1, keepdims=True)
            p = jnp.exp(s - m)
            m_sc[h] = m
            l_sc[h] = jnp.sum(p, axis=1, keepdims=True)
            ps.append(p.astype(BF16))
        for h, p in enumerate(ps):
            acc_sc[h] = jnp.dot(p, vn_ref[0, :, hcols[h]], preferred_element_type=F32)

    kc = _head_major(kc_ref, tc, nh)
    vc = _head_major(vc_ref, tc, nh)
    ss = [_qk(q_ref[0, :, h * dh:(h + 1) * dh], kc[h])
          + (cc[:, h:h + 1] - crc_ref[0, h:h + 1, :]) for h in range(nh)]
    ps = []
    for h, s in enumerate(ss):
        m_old = m_sc[h]
        m_new = jnp.maximum(m_old, jnp.max(s, axis=1, keepdims=True))
        alpha = jnp.exp(m_old - m_new)
        p = jnp.exp(s - m_new)
        m_sc[h] = m_new
        l_sc[h] = alpha * l_sc[h] + jnp.sum(p, axis=1, keepdims=True)
        ps.append((alpha, p.astype(BF16)))
    for h, (alpha, p) in enumerate(ps):
        acc_sc[h] = alpha * acc_sc[h] + jnp.dot(p, vc[h], preferred_element_type=F32)

    @pl.when(c == pl.num_programs(1) - 1)
    def _():
        for h in range(nh):
            o_ref[0, :, h * dh:(h + 1) * dh] = (acc_sc[h] / l_sc[h]).astype(o_ref.dtype)


def _fox_sample(q, kc, vc, kn, vn, cum_col, cum_t, nh, dh, tc):
    b, s_len, _ = q.shape
    past = kc.shape[1] // nh
    tp = cum_col.shape[1]
    tc = min(tc, past)
    assert past % CUM_BLK == 0 and tp - past == CUM_BLK and past % tc == 0 and s_len <= CUM_BLK
    new = pl.BlockSpec((1, s_len, nh * dh), lambda bb, c: (bb, 0, 0))
    cache = pl.BlockSpec((1, tc * nh, dh), lambda bb, c: (bb, c, 0))
    return pl.pallas_call(
        functools.partial(_fox_s_kernel, nh=nh, dh=dh, tc=tc, s_len=s_len),
        grid=(b, past // tc),
        in_specs=[new, cache, cache, new, new,
                  pl.BlockSpec((1, CUM_BLK, nh), lambda bb, c: (bb, past // CUM_BLK, 0)),
                  pl.BlockSpec((1, nh, tc), lambda bb, c: (bb, 0, c)),
                  pl.BlockSpec((1, nh, CUM_BLK), lambda bb, c: (bb, 0, past // CUM_BLK))],
        out_specs=new,
        out_shape=jax.ShapeDtypeStruct((b, s_len, nh * dh), BF16),
        scratch_shapes=[pltpu.VMEM((nh, s_len, 1), F32), pltpu.VMEM((nh, s_len, 1), F32),
                        pltpu.VMEM((nh, s_len, dh), F32)],
        compiler_params=_params("arbitrary", "arbitrary"),
        name="fox_sample",
    )(q, kc, vc, kn, vn, cum_col, cum_t, cum_t)


def _band_bias_kernel(gf_ref, o_ref):
    f8 = jnp.broadcast_to(gf_ref[0], (8, BAND_ROLL))
    sub = lax.broadcasted_iota(jnp.int32, (8, BAND_ROLL), 0)
    base = f8
    for bb in range(1, 8):
        base = jnp.where(sub == bb, pltpu.roll(f8, bb, axis=1), base)
    for a in range(o_ref.shape[1] // 8):
        rows = base if a == 0 else pltpu.roll(base, 8 * a, axis=1)
        o_ref[0, 8 * a:8 * a + 8, :] = rows[:, :BAND_WIN]


def _band_bias(rel_table, rows):
    nh = rel_table.shape[0]
    assert rows % 8 == 0 and rows <= CHUNK
    u = jnp.arange(BAND_ROLL)
    v = jnp.where(u < BAND_ROLL - BAND_TQ, u, u - BAND_ROLL)
    idx = jnp.clip(LEFT - v, -MAX_REL, MAX_REL) + MAX_REL
    gf = jnp.take(rel_table.astype(F32), idx, axis=1).reshape(nh, 1, BAND_ROLL)
    return pl.pallas_call(
        _band_bias_kernel,
        grid=(nh,),
        in_specs=[pl.BlockSpec((1, 1, BAND_ROLL), lambda h: (h, 0, 0))],
        out_specs=pl.BlockSpec((1, rows, BAND_WIN), lambda h: (h, 0, 0)),
        out_shape=jax.ShapeDtypeStruct((nh, rows, BAND_WIN), F32),
        compiler_params=_params("arbitrary"),
        name="band_bias",
    )(gf)


def _band_bias_t_kernel(gf_ref, o_ref):
    f8 = jnp.broadcast_to(gf_ref[0], (8, BAND_ROLL))
    sub = lax.broadcasted_iota(jnp.int32, (8, BAND_ROLL), 0)
    base = f8
    for bb in range(1, 8):
        base = jnp.where(sub == bb, pltpu.roll(f8, bb, axis=1), base)
    key8 = lax.broadcasted_iota(jnp.int32, (8, BAND_TQ), 0)
    qry_chunk = lax.broadcasted_iota(jnp.int32, (8, BAND_TQ), 1) // CHUNK
    for a in range(BAND_EXT // 8):
        rows = base if a == 0 else pltpu.roll(base, 8 * a, axis=1)
        cd = qry_chunk - ((key8 + 8 * a) // CHUNK - LEFT_CHUNKS)
        rows = jnp.where((cd >= 0) & (cd <= LEFT_CHUNKS), rows[:, :BAND_TQ], NEG)
        for w in range(3):
            r0 = 8 * a - (LEFT - w * BAND_TQ)
            if 0 <= r0 < BAND_WIN:
                o_ref[0, w, r0:r0 + 8, :] = rows


def _band_bias_t(rel_table):
    nh = rel_table.shape[0]
    u = jnp.arange(BAND_ROLL)
    v = jnp.where(u < 2 * BAND_TQ, u, u - BAND_ROLL)
    idx = jnp.clip(v + LEFT, -MAX_REL, MAX_REL) + MAX_REL
    gf = (jnp.take(rel_table.astype(F32), idx, axis=1) * LOG2E).reshape(nh, 1, BAND_ROLL)
    return pl.pallas_call(
        _band_bias_t_kernel,
        grid=(nh,),
        in_specs=[pl.BlockSpec((1, 1, BAND_ROLL), lambda h: (h, 0, 0))],
        out_specs=pl.BlockSpec((1, 3, BAND_WIN, BAND_TQ), lambda h: (h, 0, 0, 0)),
        out_shape=jax.ShapeDtypeStruct((nh, 3, BAND_WIN, BAND_TQ), F32),
        compiler_params=_params("arbitrary"),
        name="band_bias_t",
    )(gf)


def _softmax_pv(score_parts, value_parts):
    probs = []
    for parts in score_parts:
        m = functools.reduce(jnp.maximum, [jnp.max(s, axis=1, keepdims=True) for s in parts])
        ps = [jnp.exp(s - m) for s in parts]
        l = functools.reduce(jnp.add, [jnp.sum(p, axis=1, keepdims=True) for p in ps])
        probs.append(([p.astype(BF16) for p in ps], l))
    outs = []
    for (ps, l), vals in zip(probs, value_parts):
        o = functools.reduce(jnp.add, [jnp.dot(p, v(), preferred_element_type=F32) for p, v in zip(ps, vals)])
        outs.append(o / l)
    return outs


def _band_kernel(q_ref, k_ref, v_ref, e_ref, o_ref, vt_sc, *, hb, dh):
    g = pl.program_id(2)
    nk = vt_sc.shape[1]
    nwin = BAND_WIN // BAND_TQ
    hcols = [slice(hh * dh, (hh + 1) * dh) for hh in range(hb)]

    @pl.when(g == 0)
    def _():
        ones = jnp.ones((FOX_PAD, BAND_TQ), BF16)
        for kb in range(nk):
            for hh, cols in enumerate(hcols):
                vt_sc[hh, kb, :dh, :] = v_ref[0, kb * BAND_TQ:(kb + 1) * BAND_TQ, cols].T
                vt_sc[hh, kb, dh:, :] = ones

    wb = jnp.maximum(g - LEFT // BAND_TQ, 0)
    ws = pl.multiple_of(wb * BAND_TQ, BAND_TQ)
    w = jnp.minimum(g, LEFT // BAND_TQ)
    ss = [jnp.dot(k_ref[0, pl.ds(ws, BAND_WIN), cols], q_ref[0, :, cols].T, preferred_element_type=F32)
          + e_ref[hh, w] for hh, cols in enumerate(hcols)]
    ps = [jnp.exp2(s - jnp.max(s, axis=0, keepdims=True)).astype(BF16) for s in ss]
    for hh, (cols, p) in enumerate(zip(hcols, ps)):
        acc = functools.reduce(jnp.add, [
            jnp.dot(vt_sc[hh, wb + j], p[j * BAND_TQ:(j + 1) * BAND_TQ], preferred_element_type=F32)
            for j in range(nwin)])
        o_ref[0, :, cols] = (acc[:dh] / acc[dh:dh + 1]).T.astype(o_ref.dtype)


def _band_prompt(q, k, v, bias_t, nh, dh, hb):
    b, t, _ = q.shape
    assert t % BAND_TQ == 0 and t >= BAND_WIN
    hb = min(hb, nh)
    qspec = pl.BlockSpec((1, BAND_TQ, hb * dh), lambda h, bb, g: (bb, g, h))
    kspec = pl.BlockSpec((1, t, hb * dh), lambda h, bb, g: (bb, 0, h))
    return pl.pallas_call(
        functools.partial(_band_kernel, hb=hb, dh=dh),
        grid=(nh // hb, b, t // BAND_TQ),
        in_specs=[qspec, kspec, kspec,
                  pl.BlockSpec((hb, 3, BAND_WIN, BAND_TQ), lambda h, bb, g: (h, 0, 0, 0),
                               pipeline_mode=pl.Buffered(1))],
        out_specs=qspec,
        out_shape=jax.ShapeDtypeStruct((b, t, nh * dh), BF16),
        scratch_shapes=[pltpu.VMEM((hb, t // BAND_TQ, dh + FOX_PAD, BAND_TQ), BF16)],
        compiler_params=_params("arbitrary", "arbitrary", "arbitrary"),
        name="band_prompt",
    )(q, k, v, bias_t)


def _band_s_kernel(q_ref, kc_ref, vc_ref, kn_ref, vn_ref, kn32_ref, vn32_ref, e_ref, o_ref, nk_ref, nv_ref,
                   *, nh, dh, keep, s_len):
    for new_ref, cache_ref, n32_ref in ((nk_ref, kc_ref, kn32_ref), (nv_ref, vc_ref, vn32_ref)):
        new_ref[0, :(keep - s_len) * nh, :] = cache_ref[0, s_len * nh:, :]
        new_ref[0, (keep - s_len) * nh:, :] = n32_ref[0]
    hcols = [slice(h * dh, (h + 1) * dh) for h in range(nh)]
    scores, values = [], []
    kc = _head_major(kc_ref, keep, nh)
    vc = _head_major(vc_ref, keep, nh)
    for h, cols in enumerate(hcols):
        q = q_ref[0, :, cols]
        e = e_ref[h]
        scores.append([_qk(q, kc[h]) + e[:, :keep],
                       _qk(q, kn_ref[0, :, cols]) + e[:, keep:keep + s_len]])
        values.append([functools.partial(lambda h: vc[h], h),
                       functools.partial(lambda cols: vn_ref[0, :, cols], cols)])
    for cols, o in zip(hcols, _softmax_pv(scores, values)):
        o_ref[0, :, cols] = o.astype(o_ref.dtype)


def _band_sample(q, kc, vc, kn, vn, kn32, vn32, bias, nh, dh):
    b, s_len, _ = q.shape
    keep = kc.shape[1] // nh
    assert keep == LEFT and s_len <= CHUNK and s_len % 8 == 0
    new = pl.BlockSpec((1, s_len, nh * dh), lambda bb: (bb, 0, 0))
    new32 = pl.BlockSpec((1, s_len * nh, dh), lambda bb: (bb, 0, 0))
    cache = pl.BlockSpec((1, keep * nh, dh), lambda bb: (bb, 0, 0))
    return pl.pallas_call(
        functools.partial(_band_s_kernel, nh=nh, dh=dh, keep=keep, s_len=s_len),
        grid=(b,),
        in_specs=[new, cache, cache, new, new, new32, new32,
                  pl.BlockSpec((nh, s_len, BAND_WIN), lambda bb: (0, 0, 0))],
        out_specs=[new, cache, cache],
        out_shape=[jax.ShapeDtypeStruct((b, s_len, nh * dh), BF16)] + [jax.ShapeDtypeStruct(kc.shape, F32)] * 2,
        compiler_params=_params("arbitrary"),
        name="band_sample",
    )(q, kc, vc, kn, vn, kn32, vn32, bias)


def _oproj_kernel(oa_ref, ob_ref, wa_ref, wb_ref, x_ref, gate_ref, o_ref, *wcopy_refs):
    ca, cb = wcopy_refs if wcopy_refs else (None, None)
    acc = (jnp.dot(oa_ref[...], _bf16_weights(wa_ref, ca), preferred_element_type=F32)
           + jnp.dot(ob_ref[...], _bf16_weights(wb_ref, cb), preferred_element_type=F32))
    o_ref[...] = x_ref[...] + gate_ref[...] * acc


def _gate_spec(gate, m, tm, tn, order=lambda fn: fn):
    if gate.ndim == 2:
        return pl.BlockSpec((tm, tn), order(lambda i, j: (i, j)))
    tiles_per_seq = m // gate.shape[0] // tm
    return pl.BlockSpec((None, 1, tn), order(lambda i, j: (i // tiles_per_seq, 0, j)))


def _oproj(oa, ob, wa, wb, x, gate, tm, tn):
    m, da = oa.shape
    db = ob.shape[1]
    (wa, ra), (wb, rb) = wa, wb
    d = wa.shape[1]
    tm = min(tm, m)
    tn = _tile(d, tn)
    emit_w = wa.dtype != BF16
    assert da == db and (not emit_w or m == tm)
    out_specs = [pl.BlockSpec((tm, tn), lambda i, j: (i, j))]
    out_shape = [jax.ShapeDtypeStruct((m, d), F32)]
    if emit_w:
        out_specs += [pl.BlockSpec((da, tn), lambda i, j: (0, j))] * 2
        out_shape += [jax.ShapeDtypeStruct((da, d), BF16)] * 2
    return pl.pallas_call(
        _oproj_kernel,
        grid=(m // tm, d // tn),
        in_specs=[pl.BlockSpec((tm, da), lambda i, j: (i, 0)),
                  pl.BlockSpec((tm, db), lambda i, j: (i, 0)),
                  pl.BlockSpec((da, tn), lambda i, j: (ra, j)),
                  pl.BlockSpec((db, tn), lambda i, j: (rb, j)),
                  pl.BlockSpec((tm, tn), lambda i, j: (i, j)),
                  _gate_spec(gate, m, tm, tn)],
        out_specs=out_specs,
        out_shape=out_shape,
        compiler_params=_params("arbitrary", "arbitrary"),
        name="out_proj",
    )(oa, ob, wa, wb, x, gate)


def _ffn_up_kernel(a_ref, wg_ref, wu_ref, o_ref, *wcopy_refs):
    cg, cu = wcopy_refs if wcopy_refs else (None, None)
    a = a_ref[...]
    g = jnp.dot(a, _bf16_weights(wg_ref, cg), preferred_element_type=F32)
    u = jnp.dot(a, _bf16_weights(wu_ref, cu), preferred_element_type=F32)
    o_ref[...] = (g * jax.nn.sigmoid(g) * u).astype(o_ref.dtype)


def _ffn_up(a, wg, wu, f, tm, tn):
    m, k = a.shape
    (wg, cg), (wu, cu) = wg, wu
    tm = min(tm, m)
    tn = _tile(f, tn)
    jg, ju = cg // tn, cu // tn
    emit_w = wg.dtype != BF16
    assert cg % tn == 0 and cu % tn == 0 and (not emit_w or m == tm)
    out_specs = [pl.BlockSpec((tm, tn), lambda i, j: (i, j))]
    out_shape = [jax.ShapeDtypeStruct((m, f), BF16)]
    if emit_w:
        out_specs += [pl.BlockSpec((k, tn), lambda i, j: (0, j))] * 2
        out_shape += [jax.ShapeDtypeStruct((k, f), BF16)] * 2
    return pl.pallas_call(
        _ffn_up_kernel,
        grid=(m // tm, f // tn),
        in_specs=[pl.BlockSpec((tm, k), lambda i, j: (i, 0)),
                  pl.BlockSpec((k, tn), lambda i, j: (0, jg + j)),
                  pl.BlockSpec((k, tn), lambda i, j: (0, ju + j))],
        out_specs=out_specs,
        out_shape=out_shape,
        compiler_params=_params("arbitrary", "arbitrary"),
        name="ffn_up",
    )(a, wg, wu)


def _ffn_down_kernel(a_ref, w_ref, x_ref, gate_ref, o_ref, *wcopy_refs):
    w = _bf16_weights(w_ref, wcopy_refs[0] if wcopy_refs else None)
    acc = jnp.dot(a_ref[...], w, preferred_element_type=F32)
    o_ref[...] = x_ref[...] + gate_ref[...] * acc


def _ffn_down(a, w_down, x, gate, tm, tn):
    m, f = a.shape
    d = w_down.shape[1]
    tm = min(tm, m)
    tn = _tile(d, tn)
    emit_w = w_down.dtype != BF16
    assert not emit_w or m == tm
    order = lambda fn: (lambda j, i: fn(i, j))
    w_mode = dict(pipeline_mode=pl.Buffered(1)) if m > tm else {}
    a_mode = dict(pipeline_mode=pl.Buffered(1)) if m == tm else {}
    out_specs = [pl.BlockSpec((tm, tn), order(lambda i, j: (i, j)))]
    out_shape = [jax.ShapeDtypeStruct((m, d), F32)]
    if emit_w:
        out_specs.append(pl.BlockSpec((f, tn), order(lambda i, j: (0, j))))
        out_shape.append(jax.ShapeDtypeStruct((f, d), BF16))
    return pl.pallas_call(
        _ffn_down_kernel,
        grid=(d // tn, m // tm),
        in_specs=[pl.BlockSpec((tm, f), order(lambda i, j: (i, 0)), **a_mode),
                  pl.BlockSpec((f, tn), order(lambda i, j: (0, j)), **w_mode),
                  pl.BlockSpec((tm, tn), order(lambda i, j: (i, j))),
                  _gate_spec(gate, m, tm, tn, order)],
        out_specs=out_specs,
        out_shape=out_shape,
        compiler_params=_params("arbitrary", "arbitrary"),
        name="ffn_down",
    )(a, w_down, x, gate)


def _layer(x, mod, cache, w, bias, *, nb, tr):
    b, t, d = x.shape
    m = b * t
    shift_a, scale_a, gate_a, shift_f, scale_f, gate_f = mod
    dh = w["g_q_a"].shape[0]
    nh_a = w["b_f"].shape[0]
    d_a = d_b = w["d_a"]
    nh_b = d_b // dh
    f = w["f"]
    scale = dh ** -0.5
    prompt = cache is None
    tm = 1024
    w16 = {}

    def gates(gt):
        if prompt:
            return gt.reshape(b, 1, d)
        return jnp.broadcast_to(gt[:, None, :], (b, t, d)).reshape(m, d)

    h = _modnorm(x, scale_a, shift_a, w["g_attn"], nb, tr).reshape(m, d)
    keep_p = min(LEFT, t)
    full32 = dict(tm=tm, tn=512, f32_rows=t, seq_len=t) if prompt else dict(tm=tm, tn=512, f32_rows=m, seq_len=m)
    tail32 = dict(tm=tm, tn=1024, f32_rows=keep_p, seq_len=t) if prompt else full32

    def pj(name, n, g, **kw):
        arr, col0 = w[name]
        kw.setdefault("tn", 1024 if prompt else 512)
        outs = _proj(h, arr, col0, n, g, dh=dh, **kw)
        if not prompt:
            *outs, wcopy = outs
            w16[name] = (wcopy, 0)
        return outs

    (q_a,) = pj("w_q_a", d_a, w["g_q_a"] * (scale * LOG2E if prompt else scale), do_rms=True, tm=tm)
    k_a32, k_a = pj("w_k_a", d_a, w["g_k_a"], do_rms=True, **full32)
    v_a32, v_a = pj("w_v_a", d_a, w["g_k_a"], do_rms=False, **full32)
    (q_b,) = pj("w_q_b", d_b, w["g_q_b"] * (scale * LOG2E if prompt else scale), do_rms=True, tm=tm)
    k_b32, k_b = pj("w_k_b", d_b, w["g_k_b"], do_rms=True, **tail32)
    v_b32, v_b = pj("w_v_b", d_b, w["g_k_b"], do_rms=False, **tail32)
    logf = _forget_gate(h, *w["w_f"], w["b_f"], tm).reshape(b, t, nh_a)

    r3 = lambda z: z.reshape(b, t, -1)
    if prompt:
        cum_t = _cumsum_t(logf.transpose(0, 2, 1))
        tq = 256
        o_a = _fox_prompt(r3(q_a), r3(k_a), r3(v_a), cum_t.transpose(0, 2, 1), nh_a, dh, tq, 8)
        o_b = _band_prompt(r3(q_b), r3(k_b), r3(v_b), bias[1], nh_b, dh, 8)
        new_k_b, new_v_b = k_b32.reshape(b, keep_p, nh_b, dh), v_b32.reshape(b, keep_p, nh_b, dh)
    else:
        ck_a, cv_a, clogf_a, ck_b, cv_b = cache
        past = ck_a.shape[1]
        tp = past + CUM_BLK
        lf_all = jnp.concatenate([clogf_a.astype(F32), logf, jnp.zeros((b, tp - past - t, nh_a), F32)], axis=1)
        cum_t = _cumsum_t(lf_all.transpose(0, 2, 1))
        o_a = _fox_sample(r3(q_a), ck_a.reshape(b, past * nh_a, dh), cv_a.reshape(b, past * nh_a, dh),
                          r3(k_a), r3(v_a), cum_t.transpose(0, 2, 1), cum_t, nh_a, dh, 1024)
        keep = ck_b.shape[1]
        o_b, new_k_b, new_v_b = _band_sample(
            r3(q_b), ck_b.reshape(b, keep * nh_b, dh).astype(F32), cv_b.reshape(b, keep * nh_b, dh).astype(F32),
            r3(k_b), r3(v_b), k_b32.reshape(b, t * nh_b, dh), v_b32.reshape(b, t * nh_b, dh), bias[0], nh_b, dh)
        new_k_b = new_k_b.reshape(b, keep, nh_b, dh).astype(ck_b.dtype)
        new_v_b = new_v_b.reshape(b, keep, nh_b, dh).astype(cv_b.dtype)

    x1, *wo16 = _oproj(o_a.reshape(m, d_a), o_b.reshape(m, d_b), w["w_oa"], w["w_ob"], x.reshape(m, d),
                       gates(gate_a), tm, 1024 if prompt else 512)
    h2 = _modnorm(x1.reshape(b, t, d), scale_f, shift_f, w["g_ffn"], nb, tr).reshape(m, d)
    act, *wgu16 = _ffn_up(h2, w["w_g"], w["w_u"], f, 2 * tm, 256)
    y, *wd16 = _ffn_down(act, w["w_down"], x1, gates(gate_f), 512, 1024 if prompt else 256)
    if not prompt:
        w16.update(w_oa=(wo16[0], 0), w_ob=(wo16[1], 0), w_g=(wgu16[0], 0), w_u=(wgu16[1], 0), w_down=wd16[0])
    return (y.reshape(b, t, d), k_a32.reshape(b, t, nh_a, dh), v_a32.reshape(b, t, nh_a, dh), logf,
            new_k_b, new_v_b, w16)


def kernel(x_prompt, x_sample, cache_k_a, cache_v_a, cache_logf_a, cache_k_b, cache_v_b, c_prompt, c_sample,
           w_ada, b_ada, g_attn, g_ffn, w_in, b_f, g_q_a, g_k_a, g_q_b, g_k_b, rel_table, w_o, w_gu, w_down):
    depth = w_ada.shape[0]
    d = x_prompt.shape[-1]
    nb_p, nb_s = c_prompt.shape[0], c_sample.shape[0]
    nh_a = b_f.shape[1]
    n_qkv = w_in.shape[2] - nh_a
    d_a = n_qkv // 6
    f = w_down.shape[1]
    assert 2 * d_a == w_o.shape[1] and w_gu.shape[2] == 2 * f
    y_p, y_s = x_prompt, x_sample
    outs_p, outs_s = [], []
    for l in range(depth):
        w_in_t = jnp.swapaxes(w_in[l], 0, 1)
        small = dict(w_f=(w_in_t, n_qkv), b_f=b_f[l], g_attn=g_attn[l], g_ffn=g_ffn[l], g_q_a=g_q_a[l], g_k_a=g_k_a[l],
                     g_q_b=g_q_b[l], g_k_b=g_k_b[l], d_a=d_a, f=f)
        names = ("w_q_a", "w_k_a", "w_v_a", "w_q_b", "w_k_b", "w_v_b")
        w32 = dict(small, w_oa=(w_o[l], 0), w_ob=(w_o[l], 1), w_g=(w_gu[l], 0), w_u=(w_gu[l], f), w_down=w_down[l],
                   **{name: (w_in_t, i * d_a) for i, name in enumerate(names)})
        c_all = jnp.concatenate([c_prompt, c_sample], axis=0)
        pad = (-c_all.shape[0]) % 16
        mod = _ada(jnp.pad(c_all, ((0, pad), (0, 0))), w_ada[l], b_ada[l])
        mod_p = [mod[:nb_p, i * d:(i + 1) * d] for i in range(6)]
        mod_s = [mod[nb_p:nb_p + nb_s, i * d:(i + 1) * d] for i in range(6)]
        bias = (_band_bias(rel_table[l], x_sample.shape[1]), _band_bias_t(rel_table[l]))
        cache = (cache_k_a[l], cache_v_a[l], cache_logf_a[l], cache_k_b[l], cache_v_b[l])
        y_s, *rest_s, w16 = _layer(y_s, mod_s, cache, w32, bias, nb=nb_s, tr=x_sample.shape[1])
        y_p, *rest_p, _ = _layer(y_p, mod_p, None, dict(small, **w16), bias, nb=1, tr=min(512, x_prompt.shape[1]))
        outs_p.append(rest_p)
        outs_s.append(rest_s)
    stack = lambda outs, i: jnp.stack([o[i] for o in outs])
    return (y_p, y_s, *[stack(outs_p, i) for i in range(5)], *[stack(outs_s, i) for i in range(5)])
```

```python
import functools

import jax
import jax.numpy as jnp
from jax import lax
from jax.experimental import pallas as pl
from jax.experimental.pallas import tpu as pltpu

CHUNK = 64
LEFT_CHUNKS = 8
LEFT = LEFT_CHUNKS * CHUNK
MAX_REL = 128
EPS = 1e-6
NEG = -1e30
LOG2E = 1.4426950408889634

LANES = 128
BAND_TQ = 256
BAND_WIN = LEFT + BAND_TQ
BAND_EXT = BAND_WIN + LEFT
BAND_ROLL = 2048
CUM_BLK = 256
FOX_PAD = 16
VMEM_LIMIT_BYTES = 56 * 1024 * 1024

F32 = jnp.float32
BF16 = jnp.bfloat16


def _params(*semantics):
    return pltpu.CompilerParams(dimension_semantics=semantics, vmem_limit_bytes=VMEM_LIMIT_BYTES)


def _tile(n, want):
    if n <= want:
        return n
    t = (want // LANES) * LANES
    while t >= LANES:
        if n % t == 0:
            return t
        t -= LANES
    raise ValueError(f"no lane-aligned tile for {n}")


def _ada_kernel(c_ref, w_ref, b_ref, o_ref):
    c = c_ref[...]
    a = (c * jax.nn.sigmoid(c)).astype(BF16)
    o_ref[...] = jnp.dot(a, w_ref[...].astype(BF16), preferred_element_type=F32) + b_ref[...]


def _ada(c, w_ada, b_ada):
    m, d = c.shape
    n = w_ada.shape[1]
    tn = _tile(n, 512)
    return pl.pallas_call(
        _ada_kernel,
        grid=(n // tn,),
        in_specs=[pl.BlockSpec((m, d), lambda j: (0, 0)),
                  pl.BlockSpec((d, tn), lambda j: (0, j)),
                  pl.BlockSpec((1, tn), lambda j: (0, j))],
        out_specs=pl.BlockSpec((m, tn), lambda j: (0, j)),
        out_shape=jax.ShapeDtypeStruct((m, n), F32),
        compiler_params=_params("arbitrary"),
        name="ada_mod",
    )(c, w_ada, b_ada.reshape(1, n))


def _modnorm_kernel(x_ref, sc_ref, sh_ref, g_ref, o_ref):
    x = x_ref[...]
    ms = jnp.mean(x * x, axis=-1, keepdims=True)
    y = x * lax.rsqrt(ms + EPS) * g_ref[...]
    o_ref[...] = (y * (1.0 + sc_ref[...]) + sh_ref[...]).astype(o_ref.dtype)


def _modnorm(x, scale, shift, g, nb, tr):
    b, t, d = x.shape
    bs = pl.BlockSpec((nb, 1, d), lambda i, j: (i, 0, 0))
    return pl.pallas_call(
        _modnorm_kernel,
        grid=(b // nb, t // tr),
        in_specs=[pl.BlockSpec((nb, tr, d), lambda i, j: (i, j, 0)), bs, bs,
                  pl.BlockSpec((1, 1, d), lambda i, j: (0, 0, 0))],
        out_specs=pl.BlockSpec((nb, tr, d), lambda i, j: (i, j, 0)),
        out_shape=jax.ShapeDtypeStruct((b, t, d), BF16),
        compiler_params=_params("arbitrary", "arbitrary"),
        name="modnorm",
    )(x, scale.reshape(b, 1, d), shift.reshape(b, 1, d), g.reshape(1, 1, d))


def _bf16_weights(w_ref, copy_ref, transposed=False):
    if copy_ref is None:
        return w_ref[...]
    w = w_ref[...]
    w = (w.T if transposed else w).astype(BF16)
    copy_ref[...] = w
    return w


def _proj_kernel(a_ref, w_ref, g_ref, *o_refs, do_rms, dh, nh, rows32, tiles_per_seq, emit_w):
    if emit_w:
        *o_refs, wcopy_ref = o_refs
    w = _bf16_weights(w_ref, wcopy_ref if emit_w else None, transposed=True)
    acc = jnp.dot(a_ref[...], w, preferred_element_type=F32)
    o16_ref = o_refs[-1]
    tm, tn = acc.shape
    hpt = tn // dh
    g = g_ref[...]
    ys = []
    for hh in range(hpt):
        y = acc[:, hh * dh:(hh + 1) * dh]
        if do_rms:
            ms = jnp.mean(y * y, axis=-1, keepdims=True)
            y = y * lax.rsqrt(ms + EPS) * g
        o16_ref[:, hh * dh:(hh + 1) * dh] = y.astype(o16_ref.dtype)
        ys.append(y)
    if rows32:
        o32_ref = o_refs[0]
        h0 = pl.program_id(1) * hpt

        def store32():
            for hh in range(hpt):
                o32_ref[pl.ds(h0 + hh, rows32, stride=nh), :] = ys[hh][tm - rows32:, :]

        if rows32 == tm:
            store32()
        else:
            pl.when(pl.program_id(0) % tiles_per_seq == tiles_per_seq - 1)(store32)


def _proj(a, w, col0, n, g, *, do_rms, dh, tm, tn, f32_rows=0, seq_len=None):
    m, k = a.shape
    tm = min(tm, m)
    tn = _tile(n, tn)
    nh = n // dh
    j0 = col0 // tn
    emit_w = w.dtype != BF16
    assert col0 % tn == 0 and m % tm == 0 and (not emit_w or m == tm)
    out_specs = [pl.BlockSpec((tm, tn), lambda i, j: (i, j))]
    out_shape = [jax.ShapeDtypeStruct((m, n), BF16)]
    rows32, tps = 0, 1
    if f32_rows:
        tps = seq_len // tm
        assert seq_len % tm == 0
        if f32_rows == seq_len:
            rows32 = tm
            out_specs.insert(0, pl.BlockSpec((tm * nh, dh), lambda i, j: (i, 0)))
            out_shape.insert(0, jax.ShapeDtypeStruct((m * nh, dh), F32))
        else:
            rows32 = f32_rows
            assert rows32 <= tm
            out_specs.insert(0, pl.BlockSpec((rows32 * nh, dh), lambda i, j: (i // tps, 0)))
            out_shape.insert(0, jax.ShapeDtypeStruct((m // seq_len * rows32 * nh, dh), F32))
    if emit_w:
        out_specs.append(pl.BlockSpec((k, tn), lambda i, j: (0, j)))
        out_shape.append(jax.ShapeDtypeStruct((k, n), BF16))
    return pl.pallas_call(
        functools.partial(_proj_kernel, do_rms=do_rms, dh=dh, nh=nh, rows32=rows32, tiles_per_seq=tps,
                          emit_w=emit_w),
        grid=(m // tm, n // tn),
        in_specs=[pl.BlockSpec((tm, k), lambda i, j: (i, 0)),
                  pl.BlockSpec((tn, k), lambda i, j: (j0 + j, 0)) if emit_w else
                  pl.BlockSpec((k, tn), lambda i, j: (0, j0 + j)),
                  pl.BlockSpec((1, dh), lambda i, j: (0, 0))],
        out_specs=out_specs,
        out_shape=out_shape,
        compiler_params=_params("arbitrary", "arbitrary"),
        name="proj",
    )(a, w, g.reshape(1, dh).astype(F32))


def _gate_kernel(a_ref, w_ref, b_ref, o_ref):
    z = _qk(a_ref[...], w_ref[...].astype(BF16)) + b_ref[...]
    o_ref[...] = jnp.minimum(z, 0.0) - jnp.log1p(jnp.exp(-jnp.abs(z)))


def _forget_gate(a, w_t, row0, b_f, tm):
    m, k = a.shape
    nh = b_f.shape[0]
    tm = min(tm, m)
    assert row0 % nh == 0
    return pl.pallas_call(
        _gate_kernel,
        grid=(m // tm,),
        in_specs=[pl.BlockSpec((tm, k), lambda i: (i, 0)),
                  pl.BlockSpec((nh, k), lambda i: (row0 // nh, 0)),
                  pl.BlockSpec((1, nh), lambda i: (0, 0))],
        out_specs=pl.BlockSpec((tm, nh), lambda i: (i, 0)),
        out_shape=jax.ShapeDtypeStruct((m, nh), F32),
        compiler_params=_params("arbitrary"),
        name="forget_gate",
    )(a, w_t, b_f.reshape(1, nh).astype(F32))


def _cumsum_kernel(x_ref, o_ref):
    h, tp = x_ref.shape[1], x_ref.shape[2]
    r = lax.broadcasted_iota(jnp.int32, (CUM_BLK, CUM_BLK), 0)
    c = lax.broadcasted_iota(jnp.int32, (CUM_BLK, CUM_BLK), 1)
    tri = (r <= c).astype(BF16)
    carry = jnp.zeros((h, 1), F32)
    for blk in range(tp // CUM_BLK):
        x = x_ref[0, :, blk * CUM_BLK:(blk + 1) * CUM_BLK]
        hi = x.astype(BF16)
        r1 = x - hi.astype(F32)
        mid = r1.astype(BF16)
        lo = (r1 - mid.astype(F32)).astype(BF16)
        cs = (jnp.dot(hi, tri, preferred_element_type=F32)
              + jnp.dot(mid, tri, preferred_element_type=F32)
              + jnp.dot(lo, tri, preferred_element_type=F32)) + carry
        o_ref[0, :, blk * CUM_BLK:(blk + 1) * CUM_BLK] = cs
        carry = cs[:, CUM_BLK - 1:CUM_BLK]


def _cumsum_t(logf_t):
    b, h, tp = logf_t.shape
    return pl.pallas_call(
        _cumsum_kernel,
        grid=(b,),
        in_specs=[pl.BlockSpec((1, h, tp), lambda i: (i, 0, 0))],
        out_specs=pl.BlockSpec((1, h, tp), lambda i: (i, 0, 0)),
        out_shape=jax.ShapeDtypeStruct((b, h, tp), F32),
        compiler_params=_params("arbitrary"),
        name="logf_cumsum",
    )(logf_t)


def _qk(q, k):
    return lax.dot_general(q, k, (((1,), (1,)), ((), ())), preferred_element_type=F32)


def _fox_kernel(q_ref, k_ref, v_ref, cc_ref, o_ref, vt_sc, ka_sc, *, tq, hb, dh, nh):
    hg = pl.program_id(1)
    i = pl.program_id(2)
    nk = vt_sc.shape[1]
    heads = range(hb)
    hcols = [slice(hh * dh, (hh + 1) * dh) for hh in heads]

    @pl.when(i == 0)
    def _():
        ones = jnp.ones((FOX_PAD, tq), BF16)
        r = lax.broadcasted_iota(jnp.int32, (LANES, dh), 0)
        lane = lax.broadcasted_iota(jnp.int32, (LANES, dh), 1)
        sels = [((r % nh == hg * hb + hh) & (r // nh == lane) & (r < 3 * nh)).astype(BF16) for hh in heads]
        for kb in range(nk):
            rows = slice(kb * tq, (kb + 1) * tq)
            ck = cc_ref[0, rows, :] * (-LOG2E)
            hi = ck.astype(BF16).astype(F32)
            mid = (ck - hi).astype(BF16).astype(F32)
            lo = ck - hi - mid
            packed = (hi + pltpu.roll(mid, nh, axis=1) + pltpu.roll(lo, 2 * nh, axis=1)).astype(BF16)
            for hh in heads:
                vt_sc[hh, kb, :dh, :] = v_ref[0, rows, hcols[hh]].T
                vt_sc[hh, kb, dh:, :] = ones
                ka_sc[hh, rows, :dh] = k_ref[0, rows, hcols[hh]]
                ka_sc[hh, rows, dh:] = jnp.dot(packed, sels[hh], preferred_element_type=F32).astype(BF16)

    ones_rows = (lax.broadcasted_iota(jnp.int32, (dh, tq), 0) < 3).astype(BF16)
    qts = [jnp.concatenate([q_ref[0, :, cols].T, ones_rows], axis=0) for cols in hcols]

    def scores(kb):
        off = pl.multiple_of(kb * tq, tq)
        out = []
        for hh in heads:
            out.append(jnp.dot(ka_sc[hh, pl.ds(off, tq), :], qts[hh], preferred_element_type=F32))
        return out

    def step(kb, state, mask=None):
        stats = []
        for s, (m, _) in zip(scores(kb), state):
            if mask is not None:
                s = jnp.where(mask, s, NEG)
            m_new = jnp.maximum(m, jnp.max(s, axis=0, keepdims=True))
            stats.append((m_new, jnp.exp2(m - m_new), jnp.exp2(s - m_new).astype(BF16)))
        return [(m_new, alpha * acc + jnp.dot(vt_sc[hh, kb], p, preferred_element_type=F32))
                for hh, ((m_new, alpha, p), (_, acc)) in enumerate(zip(stats, state))]

    state = [(jnp.full((1, tq), NEG, F32), jnp.zeros((dh + FOX_PAD, tq), F32)) for _ in heads]
    state = lax.fori_loop(0, i, step, state)
    key = lax.broadcasted_iota(jnp.int32, (tq, tq), 0)
    qry = lax.broadcasted_iota(jnp.int32, (tq, tq), 1)
    state = step(i, state, mask=key <= qry)
    for cols, (_, acc) in zip(hcols, state):
        o_ref[0, :, cols] = (acc[:dh] / acc[dh:dh + 1]).T.astype(o_ref.dtype)


def _fox_prompt(q, k, v, cum_col, nh, dh, tq, hb):
    b, t, _ = q.shape
    hb = min(hb, nh)
    assert t % tq == 0 and tq % LANES == 0 and 3 * nh <= LANES and dh == LANES
    cum_col = jnp.pad(cum_col, ((0, 0), (0, 0), (0, LANES - nh)))
    qspec = pl.BlockSpec((1, tq, hb * dh), lambda bb, h, i: (bb, i, h))
    kspec = pl.BlockSpec((1, t, hb * dh), lambda bb, h, i: (bb, 0, h))
    return pl.pallas_call(
        functools.partial(_fox_kernel, tq=tq, hb=hb, dh=dh, nh=nh),
        grid=(b, nh // hb, t // tq),
        in_specs=[qspec, kspec, kspec, pl.BlockSpec((1, t, LANES), lambda bb, h, i: (bb, 0, 0))],
        out_specs=qspec,
        out_shape=jax.ShapeDtypeStruct((b, t, nh * dh), BF16),
        scratch_shapes=[pltpu.VMEM((hb, t // tq, dh + FOX_PAD, tq), BF16), pltpu.VMEM((hb, t, 2 * dh), BF16)],
        compiler_params=_params("arbitrary", "arbitrary", "arbitrary"),
        name="fox_prompt",
    )(q, k, v, cum_col)


def _head_major(ref, n, nh):
    return pltpu.einshape("tjd->jtd", ref[0].astype(BF16).reshape(n, nh, ref.shape[2]))


def _fox_s_kernel(q_ref, kc_ref, vc_ref, kn_ref, vn_ref, cc_ref, crc_ref, crn_ref, o_ref, m_sc, l_sc, acc_sc,
                  *, nh, dh, tc, s_len):
    c = pl.program_id(1)
    cc = cc_ref[0][:s_len]
    r = lax.broadcasted_iota(jnp.int32, (s_len, s_len), 0)
    col = lax.broadcasted_iota(jnp.int32, (s_len, s_len), 1)

    @pl.when(c == 0)
    def _():
        hcols = [slice(h * dh, (h + 1) * dh) for h in range(nh)]
        ss = [_qk(q_ref[0, :, cols], kn_ref[0, :, cols]) + (cc[:, h:h + 1] - crn_ref[0, h:h + 1, :s_len])
              for h, cols in enumerate(hcols)]
        ps = []
        for h, s in enumerate(ss):
            s = jnp.where(col <= r, s, NEG)
            m = jnp.max(s, axis=1, keepdims=True)
            p = jnp.exp(s - m)
            m_sc[h] = m
            l_sc[h] = jnp.sum(p, axis=1, keepdims=True)
            ps.append(p.astype(BF16))
        for h, p in enumerate(ps):
            acc_sc[h] = jnp.dot(p, vn_ref[0, :, hcols[h]], preferred_element_type=F32)

    kc = _head_major(kc_ref, tc, nh)
    vc = _head_major(vc_ref, tc, nh)
    ss = [_qk(q_ref[0, :, h * dh:(h + 1) * dh], kc[h])
          + (cc[:, h:h + 1] - crc_ref[0, h:h + 1, :]) for h in range(nh)]
    ps = []
    for h, s in enumerate(ss):
        m_old = m_sc[h]
        m_new = jnp.maximum(m_old, jnp.max(s, axis=1, keepdims=True))
        alpha = jnp.exp(m_old - m_new)
        p = jnp.exp(s - m_new)
        m_sc[h] = m_new
        l_sc[h] = alpha * l_sc[h] + jnp.sum(p, axis=1, keepdims=True)
        ps.append((alpha, p.astype(BF16)))
    for h, (alpha, p) in enumerate(ps):
        acc_sc[h] = alpha * acc_sc[h] + jnp.dot(p, vc[h], preferred_element_type=F32)

    @pl.when(c == pl.num_programs(1) - 1)
    def _():
        for h in range(nh):
            o_ref[0, :, h * dh:(h + 1) * dh] = (acc_sc[h] / l_sc[h]).astype(o_ref.dtype)


def _fox_sample(q, kc, vc, kn, vn, cum_col, cum_t, nh, dh, tc):
    b, s_len, _ = q.shape
    past = kc.shape[1] // nh
    tp = cum_col.shape[1]
    tc = min(tc, past)
    assert past % CUM_BLK == 0 and tp - past == CUM_BLK and past % tc == 0 and s_len <= CUM_BLK
    new = pl.BlockSpec((1, s_len, nh * dh), lambda bb, c: (bb, 0, 0))
    cache = pl.BlockSpec((1, tc * nh, dh), lambda bb, c: (bb, c, 0))
    return pl.pallas_call(
        functools.partial(_fox_s_kernel, nh=nh, dh=dh, tc=tc, s_len=s_len),
        grid=(b, past // tc),
        in_specs=[new, cache, cache, new, new,
                  pl.BlockSpec((1, CUM_BLK, nh), lambda bb, c: (bb, past // CUM_BLK, 0)),
                  pl.BlockSpec((1, nh, tc), lambda bb, c: (bb, 0, c)),
                  pl.BlockSpec((1, nh, CUM_BLK), lambda bb, c: (bb, 0, past // CUM_BLK))],
        out_specs=new,
        out_shape=jax.ShapeDtypeStruct((b, s_len, nh * dh), BF16),
        scratch_shapes=[pltpu.VMEM((nh, s_len, 1), F32), pltpu.VMEM((nh, s_len, 1), F32),
                        pltpu.VMEM((nh, s_len, dh), F32)],
        compiler_params=_params("arbitrary", "arbitrary"),
        name="fox_sample",
    )(q, kc, vc, kn, vn, cum_col, cum_t, cum_t)


def _band_bias_kernel(gf_ref, o_ref):
    f8 = jnp.broadcast_to(gf_ref[0], (8, BAND_ROLL))
    sub = lax.broadcasted_iota(jnp.int32, (8, BAND_ROLL), 0)
    base = f8
    for bb in range(1, 8):
        base = jnp.where(sub == bb, pltpu.roll(f8, bb, axis=1), base)
    for a in range(o_ref.shape[1] // 8):
        rows = base if a == 0 else pltpu.roll(base, 8 * a, axis=1)
        o_ref[0, 8 * a:8 * a + 8, :] = rows[:, :BAND_WIN]


def _band_bias(rel_table, rows):
    nh = rel_table.shape[0]
    assert rows % 8 == 0 and rows <= CHUNK
    u = jnp.arange(BAND_ROLL)
    v = jnp.where(u < BAND_ROLL - BAND_TQ, u, u - BAND_ROLL)
    idx = jnp.clip(LEFT - v, -MAX_REL, MAX_REL) + MAX_REL
    gf = jnp.take(rel_table.astype(F32), idx, axis=1).reshape(nh, 1, BAND_ROLL)
    return pl.pallas_call(
        _band_bias_kernel,
        grid=(nh,),
        in_specs=[pl.BlockSpec((1, 1, BAND_ROLL), lambda h: (h, 0, 0))],
        out_specs=pl.BlockSpec((1, rows, BAND_WIN), lambda h: (h, 0, 0)),
        out_shape=jax.ShapeDtypeStruct((nh, rows, BAND_WIN), F32),
        compiler_params=_params("arbitrary"),
        name="band_bias",
    )(gf)


def _band_bias_t_kernel(gf_ref, o_ref):
    f8 = jnp.broadcast_to(gf_ref[0], (8, BAND_ROLL))
    sub = lax.broadcasted_iota(jnp.int32, (8, BAND_ROLL), 0)
    base = f8
    for bb in range(1, 8):
        base = jnp.where(sub == bb, pltpu.roll(f8, bb, axis=1), base)
    key8 = lax.broadcasted_iota(jnp.int32, (8, BAND_TQ), 0)
    qry_chunk = lax.broadcasted_iota(jnp.int32, (8, BAND_TQ), 1) // CHUNK
    for a in range(BAND_EXT // 8):
        rows = base if a == 0 else pltpu.roll(base, 8 * a, axis=1)
        cd = qry_chunk - ((key8 + 8 * a) // CHUNK - LEFT_CHUNKS)
        rows = jnp.where((cd >= 0) & (cd <= LEFT_CHUNKS), rows[:, :BAND_TQ], NEG)
        for w in range(3):
            r0 = 8 * a - (LEFT - w * BAND_TQ)
            if 0 <= r0 < BAND_WIN:
                o_ref[0, w, r0:r0 + 8, :] = rows


def _band_bias_t(rel_table):
    nh = rel_table.shape[0]
    u = jnp.arange(BAND_ROLL)
    v = jnp.where(u < 2 * BAND_TQ, u, u - BAND_ROLL)
    idx = jnp.clip(v + LEFT, -MAX_REL, MAX_REL) + MAX_REL
    gf = (jnp.take(rel_table.astype(F32), idx, axis=1) * LOG2E).reshape(nh, 1, BAND_ROLL)
    return pl.pallas_call(
        _band_bias_t_kernel,
        grid=(nh,),
        in_specs=[pl.BlockSpec((1, 1, BAND_ROLL), lambda h: (h, 0, 0))],
        out_specs=pl.BlockSpec((1, 3, BAND_WIN, BAND_TQ), lambda h: (h, 0, 0, 0)),
        out_shape=jax.ShapeDtypeStruct((nh, 3, BAND_WIN, BAND_TQ), F32),
        compiler_params=_params("arbitrary"),
        name="band_bias_t",
    )(gf)


def _softmax_pv(score_parts, value_parts):
    probs = []
    for parts in score_parts:
        m = functools.reduce(jnp.maximum, [jnp.max(s, axis=1, keepdims=True) for s in parts])
        ps = [jnp.exp(s - m) for s in parts]
        l = functools.reduce(jnp.add, [jnp.sum(p, axis=1, keepdims=True) for p in ps])
        probs.append(([p.astype(BF16) for p in ps], l))
    outs = []
    for (ps, l), vals in zip(probs, value_parts):
        o = functools.reduce(jnp.add, [jnp.dot(p, v(), preferred_element_type=F32) for p, v in zip(ps, vals)])
        outs.append(o / l)
    return outs


def _band_kernel(q_ref, k_ref, v_ref, e_ref, o_ref, vt_sc, *, hb, dh):
    g = pl.program_id(2)
    nk = vt_sc.shape[1]
    nwin = BAND_WIN // BAND_TQ
    hcols = [slice(hh * dh, (hh + 1) * dh) for hh in range(hb)]

    @pl.when(g == 0)
    def _():
        ones = jnp.ones((FOX_PAD, BAND_TQ), BF16)
        for kb in range(nk):
            for hh, cols in enumerate(hcols):
                vt_sc[hh, kb, :dh, :] = v_ref[0, kb * BAND_TQ:(kb + 1) * BAND_TQ, cols].T
                vt_sc[hh, kb, dh:, :] = ones

    wb = jnp.maximum(g - LEFT // BAND_TQ, 0)
    ws = pl.multiple_of(wb * BAND_TQ, BAND_TQ)
    w = jnp.minimum(g, LEFT // BAND_TQ)
    ss = [jnp.dot(k_ref[0, pl.ds(ws, BAND_WIN), cols], q_ref[0, :, cols].T, preferred_element_type=F32)
          + e_ref[hh, w] for hh, cols in enumerate(hcols)]
    ps = [jnp.exp2(s - jnp.max(s, axis=0, keepdims=True)).astype(BF16) for s in ss]
    for hh, (cols, p) in enumerate(zip(hcols, ps)):
        acc = functools.reduce(jnp.add, [
            jnp.dot(vt_sc[hh, wb + j], p[j * BAND_TQ:(j + 1) * BAND_TQ], preferred_element_type=F32)
            for j in range(nwin)])
        o_ref[0, :, cols] = (acc[:dh] / acc[dh:dh + 1]).T.astype(o_ref.dtype)


def _band_prompt(q, k, v, bias_t, nh, dh, hb):
    b, t, _ = q.shape
    assert t % BAND_TQ == 0 and t >= BAND_WIN
    hb = min(hb, nh)
    qspec = pl.BlockSpec((1, BAND_TQ, hb * dh), lambda h, bb, g: (bb, g, h))
    kspec = pl.BlockSpec((1, t, hb * dh), lambda h, bb, g: (bb, 0, h))
    return pl.pallas_call(
        functools.partial(_band_kernel, hb=hb, dh=dh),
        grid=(nh // hb, b, t // BAND_TQ),
        in_specs=[qspec, kspec, kspec,
                  pl.BlockSpec((hb, 3, BAND_WIN, BAND_TQ), lambda h, bb, g: (h, 0, 0, 0),
                               pipeline_mode=pl.Buffered(1))],
        out_specs=qspec,
        out_shape=jax.ShapeDtypeStruct((b, t, nh * dh), BF16),
        scratch_shapes=[pltpu.VMEM((hb, t // BAND_TQ, dh + FOX_PAD, BAND_TQ), BF16)],
        compiler_params=_params("arbitrary", "arbitrary", "arbitrary"),
        name="band_prompt",
    )(q, k, v, bias_t)


def _band_s_kernel(q_ref, kc_ref, vc_ref, kn_ref, vn_ref, kn32_ref, vn32_ref, e_ref, o_ref, nk_ref, nv_ref,
                   *, nh, dh, keep, s_len):
    for new_ref, cache_ref, n32_ref in ((nk_ref, kc_ref, kn32_ref), (nv_ref, vc_ref, vn32_ref)):
        new_ref[0, :(keep - s_len) * nh, :] = cache_ref[0, s_len * nh:, :]
        new_ref[0, (keep - s_len) * nh:, :] = n32_ref[0]
    hcols = [slice(h * dh, (h + 1) * dh) for h in range(nh)]
    scores, values = [], []
    kc = _head_major(kc_ref, keep, nh)
    vc = _head_major(vc_ref, keep, nh)
    for h, cols in enumerate(hcols):
        q = q_ref[0, :, cols]
        e = e_ref[h]
        scores.append([_qk(q, kc[h]) + e[:, :keep],
                       _qk(q, kn_ref[0, :, cols]) + e[:, keep:keep + s_len]])
        values.append([functools.partial(lambda h: vc[h], h),
                       functools.partial(lambda cols: vn_ref[0, :, cols], cols)])
    for cols, o in zip(hcols, _softmax_pv(scores, values)):
        o_ref[0, :, cols] = o.astype(o_ref.dtype)


def _band_sample(q, kc, vc, kn, vn, kn32, vn32, bias, nh, dh):
    b, s_len, _ = q.shape
    keep = kc.shape[1] // nh
    assert keep == LEFT and s_len <= CHUNK and s_len % 8 == 0
    new = pl.BlockSpec((1, s_len, nh * dh), lambda bb: (bb, 0, 0))
    new32 = pl.BlockSpec((1, s_len * nh, dh), lambda bb: (bb, 0, 0))
    cache = pl.BlockSpec((1, keep * nh, dh), lambda bb: (bb, 0, 0))
    return pl.pallas_call(
        functools.partial(_band_s_kernel, nh=nh, dh=dh, keep=keep, s_len=s_len),
        grid=(b,),
        in_specs=[new, cache, cache, new, new, new32, new32,
                  pl.BlockSpec((nh, s_len, BAND_WIN), lambda bb: (0, 0, 0))],
        out_specs=[new, cache, cache],
        out_shape=[jax.ShapeDtypeStruct((b, s_len, nh * dh), BF16)] + [jax.ShapeDtypeStruct(kc.shape, F32)] * 2,
        compiler_params=_params("arbitrary"),
        name="band_sample",
    )(q, kc, vc, kn, vn, kn32, vn32, bias)


def _oproj_kernel(oa_ref, ob_ref, wa_ref, wb_ref, x_ref, gate_ref, o_ref, *wcopy_refs):
    ca, cb = wcopy_refs if wcopy_refs else (None, None)
    acc = (jnp.dot(oa_ref[...], _bf16_weights(wa_ref, ca), preferred_element_type=F32)
           + jnp.dot(ob_ref[...], _bf16_weights(wb_ref, cb), preferred_element_type=F32))
    o_ref[...] = x_ref[...] + gate_ref[...] * acc


def _gate_spec(gate, m, tm, tn, order=lambda fn: fn):
    if gate.ndim == 2:
        return pl.BlockSpec((tm, tn), order(lambda i, j: (i, j)))
    tiles_per_seq = m // gate.shape[0] // tm
    return pl.BlockSpec((None, 1, tn), order(lambda i, j: (i // tiles_per_seq, 0, j)))


def _oproj(oa, ob, wa, wb, x, gate, tm, tn):
    m, da = oa.shape
    db = ob.shape[1]
    (wa, ra), (wb, rb) = wa, wb
    d = wa.shape[1]
    tm = min(tm, m)
    tn = _tile(d, tn)
    emit_w = wa.dtype != BF16
    assert da == db and (not emit_w or m == tm)
    out_specs = [pl.BlockSpec((tm, tn), lambda i, j: (i, j))]
    out_shape = [jax.ShapeDtypeStruct((m, d), F32)]
    if emit_w:
        out_specs += [pl.BlockSpec((da, tn), lambda i, j: (0, j))] * 2
        out_shape += [jax.ShapeDtypeStruct((da, d), BF16)] * 2
    return pl.pallas_call(
        _oproj_kernel,
        grid=(m // tm, d // tn),
        in_specs=[pl.BlockSpec((tm, da), lambda i, j: (i, 0)),
                  pl.BlockSpec((tm, db), lambda i, j: (i, 0)),
                  pl.BlockSpec((da, tn), lambda i, j: (ra, j)),
                  pl.BlockSpec((db, tn), lambda i, j: (rb, j)),
                  pl.BlockSpec((tm, tn), lambda i, j: (i, j)),
                  _gate_spec(gate, m, tm, tn)],
        out_specs=out_specs,
        out_shape=out_shape,
        compiler_params=_params("arbitrary", "arbitrary"),
        name="out_proj",
    )(oa, ob, wa, wb, x, gate)


def _ffn_up_kernel(a_ref, wg_ref, wu_ref, o_ref, *wcopy_refs):
    cg, cu = wcopy_refs if wcopy_refs else (None, None)
    a = a_ref[...]
    g = jnp.dot(a, _bf16_weights(wg_ref, cg), preferred_element_type=F32)
    u = jnp.dot(a, _bf16_weights(wu_ref, cu), preferred_element_type=F32)
    o_ref[...] = (g * jax.nn.sigmoid(g) * u).astype(o_ref.dtype)


def _ffn_up(a, wg, wu, f, tm, tn):
    m, k = a.shape
    (wg, cg), (wu, cu) = wg, wu
    tm = min(tm, m)
    tn = _tile(f, tn)
    jg, ju = cg // tn, cu // tn
    emit_w = wg.dtype != BF16
    assert cg % tn == 0 and cu % tn == 0 and (not emit_w or m == tm)
    out_specs = [pl.BlockSpec((tm, tn), lambda i, j: (i, j))]
    out_shape = [jax.ShapeDtypeStruct((m, f), BF16)]
    if emit_w:
        out_specs += [pl.BlockSpec((k, tn), lambda i, j: (0, j))] * 2
        out_shape += [jax.ShapeDtypeStruct((k, f), BF16)] * 2
    return pl.pallas_call(
        _ffn_up_kernel,
        grid=(m // tm, f // tn),
        in_specs=[pl.BlockSpec((tm, k), lambda i, j: (i, 0)),
                  pl.BlockSpec((k, tn), lambda i, j: (0, jg + j)),
                  pl.BlockSpec((k, tn), lambda i, j: (0, ju + j))],
        out_specs=out_specs,
        out_shape=out_shape,
        compiler_params=_params("arbitrary", "arbitrary"),
        name="ffn_up",
    )(a, wg, wu)


def _ffn_down_kernel(a_ref, w_ref, x_ref, gate_ref, o_ref, *wcopy_refs):
    w = _bf16_weights(w_ref, wcopy_refs[0] if wcopy_refs else None)
    acc = jnp.dot(a_ref[...], w, preferred_element_type=F32)
    o_ref[...] = x_ref[...] + gate_ref[...] * acc


def _ffn_down(a, w_down, x, gate, tm, tn):
    m, f = a.shape
    d = w_down.shape[1]
    tm = min(tm, m)
    tn = _tile(d, tn)
    emit_w = w_down.dtype != BF16
    assert not emit_w or m == tm
    order = lambda fn: (lambda j, i: fn(i, j))
    w_mode = dict(pipeline_mode=pl.Buffered(1)) if m > tm else {}
    a_mode = dict(pipeline_mode=pl.Buffered(1)) if m == tm else {}
    out_specs = [pl.BlockSpec((tm, tn), order(lambda i, j: (i, j)))]
    out_shape = [jax.ShapeDtypeStruct((m, d), F32)]
    if emit_w:
        out_specs.append(pl.BlockSpec((f, tn), order(lambda i, j: (0, j))))
        out_shape.append(jax.ShapeDtypeStruct((f, d), BF16))
    return pl.pallas_call(
        _ffn_down_kernel,
        grid=(d // tn, m // tm),
        in_specs=[pl.BlockSpec((tm, f), order(lambda i, j: (i, 0)), **a_mode),
                  pl.BlockSpec((f, tn), order(lambda i, j: (0, j)), **w_mode),
                  pl.BlockSpec((tm, tn), order(lambda i, j: (i, j))),
                  _gate_spec(gate, m, tm, tn, order)],
        out_specs=out_specs,
        out_shape=out_shape,
        compiler_params=_params("arbitrary", "arbitrary"),
        name="ffn_down",
    )(a, w_down, x, gate)


def _layer(x, mod, cache, w, bias, *, nb, tr):
    b, t, d = x.shape
    m = b * t
    shift_a, scale_a, gate_a, shift_f, scale_f, gate_f = mod
    dh = w["g_q_a"].shape[0]
    nh_a = w["b_f"].shape[0]
    d_a = d_b = w["d_a"]
    nh_b = d_b // dh
    f = w["f"]
    scale = dh ** -0.5
    prompt = cache is None
    tm = 1024
    w16 = {}

    def gates(gt):
        if prompt:
            return gt.reshape(b, 1, d)
        return jnp.broadcast_to(gt[:, None, :], (b, t, d)).reshape(m, d)

    h = _modnorm(x, scale_a, shift_a, w["g_attn"], nb, tr).reshape(m, d)
    keep_p = min(LEFT, t)
    full32 = dict(tm=tm, tn=512, f32_rows=t, seq_len=t) if prompt else dict(tm=tm, tn=512, f32_rows=m, seq_len=m)
    tail32 = dict(tm=tm, tn=1024, f32_rows=keep_p, seq_len=t) if prompt else full32

    def pj(name, n, g, **kw):
        arr, col0 = w[name]
        kw.setdefault("tn", 1024 if prompt else 512)
        outs = _proj(h, arr, col0, n, g, dh=dh, **kw)
        if not prompt:
            *outs, wcopy = outs
            w16[name] = (wcopy, 0)
        return outs

    (q_a,) = pj("w_q_a", d_a, w["g_q_a"] * (scale * LOG2E if prompt else scale), do_rms=True, tm=tm)
    k_a32, k_a = pj("w_k_a", d_a, w["g_k_a"], do_rms=True, **full32)
    v_a32, v_a = pj("w_v_a", d_a, w["g_k_a"], do_rms=False, **full32)
    (q_b,) = pj("w_q_b", d_b, w["g_q_b"] * (scale * LOG2E if prompt else scale), do_rms=True, tm=tm)
    k_b32, k_b = pj("w_k_b", d_b, w["g_k_b"], do_rms=True, **tail32)
    v_b32, v_b = pj("w_v_b", d_b, w["g_k_b"], do_rms=False, **tail32)
    logf = _forget_gate(h, *w["w_f"], w["b_f"], tm).reshape(b, t, nh_a)

    r3 = lambda z: z.reshape(b, t, -1)
    if prompt:
        cum_t = _cumsum_t(logf.transpose(0, 2, 1))
        tq = 256
        o_a = _fox_prompt(r3(q_a), r3(k_a), r3(v_a), cum_t.transpose(0, 2, 1), nh_a, dh, tq, 8)
        o_b = _band_prompt(r3(q_b), r3(k_b), r3(v_b), bias[1], nh_b, dh, 8)
        new_k_b, new_v_b = k_b32.reshape(b, keep_p, nh_b, dh), v_b32.reshape(b, keep_p, nh_b, dh)
    else:
        ck_a, cv_a, clogf_a, ck_b, cv_b = cache
        past = ck_a.shape[1]
        tp = past + CUM_BLK
        lf_all = jnp.concatenate([clogf_a.astype(F32), logf, jnp.zeros((b, tp - past - t, nh_a), F32)], axis=1)
        cum_t = _cumsum_t(lf_all.transpose(0, 2, 1))
        o_a = _fox_sample(r3(q_a), ck_a.reshape(b, past * nh_a, dh), cv_a.reshape(b, past * nh_a, dh),
                          r3(k_a), r3(v_a), cum_t.transpose(0, 2, 1), cum_t, nh_a, dh, 1024)
        keep = ck_b.shape[1]
        o_b, new_k_b, new_v_b = _band_sample(
            r3(q_b), ck_b.reshape(b, keep * nh_b, dh).astype(F32), cv_b.reshape(b, keep * nh_b, dh).astype(F32),
            r3(k_b), r3(v_b), k_b32.reshape(b, t * nh_b, dh), v_b32.reshape(b, t * nh_b, dh), bias[0], nh_b, dh)
        new_k_b = new_k_b.reshape(b, keep, nh_b, dh).astype(ck_b.dtype)
        new_v_b = new_v_b.reshape(b, keep, nh_b, dh).astype(cv_b.dtype)

    x1, *wo16 = _oproj(o_a.reshape(m, d_a), o_b.reshape(m, d_b), w["w_oa"], w["w_ob"], x.reshape(m, d),
                       gates(gate_a), tm, 1024 if prompt else 512)
    h2 = _modnorm(x1.reshape(b, t, d), scale_f, shift_f, w["g_ffn"], nb, tr).reshape(m, d)
    act, *wgu16 = _ffn_up(h2, w["w_g"], w["w_u"], f, 2 * tm, 256)
    y, *wd16 = _ffn_down(act, w["w_down"], x1, gates(gate_f), 512, 1024 if prompt else 256)
    if not prompt:
        w16.update(w_oa=(wo16[0], 0), w_ob=(wo16[1], 0), w_g=(wgu16[0], 0), w_u=(wgu16[1], 0), w_down=wd16[0])
    return (y.reshape(b, t, d), k_a32.reshape(b, t, nh_a, dh), v_a32.reshape(b, t, nh_a, dh), logf,
            new_k_b, new_v_b, w16)


def kernel(x_prompt, x_sample, cache_k_a, cache_v_a, cache_logf_a, cache_k_b, cache_v_b, c_prompt, c_sample,
           w_ada, b_ada, g_attn, g_ffn, w_in, b_f, g_q_a, g_k_a, g_q_b, g_k_b, rel_table, w_o, w_gu, w_down):
    depth = w_ada.shape[0]
    d = x_prompt.shape[-1]
    nb_p, nb_s = c_prompt.shape[0], c_sample.shape[0]
    nh_a = b_f.shape[1]
    n_qkv = w_in.shape[2] - nh_a
    d_a = n_qkv // 6
    f = w_down.shape[1]
    assert 2 * d_a == w_o.shape[1] and w_gu.shape[2] == 2 * f
    y_p, y_s = x_prompt, x_sample
    outs_p, outs_s = [], []
    for l in range(depth):
        w_in_t = jnp.swapaxes(w_in[l], 0, 1)
        small = dict(w_f=(w_in_t, n_qkv), b_f=b_f[l], g_attn=g_attn[l], g_ffn=g_ffn[l], g_q_a=g_q_a[l], g_k_a=g_k_a[l],
                     g_q_b=g_q_b[l], g_k_b=g_k_b[l], d_a=d_a, f=f)
        names = ("w_q_a", "w_k_a", "w_v_a", "w_q_b", "w_k_b", "w_v_b")
        w32 = dict(small, w_oa=(w_o[l], 0), w_ob=(w_o[l], 1), w_g=(w_gu[l], 0), w_u=(w_gu[l], f), w_down=w_down[l],
                   **{name: (w_in_t, i * d_a) for i, name in enumerate(names)})
        c_all = jnp.concatenate([c_prompt, c_sample], axis=0)
        pad = (-c_all.shape[0]) % 16
        mod = _ada(jnp.pad(c_all, ((0, pad), (0, 0))), w_ada[l], b_ada[l])
        mod_p = [mod[:nb_p, i * d:(i + 1) * d] for i in range(6)]
        mod_s = [mod[nb_p:nb_p + nb_s, i * d:(i + 1) * d] for i in range(6)]
        bias = (_band_bias(rel_table[l], x_sample.shape[1]), _band_bias_t(rel_table[l]))
        cache = (cache_k_a[l], cache_v_a[l], cache_logf_a[l], cache_k_b[l], cache_v_b[l])
        y_s, *rest_s, w16 = _layer(y_s, mod_s, cache, w32, bias, nb=nb_s, tr=x_sample.shape[1])
        y_p, *rest_p, _ = _layer(y_p, mod_p, None, dict(small, **w16), bias, nb=1, tr=min(512, x_prompt.shape[1]))
        outs_p.append(rest_p)
        outs_s.append(rest_s)
    stack = lambda outs, i: jnp.stack([o[i] for o in outs])
    return (y_p, y_s, *[stack(outs_p, i) for i in range(5)], *[stack(outs_s, i) for i in range(5)])
```

```python
import functools

import jax
import jax.numpy as jnp
from jax import lax
from jax.experimental import pallas as pl
from jax.experimental.pallas import tpu as pltpu

CHUNK = 64
LEFT_CHUNKS = 8
LEFT = LEFT_CHUNKS * CHUNK
MAX_REL = 128
EPS = 1e-6
NEG = -1e30
LOG2E = 1.4426950408889634

LANES = 128
BAND_TQ = 256
BAND_WIN = LEFT + BAND_TQ
BAND_EXT = BAND_WIN + LEFT
BAND_ROLL = 2048
CUM_BLK = 256
FOX_PAD = 16
VMEM_LIMIT_BYTES = 56 * 1024 * 1024

F32 = jnp.float32
BF16 = jnp.bfloat16


def _params(*semantics):
    return pltpu.CompilerParams(dimension_semantics=semantics, vmem_limit_bytes=VMEM_LIMIT_BYTES)


def _tile(n, want):
    if n <= want:
        return n
    t = (want // LANES) * LANES
    while t >= LANES:
        if n % t == 0:
            return t
        t -= LANES
    raise ValueError(f"no lane-aligned tile for {n}")


def _ada_kernel(c_ref, w_ref, b_ref, o_ref):
    c = c_ref[...]
    a = (c * jax.nn.sigmoid(c)).astype(BF16)
    o_ref[...] = jnp.dot(a, w_ref[...].astype(BF16), preferred_element_type=F32) + b_ref[...]


def _ada(c, w_ada, b_ada):
    m, d = c.shape
    n = w_ada.shape[1]
    tn = _tile(n, 512)
    return pl.pallas_call(
        _ada_kernel,
        grid=(n // tn,),
        in_specs=[pl.BlockSpec((m, d), lambda j: (0, 0)),
                  pl.BlockSpec((d, tn), lambda j: (0, j)),
                  pl.BlockSpec((1, tn), lambda j: (0, j))],
        out_specs=pl.BlockSpec((m, tn), lambda j: (0, j)),
        out_shape=jax.ShapeDtypeStruct((m, n), F32),
        compiler_params=_params("arbitrary"),
        name="ada_mod",
    )(c, w_ada, b_ada.reshape(1, n))


def _modnorm_kernel(x_ref, sc_ref, sh_ref, g_ref, o_ref):
    x = x_ref[...]
    ms = jnp.mean(x * x, axis=-1, keepdims=True)
    y = x * lax.rsqrt(ms + EPS) * g_ref[...]
    o_ref[...] = (y * (1.0 + sc_ref[...]) + sh_ref[...]).astype(o_ref.dtype)


def _modnorm(x, scale, shift, g, nb, tr):
    b, t, d = x.shape
    bs = pl.BlockSpec((nb, 1, d), lambda i, j: (i, 0, 0))
    return pl.pallas_call(
        _modnorm_kernel,
        grid=(b // nb, t // tr),
        in_specs=[pl.BlockSpec((nb, tr, d), lambda i, j: (i, j, 0)), bs, bs,
                  pl.BlockSpec((1, 1, d), lambda i, j: (0, 0, 0))],
        out_specs=pl.BlockSpec((nb, tr, d), lambda i, j: (i, j, 0)),
        out_shape=jax.ShapeDtypeStruct((b, t, d), BF16),
        compiler_params=_params("arbitrary", "arbitrary"),
        name="modnorm",
    )(x, scale.reshape(b, 1, d), shift.reshape(b, 1, d), g.reshape(1, 1, d))


def _bf16_weights(w_ref, copy_ref, transposed=False):
    if copy_ref is None:
        return w_ref[...]
    w = w_ref[...]
    w = (w.T if transposed else w).astype(BF16)
    copy_ref[...] = w
    return w


def _proj_kernel(a_ref, w_ref, g_ref, *o_refs, do_rms, dh, nh, rows32, tiles_per_seq, emit_w):
    if emit_w:
        *o_refs, wcopy_ref = o_refs
    w = _bf16_weights(w_ref, wcopy_ref if emit_w else None, transposed=True)
    acc = jnp.dot(a_ref[...], w, preferred_element_type=F32)
    o16_ref = o_refs[-1]
    tm, tn = acc.shape
    hpt = tn // dh
    g = g_ref[...]
    ys = []
    for hh in range(hpt):
        y = acc[:, hh * dh:(hh + 1) * dh]
        if do_rms:
            ms = jnp.mean(y * y, axis=-1, keepdims=True)
            y = y * lax.rsqrt(ms + EPS) * g
        o16_ref[:, hh * dh:(hh + 1) * dh] = y.astype(o16_ref.dtype)
        ys.append(y)
    if rows32:
        o32_ref = o_refs[0]
        h0 = pl.program_id(1) * hpt

        def store32():
            for hh in range(hpt):
                o32_ref[pl.ds(h0 + hh, rows32, stride=nh), :] = ys[hh][tm - rows32:, :]

        if rows32 == tm:
            store32()
        else:
            pl.when(pl.program_id(0) % tiles_per_seq == tiles_per_seq - 1)(store32)


def _proj(a, w, col0, n, g, *, do_rms, dh, tm, tn, f32_rows=0, seq_len=None):
    m, k = a.shape
    tm = min(tm, m)
    tn = _tile(n, tn)
    nh = n // dh
    j0 = col0 // tn
    emit_w = w.dtype != BF16
    assert col0 % tn == 0 and m % tm == 0 and (not emit_w or m == tm)
    w_mode = dict(pipeline_mode=pl.Buffered(1)) if n == tn else {}
    out_specs = [pl.BlockSpec((tm, tn), lambda i, j: (i, j))]
    out_shape = [jax.ShapeDtypeStruct((m, n), BF16)]
    rows32, tps = 0, 1
    if f32_rows:
        tps = seq_len // tm
        assert seq_len % tm == 0
        if f32_rows == seq_len:
            rows32 = tm
            out_specs.insert(0, pl.BlockSpec((tm * nh, dh), lambda i, j: (i, 0)))
            out_shape.insert(0, jax.ShapeDtypeStruct((m * nh, dh), F32))
        else:
            rows32 = f32_rows
            assert rows32 <= tm
            out_specs.insert(0, pl.BlockSpec((rows32 * nh, dh), lambda i, j: (i // tps, 0)))
            out_shape.insert(0, jax.ShapeDtypeStruct((m // seq_len * rows32 * nh, dh), F32))
    if emit_w:
        out_specs.append(pl.BlockSpec((k, tn), lambda i, j: (0, j)))
        out_shape.append(jax.ShapeDtypeStruct((k, n), BF16))
    return pl.pallas_call(
        functools.partial(_proj_kernel, do_rms=do_rms, dh=dh, nh=nh, rows32=rows32, tiles_per_seq=tps,
                          emit_w=emit_w),
        grid=(m // tm, n // tn),
        in_specs=[pl.BlockSpec((tm, k), lambda i, j: (i, 0)),
                  pl.BlockSpec((tn, k), lambda i, j: (j0 + j, 0)) if emit_w else
                  pl.BlockSpec((k, tn), lambda i, j: (0, j0 + j), **w_mode),
                  pl.BlockSpec((1, dh), lambda i, j: (0, 0))],
        out_specs=out_specs,
        out_shape=out_shape,
        compiler_params=_params("arbitrary", "arbitrary"),
        name="proj",
    )(a, w, g.reshape(1, dh).astype(F32))


def _gate_kernel(a_ref, w_ref, b_ref, o_ref):
    z = _qk(a_ref[...], w_ref[...].astype(BF16)) + b_ref[...]
    o_ref[...] = jnp.minimum(z, 0.0) - jnp.log1p(jnp.exp(-jnp.abs(z)))


def _forget_gate(a, w_t, row0, b_f, tm):
    m, k = a.shape
    nh = b_f.shape[0]
    tm = min(tm, m)
    assert row0 % nh == 0
    return pl.pallas_call(
        _gate_kernel,
        grid=(m // tm,),
        in_specs=[pl.BlockSpec((tm, k), lambda i: (i, 0)),
                  pl.BlockSpec((nh, k), lambda i: (row0 // nh, 0)),
                  pl.BlockSpec((1, nh), lambda i: (0, 0))],
        out_specs=pl.BlockSpec((tm, nh), lambda i: (i, 0)),
        out_shape=jax.ShapeDtypeStruct((m, nh), F32),
        compiler_params=_params("arbitrary"),
        name="forget_gate",
    )(a, w_t, b_f.reshape(1, nh).astype(F32))


def _cumsum_kernel(x_ref, o_ref):
    h, tp = x_ref.shape[1], x_ref.shape[2]
    r = lax.broadcasted_iota(jnp.int32, (CUM_BLK, CUM_BLK), 0)
    c = lax.broadcasted_iota(jnp.int32, (CUM_BLK, CUM_BLK), 1)
    tri = (r <= c).astype(BF16)
    carry = jnp.zeros((h, 1), F32)
    for blk in range(tp // CUM_BLK):
        x = x_ref[0, :, blk * CUM_BLK:(blk + 1) * CUM_BLK]
        hi = x.astype(BF16)
        r1 = x - hi.astype(F32)
        mid = r1.astype(BF16)
        lo = (r1 - mid.astype(F32)).astype(BF16)
        cs = (jnp.dot(hi, tri, preferred_element_type=F32)
              + jnp.dot(mid, tri, preferred_element_type=F32)
              + jnp.dot(lo, tri, preferred_element_type=F32)) + carry
        o_ref[0, :, blk * CUM_BLK:(blk + 1) * CUM_BLK] = cs
        carry = cs[:, CUM_BLK - 1:CUM_BLK]


def _cumsum_t(logf_t):
    b, h, tp = logf_t.shape
    return pl.pallas_call(
        _cumsum_kernel,
        grid=(b,),
        in_specs=[pl.BlockSpec((1, h, tp), lambda i: (i, 0, 0))],
        out_specs=pl.BlockSpec((1, h, tp), lambda i: (i, 0, 0)),
        out_shape=jax.ShapeDtypeStruct((b, h, tp), F32),
        compiler_params=_params("arbitrary"),
        name="logf_cumsum",
    )(logf_t)


def _qk(q, k):
    return lax.dot_general(q, k, (((1,), (1,)), ((), ())), preferred_element_type=F32)


def _fox_kernel(q_ref, k_ref, v_ref, cc_ref, o_ref, vt_sc, ka_sc, *, tq, hb, dh, nh):
    hg = pl.program_id(1)
    i = pl.program_id(2)
    nk = vt_sc.shape[1]
    heads = range(hb)
    hcols = [slice(hh * dh, (hh + 1) * dh) for hh in heads]

    @pl.when(i == 0)
    def _():
        ones = jnp.ones((FOX_PAD, tq), BF16)
        r = lax.broadcasted_iota(jnp.int32, (LANES, dh), 0)
        lane = lax.broadcasted_iota(jnp.int32, (LANES, dh), 1)
        sels = [((r % nh == hg * hb + hh) & (r // nh == lane) & (r < 3 * nh)).astype(BF16) for hh in heads]
        for kb in range(nk):
            rows = slice(kb * tq, (kb + 1) * tq)
            ck = cc_ref[0, rows, :] * (-LOG2E)
            hi = ck.astype(BF16).astype(F32)
            mid = (ck - hi).astype(BF16).astype(F32)
            lo = ck - hi - mid
            packed = (hi + pltpu.roll(mid, nh, axis=1) + pltpu.roll(lo, 2 * nh, axis=1)).astype(BF16)
            for hh in heads:
                vt_sc[hh, kb, :dh, :] = v_ref[0, rows, hcols[hh]].T
                vt_sc[hh, kb, dh:, :] = ones
                ka_sc[hh, rows, :dh] = k_ref[0, rows, hcols[hh]]
                ka_sc[hh, rows, dh:] = jnp.dot(packed, sels[hh], preferred_element_type=F32).astype(BF16)

    ones_rows = (lax.broadcasted_iota(jnp.int32, (dh, tq), 0) < 3).astype(BF16)
    qts = [jnp.concatenate([q_ref[0, :, cols].T, ones_rows], axis=0) for cols in hcols]

    def scores(kb):
        off = pl.multiple_of(kb * tq, tq)
        out = []
        for hh in heads:
            out.append(jnp.dot(ka_sc[hh, pl.ds(off, tq), :], qts[hh], preferred_element_type=F32))
        return out

    def step(kb, state, mask=None):
        stats = []
        for s, (m, _) in zip(scores(kb), state):
            if mask is not None:
                s = jnp.where(mask, s, NEG)
            m_new = jnp.maximum(m, jnp.max(s, axis=0, keepdims=True))
            stats.append((m_new, jnp.exp2(m - m_new), jnp.exp2(s - m_new).astype(BF16)))
        return [(m_new, alpha * acc + jnp.dot(vt_sc[hh, kb], p, preferred_element_type=F32))
                for hh, ((m_new, alpha, p), (_, acc)) in enumerate(zip(stats, state))]

    state = [(jnp.full((1, tq), NEG, F32), jnp.zeros((dh + FOX_PAD, tq), F32)) for _ in heads]
    state = lax.fori_loop(0, i, step, state)
    key = lax.broadcasted_iota(jnp.int32, (tq, tq), 0)
    qry = lax.broadcasted_iota(jnp.int32, (tq, tq), 1)
    state = step(i, state, mask=key <= qry)
    for cols, (_, acc) in zip(hcols, state):
        o_ref[0, :, cols] = (acc[:dh] / acc[dh:dh + 1]).T.astype(o_ref.dtype)


def _fox_prompt(q, k, v, cum_col, nh, dh, tq, hb):
    b, t, _ = q.shape
    hb = min(hb, nh)
    assert t % tq == 0 and tq % LANES == 0 and 3 * nh <= LANES and dh == LANES
    cum_col = jnp.pad(cum_col, ((0, 0), (0, 0), (0, LANES - nh)))
    qspec = pl.BlockSpec((1, tq, hb * dh), lambda bb, h, i: (bb, i, h))
    kspec = pl.BlockSpec((1, t, hb * dh), lambda bb, h, i: (bb, 0, h))
    return pl.pallas_call(
        functools.partial(_fox_kernel, tq=tq, hb=hb, dh=dh, nh=nh),
        grid=(b, nh // hb, t // tq),
        in_specs=[qspec, kspec, kspec, pl.BlockSpec((1, t, LANES), lambda bb, h, i: (bb, 0, 0))],
        out_specs=qspec,
        out_shape=jax.ShapeDtypeStruct((b, t, nh * dh), BF16),
        scratch_shapes=[pltpu.VMEM((hb, t // tq, dh + FOX_PAD, tq), BF16), pltpu.VMEM((hb, t, 2 * dh), BF16)],
        compiler_params=_params("arbitrary", "arbitrary", "arbitrary"),
        name="fox_prompt",
    )(q, k, v, cum_col)


def _head_major(ref, n, nh):
    return pltpu.einshape("tjd->jtd", ref[0].astype(BF16).reshape(n, nh, ref.shape[2]))


def _fox_s_kernel(q_ref, kc_ref, vc_ref, kn_ref, vn_ref, cc_ref, crc_ref, crn_ref, o_ref, m_sc, l_sc, acc_sc,
                  *, nh, dh, tc, s_len):
    c = pl.program_id(1)
    cc = cc_ref[0][:s_len]
    r = lax.broadcasted_iota(jnp.int32, (s_len, s_len), 0)
    col = lax.broadcasted_iota(jnp.int32, (s_len, s_len), 1)

    @pl.when(c == 0)
    def _():
        hcols = [slice(h * dh, (h + 1) * dh) for h in range(nh)]
        ss = [_qk(q_ref[0, :, cols], kn_ref[0, :, cols]) + (cc[:, h:h + 1] - crn_ref[0, h:h + 1, :s_len])
              for h, cols in enumerate(hcols)]
        ps = []
        for h, s in enumerate(ss):
            s = jnp.where(col <= r, s, NEG)
            m = jnp.max(s, axis=1, keepdims=True)
            p = jnp.exp(s - m)
            m_sc[h] = m
            l_sc[h] = jnp.sum(p, axis=1, keepdims=True)
            ps.append(p.astype(BF16))
        for h, p in enumerate(ps):
            acc_sc[h] = jnp.dot(p, vn_ref[0, :, hcols[h]], preferred_element_type=F32)

    kc = _head_major(kc_ref, tc, nh)
    vc = _head_major(vc_ref, tc, nh)
    ss = [_qk(q_ref[0, :, h * dh:(h + 1) * dh], kc[h])
          + (cc[:, h:h + 1] - crc_ref[0, h:h + 1, :]) for h in range(nh)]
    ps = []
    for h, s in enumerate(ss):
        m_old = m_sc[h]
        m_new = jnp.maximum(m_old, jnp.max(s, axis=1, keepdims=True))
        alpha = jnp.exp(m_old - m_new)
        p = jnp.exp(s - m_new)
        m_sc[h] = m_new
        l_sc[h] = alpha * l_sc[h] + jnp.sum(p, axis=1, keepdims=True)
        ps.append((alpha, p.astype(BF16)))
    for h, (alpha, p) in enumerate(ps):
        acc_sc[h] = alpha * acc_sc[h] + jnp.dot(p, vc[h], preferred_element_type=F32)

    @pl.when(c == pl.num_programs(1) - 1)
    def _():
        for h in range(nh):
            o_ref[0, :, h * dh:(h + 1) * dh] = (acc_sc[h] / l_sc[h]).astype(o_ref.dtype)


def _fox_sample(q, kc, vc, kn, vn, cum_col, cum_t, nh, dh, tc):
    b, s_len, _ = q.shape
    past = kc.shape[1] // nh
    tp = cum_col.shape[1]
    tc = min(tc, past)
    assert past % CUM_BLK == 0 and tp - past == CUM_BLK and past % tc == 0 and s_len <= CUM_BLK
    new = pl.BlockSpec((1, s_len, nh * dh), lambda bb, c: (bb, 0, 0))
    cache = pl.BlockSpec((1, tc * nh, dh), lambda bb, c: (bb, c, 0))
    return pl.pallas_call(
        functools.partial(_fox_s_kernel, nh=nh, dh=dh, tc=tc, s_len=s_len),
        grid=(b, past // tc),
        in_specs=[new, cache, cache, new, new,
                  pl.BlockSpec((1, CUM_BLK, nh), lambda bb, c: (bb, past // CUM_BLK, 0)),
                  pl.BlockSpec((1, nh, tc), lambda bb, c: (bb, 0, c)),
                  pl.BlockSpec((1, nh, CUM_BLK), lambda bb, c: (bb, 0, past // CUM_BLK))],
        out_specs=new,
        out_shape=jax.ShapeDtypeStruct((b, s_len, nh * dh), BF16),
        scratch_shapes=[pltpu.VMEM((nh, s_len, 1), F32), pltpu.VMEM((nh, s_len, 1), F32),
                        pltpu.VMEM((nh, s_len, dh), F32)],
        compiler_params=_params("arbitrary", "arbitrary"),
        name="fox_sample",
    )(q, kc, vc, kn, vn, cum_col, cum_t, cum_t)


def _band_bias_kernel(gf_ref, o_ref):
    f8 = jnp.broadcast_to(gf_ref[0], (8, BAND_ROLL))
    sub = lax.broadcasted_iota(jnp.int32, (8, BAND_ROLL), 0)
    base = f8
    for bb in range(1, 8):
        base = jnp.where(sub == bb, pltpu.roll(f8, bb, axis=1), base)
    for a in range(o_ref.shape[1] // 8):
        rows = base if a == 0 else pltpu.roll(base, 8 * a, axis=1)
        o_ref[0, 8 * a:8 * a + 8, :] = rows[:, :BAND_WIN]


def _band_bias(rel_table, rows):
    nh = rel_table.shape[0]
    assert rows % 8 == 0 and rows <= CHUNK
    u = jnp.arange(BAND_ROLL)
    v = jnp.where(u < BAND_ROLL - BAND_TQ, u, u - BAND_ROLL)
    idx = jnp.clip(LEFT - v, -MAX_REL, MAX_REL) + MAX_REL
    gf = jnp.take(rel_table.astype(F32), idx, axis=1).reshape(nh, 1, BAND_ROLL)
    return pl.pallas_call(
        _band_bias_kernel,
        grid=(nh,),
        in_specs=[pl.BlockSpec((1, 1, BAND_ROLL), lambda h: (h, 0, 0))],
        out_specs=pl.BlockSpec((1, rows, BAND_WIN), lambda h: (h, 0, 0)),
        out_shape=jax.ShapeDtypeStruct((nh, rows, BAND_WIN), F32),
        compiler_params=_params("arbitrary"),
        name="band_bias",
    )(gf)


def _band_bias_t_kernel(gf_ref, o_ref):
    f8 = jnp.broadcast_to(gf_ref[0], (8, BAND_ROLL))
    sub = lax.broadcasted_iota(jnp.int32, (8, BAND_ROLL), 0)
    base = f8
    for bb in range(1, 8):
        base = jnp.where(sub == bb, pltpu.roll(f8, bb, axis=1), base)
    key8 = lax.broadcasted_iota(jnp.int32, (8, BAND_TQ), 0)
    qry_chunk = lax.broadcasted_iota(jnp.int32, (8, BAND_TQ), 1) // CHUNK
    for a in range(BAND_EXT // 8):
        rows = base if a == 0 else pltpu.roll(base, 8 * a, axis=1)
        cd = qry_chunk - ((key8 + 8 * a) // CHUNK - LEFT_CHUNKS)
        rows = jnp.where((cd >= 0) & (cd <= LEFT_CHUNKS), rows[:, :BAND_TQ], NEG)
        for w in range(3):
            r0 = 8 * a - (LEFT - w * BAND_TQ)
            if 0 <= r0 < BAND_WIN:
                o_ref[0, w, r0:r0 + 8, :] = rows


def _band_bias_t(rel_table):
    nh = rel_table.shape[0]
    u = jnp.arange(BAND_ROLL)
    v = jnp.where(u < 2 * BAND_TQ, u, u - BAND_ROLL)
    idx = jnp.clip(v + LEFT, -MAX_REL, MAX_REL) + MAX_REL
    gf = (jnp.take(rel_table.astype(F32), idx, axis=1) * LOG2E).reshape(nh, 1, BAND_ROLL)
    return pl.pallas_call(
        _band_bias_t_kernel,
        grid=(nh,),
        in_specs=[pl.BlockSpec((1, 1, BAND_ROLL), lambda h: (h, 0, 0))],
        out_specs=pl.BlockSpec((1, 3, BAND_WIN, BAND_TQ), lambda h: (h, 0, 0, 0)),
        out_shape=jax.ShapeDtypeStruct((nh, 3, BAND_WIN, BAND_TQ), F32),
        compiler_params=_params("arbitrary"),
        name="band_bias_t",
    )(gf)


def _softmax_pv(score_parts, value_parts):
    probs = []
    for parts in score_parts:
        m = functools.reduce(jnp.maximum, [jnp.max(s, axis=1, keepdims=True) for s in parts])
        ps = [jnp.exp(s - m) for s in parts]
        l = functools.reduce(jnp.add, [jnp.sum(p, axis=1, keepdims=True) for p in ps])
        probs.append(([p.astype(BF16) for p in ps], l))
    outs = []
    for (ps, l), vals in zip(probs, value_parts):
        o = functools.reduce(jnp.add, [jnp.dot(p, v(), preferred_element_type=F32) for p, v in zip(ps, vals)])
        outs.append(o / l)
    return outs


def _band_kernel(q_ref, k_ref, v_ref, e_ref, o_ref, vt_sc, *, hb, dh):
    g = pl.program_id(2)
    nk = vt_sc.shape[1]
    nwin = BAND_WIN // BAND_TQ
    hcols = [slice(hh * dh, (hh + 1) * dh) for hh in range(hb)]

    @pl.when(g == 0)
    def _():
        ones = jnp.ones((FOX_PAD, BAND_TQ), BF16)
        for kb in range(nk):
            for hh, cols in enumerate(hcols):
                vt_sc[hh, kb, :dh, :] = v_ref[0, kb * BAND_TQ:(kb + 1) * BAND_TQ, cols].T
                vt_sc[hh, kb, dh:, :] = ones

    wb = jnp.maximum(g - LEFT // BAND_TQ, 0)
    ws = pl.multiple_of(wb * BAND_TQ, BAND_TQ)
    w = jnp.minimum(g, LEFT // BAND_TQ)
    ss = [jnp.dot(k_ref[0, pl.ds(ws, BAND_WIN), cols], q_ref[0, :, cols].T, preferred_element_type=F32)
          + e_ref[hh, w] for hh, cols in enumerate(hcols)]
    ps = [jnp.exp2(s - jnp.max(s, axis=0, keepdims=True)).astype(BF16) for s in ss]
    for hh, (cols, p) in enumerate(zip(hcols, ps)):
        acc = functools.reduce(jnp.add, [
            jnp.dot(vt_sc[hh, wb + j], p[j * BAND_TQ:(j + 1) * BAND_TQ], preferred_element_type=F32)
            for j in range(nwin)])
        o_ref[0, :, cols] = (acc[:dh] / acc[dh:dh + 1]).T.astype(o_ref.dtype)


def _band_prompt(q, k, v, bias_t, nh, dh, hb):
    b, t, _ = q.shape
    assert t % BAND_TQ == 0 and t >= BAND_WIN
    hb = min(hb, nh)
    qspec = pl.BlockSpec((1, BAND_TQ, hb * dh), lambda h, bb, g: (bb, g, h))
    kspec = pl.BlockSpec((1, t, hb * dh), lambda h, bb, g: (bb, 0, h))
    return pl.pallas_call(
        functools.partial(_band_kernel, hb=hb, dh=dh),
        grid=(nh // hb, b, t // BAND_TQ),
        in_specs=[qspec, kspec, kspec,
                  pl.BlockSpec((hb, 3, BAND_WIN, BAND_TQ), lambda h, bb, g: (h, 0, 0, 0),
                               pipeline_mode=pl.Buffered(1))],
        out_specs=qspec,
        out_shape=jax.ShapeDtypeStruct((b, t, nh * dh), BF16),
        scratch_shapes=[pltpu.VMEM((hb, t // BAND_TQ, dh + FOX_PAD, BAND_TQ), BF16)],
        compiler_params=_params("arbitrary", "arbitrary", "arbitrary"),
        name="band_prompt",
    )(q, k, v, bias_t)


def _band_s_kernel(q_ref, kc_ref, vc_ref, kn_ref, vn_ref, kn32_ref, vn32_ref, e_ref, o_ref, nk_ref, nv_ref,
                   *, nh, dh, keep, s_len):
    for new_ref, cache_ref, n32_ref in ((nk_ref, kc_ref, kn32_ref), (nv_ref, vc_ref, vn32_ref)):
        new_ref[0, :(keep - s_len) * nh, :] = cache_ref[0, s_len * nh:, :]
        new_ref[0, (keep - s_len) * nh:, :] = n32_ref[0]
    hcols = [slice(h * dh, (h + 1) * dh) for h in range(nh)]
    scores, values = [], []
    kc = _head_major(kc_ref, keep, nh)
    vc = _head_major(vc_ref, keep, nh)
    for h, cols in enumerate(hcols):
        q = q_ref[0, :, cols]
        e = e_ref[h]
        scores.append([_qk(q, kc[h]) + e[:, :keep],
                       _qk(q, kn_ref[0, :, cols]) + e[:, keep:keep + s_len]])
        values.append([functools.partial(lambda h: vc[h], h),
                       functools.partial(lambda cols: vn_ref[0, :, cols], cols)])
    for cols, o in zip(hcols, _softmax_pv(scores, values)):
        o_ref[0, :, cols] = o.astype(o_ref.dtype)


def _band_sample(q, kc, vc, kn, vn, kn32, vn32, bias, nh, dh):
    b, s_len, _ = q.shape
    keep = kc.shape[1] // nh
    assert keep == LEFT and s_len <= CHUNK and s_len % 8 == 0
    new = pl.BlockSpec((1, s_len, nh * dh), lambda bb: (bb, 0, 0))
    new32 = pl.BlockSpec((1, s_len * nh, dh), lambda bb: (bb, 0, 0))
    cache = pl.BlockSpec((1, keep * nh, dh), lambda bb: (bb, 0, 0))
    return pl.pallas_call(
        functools.partial(_band_s_kernel, nh=nh, dh=dh, keep=keep, s_len=s_len),
        grid=(b,),
        in_specs=[new, cache, cache, new, new, new32, new32,
                  pl.BlockSpec((nh, s_len, BAND_WIN), lambda bb: (0, 0, 0))],
        out_specs=[new, cache, cache],
        out_shape=[jax.ShapeDtypeStruct((b, s_len, nh * dh), BF16)] + [jax.ShapeDtypeStruct(kc.shape, F32)] * 2,
        compiler_params=_params("arbitrary"),
        name="band_sample",
    )(q, kc, vc, kn, vn, kn32, vn32, bias)


def _oproj_kernel(oa_ref, ob_ref, wa_ref, wb_ref, x_ref, gate_ref, o_ref, *wcopy_refs):
    ca, cb = wcopy_refs if wcopy_refs else (None, None)
    acc = (jnp.dot(oa_ref[...], _bf16_weights(wa_ref, ca), preferred_element_type=F32)
           + jnp.dot(ob_ref[...], _bf16_weights(wb_ref, cb), preferred_element_type=F32))
    o_ref[...] = x_ref[...] + gate_ref[...] * acc


def _gate_spec(gate, m, tm, tn, order=lambda fn: fn):
    if gate.ndim == 2:
        return pl.BlockSpec((tm, tn), order(lambda i, j: (i, j)))
    tiles_per_seq = m // gate.shape[0] // tm
    return pl.BlockSpec((None, 1, tn), order(lambda i, j: (i // tiles_per_seq, 0, j)))


def _oproj(oa, ob, wa, wb, x, gate, tm, tn):
    m, da = oa.shape
    db = ob.shape[1]
    (wa, ra), (wb, rb) = wa, wb
    d = wa.shape[1]
    tm = min(tm, m)
    tn = _tile(d, tn)
    emit_w = wa.dtype != BF16
    assert da == db and (not emit_w or m == tm)
    out_specs = [pl.BlockSpec((tm, tn), lambda i, j: (i, j))]
    out_shape = [jax.ShapeDtypeStruct((m, d), F32)]
    if emit_w:
        out_specs += [pl.BlockSpec((da, tn), lambda i, j: (0, j))] * 2
        out_shape += [jax.ShapeDtypeStruct((da, d), BF16)] * 2
    return pl.pallas_call(
        _oproj_kernel,
        grid=(m // tm, d // tn),
        in_specs=[pl.BlockSpec((tm, da), lambda i, j: (i, 0)),
                  pl.BlockSpec((tm, db), lambda i, j: (i, 0)),
                  pl.BlockSpec((da, tn), lambda i, j: (ra, j)),
                  pl.BlockSpec((db, tn), lambda i, j: (rb, j)),
                  pl.BlockSpec((tm, tn), lambda i, j: (i, j)),
                  _gate_spec(gate, m, tm, tn)],
        out_specs=out_specs,
        out_shape=out_shape,
        compiler_params=_params("arbitrary", "arbitrary"),
        name="out_proj",
    )(oa, ob, wa, wb, x, gate)


def _ffn_up_kernel(a_ref, wg_ref, wu_ref, o_ref, *wcopy_refs):
    cg, cu = wcopy_refs if wcopy_refs else (None, None)
    a = a_ref[...]
    g = jnp.dot(a, _bf16_weights(wg_ref, cg), preferred_element_type=F32)
    u = jnp.dot(a, _bf16_weights(wu_ref, cu), preferred_element_type=F32)
    o_ref[...] = (g * jax.nn.sigmoid(g) * u).astype(o_ref.dtype)


def _ffn_up(a, wg, wu, f, tm, tn):
    m, k = a.shape
    (wg, cg), (wu, cu) = wg, wu
    tm = min(tm, m)
    tn = _tile(f, tn)
    jg, ju = cg // tn, cu // tn
    emit_w = wg.dtype != BF16
    assert cg % tn == 0 and cu % tn == 0 and (not emit_w or m == tm)
    out_specs = [pl.BlockSpec((tm, tn), lambda i, j: (i, j))]
    out_shape = [jax.ShapeDtypeStruct((m, f), BF16)]
    if emit_w:
        out_specs += [pl.BlockSpec((k, tn), lambda i, j: (0, j))] * 2
        out_shape += [jax.ShapeDtypeStruct((k, f), BF16)] * 2
    return pl.pallas_call(
        _ffn_up_kernel,
        grid=(m // tm, f // tn),
        in_specs=[pl.BlockSpec((tm, k), lambda i, j: (i, 0)),
                  pl.BlockSpec((k, tn), lambda i, j: (0, jg + j)),
                  pl.BlockSpec((k, tn), lambda i, j: (0, ju + j))],
        out_specs=out_specs,
        out_shape=out_shape,
        compiler_params=_params("arbitrary", "arbitrary"),
        name="ffn_up",
    )(a, wg, wu)


def _ffn_down_kernel(a_ref, w_ref, x_ref, gate_ref, o_ref, *wcopy_refs):
    w = _bf16_weights(w_ref, wcopy_refs[0] if wcopy_refs else None)
    acc = jnp.dot(a_ref[...], w, preferred_element_type=F32)
    o_ref[...] = x_ref[...] + gate_ref[...] * acc


def _ffn_down(a, w_down, x, gate, tm, tn):
    m, f = a.shape
    d = w_down.shape[1]
    tm = min(tm, m)
    tn = _tile(d, tn)
    emit_w = w_down.dtype != BF16
    assert not emit_w or m == tm
    order = lambda fn: (lambda j, i: fn(i, j))
    w_mode = dict(pipeline_mode=pl.Buffered(1)) if m > tm else {}
    a_mode = dict(pipeline_mode=pl.Buffered(1)) if m == tm else {}
    out_specs = [pl.BlockSpec((tm, tn), order(lambda i, j: (i, j)))]
    out_shape = [jax.ShapeDtypeStruct((m, d), F32)]
    if emit_w:
        out_specs.append(pl.BlockSpec((f, tn), order(lambda i, j: (0, j))))
        out_shape.append(jax.ShapeDtypeStruct((f, d), BF16))
    return pl.pallas_call(
        _ffn_down_kernel,
        grid=(d // tn, m // tm),
        in_specs=[pl.BlockSpec((tm, f), order(lambda i, j: (i, 0)), **a_mode),
                  pl.BlockSpec((f, tn), order(lambda i, j: (0, j)), **w_mode),
                  pl.BlockSpec((tm, tn), order(lambda i, j: (i, j))),
                  _gate_spec(gate, m, tm, tn, order)],
        out_specs=out_specs,
        out_shape=out_shape,
        compiler_params=_params("arbitrary", "arbitrary"),
        name="ffn_down",
    )(a, w_down, x, gate)


def _layer(x, mod, cache, w, bias, *, nb, tr):
    b, t, d = x.shape
    m = b * t
    shift_a, scale_a, gate_a, shift_f, scale_f, gate_f = mod
    dh = w["g_q_a"].shape[0]
    nh_a = w["b_f"].shape[0]
    d_a = d_b = w["d_a"]
    nh_b = d_b // dh
    f = w["f"]
    scale = dh ** -0.5
    prompt = cache is None
    tm = 1024
    w16 = {}

    def gates(gt):
        if prompt:
            return gt.reshape(b, 1, d)
        return jnp.broadcast_to(gt[:, None, :], (b, t, d)).reshape(m, d)

    h = _modnorm(x, scale_a, shift_a, w["g_attn"], nb, tr).reshape(m, d)
    keep_p = min(LEFT, t)
    full32 = dict(tm=512, tn=2048, f32_rows=t, seq_len=t) if prompt else dict(tm=tm, tn=512, f32_rows=m, seq_len=m)
    tail32 = dict(tm=tm, tn=1024, f32_rows=keep_p, seq_len=t) if prompt else full32

    def pj(name, n, g, **kw):
        arr, col0 = w[name]
        kw.setdefault("tn", 1024 if prompt else 512)
        outs = _proj(h, arr, col0, n, g, dh=dh, **kw)
        if not prompt:
            *outs, wcopy = outs
            w16[name] = (wcopy, 0)
        return outs

    (q_a,) = pj("w_q_a", d_a, w["g_q_a"] * (scale * LOG2E if prompt else scale), do_rms=True, tm=tm)
    k_a32, k_a = pj("w_k_a", d_a, w["g_k_a"], do_rms=True, **full32)
    v_a32, v_a = pj("w_v_a", d_a, w["g_k_a"], do_rms=False, **full32)
    (q_b,) = pj("w_q_b", d_b, w["g_q_b"] * (scale * LOG2E if prompt else scale), do_rms=True, tm=tm)
    k_b32, k_b = pj("w_k_b", d_b, w["g_k_b"], do_rms=True, **tail32)
    v_b32, v_b = pj("w_v_b", d_b, w["g_k_b"], do_rms=False, **tail32)
    logf = _forget_gate(h, *w["w_f"], w["b_f"], tm).reshape(b, t, nh_a)

    r3 = lambda z: z.reshape(b, t, -1)
    if prompt:
        cum_t = _cumsum_t(logf.transpose(0, 2, 1))
        tq = 256
        o_a = _fox_prompt(r3(q_a), r3(k_a), r3(v_a), cum_t.transpose(0, 2, 1), nh_a, dh, tq, 8)
        o_b = _band_prompt(r3(q_b), r3(k_b), r3(v_b), bias[1], nh_b, dh, 8)
        new_k_b, new_v_b = k_b32.reshape(b, keep_p, nh_b, dh), v_b32.reshape(b, keep_p, nh_b, dh)
    else:
        ck_a, cv_a, clogf_a, ck_b, cv_b = cache
        past = ck_a.shape[1]
        tp = past + CUM_BLK
        lf_all = jnp.concatenate([clogf_a.astype(F32), logf, jnp.zeros((b, tp - past - t, nh_a), F32)], axis=1)
        cum_t = _cumsum_t(lf_all.transpose(0, 2, 1))
        o_a = _fox_sample(r3(q_a), ck_a.reshape(b, past * nh_a, dh), cv_a.reshape(b, past * nh_a, dh),
                          r3(k_a), r3(v_a), cum_t.transpose(0, 2, 1), cum_t, nh_a, dh, 1024)
        keep = ck_b.shape[1]
        o_b, new_k_b, new_v_b = _band_sample(
            r3(q_b), ck_b.reshape(b, keep * nh_b, dh).astype(F32), cv_b.reshape(b, keep * nh_b, dh).astype(F32),
            r3(k_b), r3(v_b), k_b32.reshape(b, t * nh_b, dh), v_b32.reshape(b, t * nh_b, dh), bias[0], nh_b, dh)
        new_k_b = new_k_b.reshape(b, keep, nh_b, dh).astype(ck_b.dtype)
        new_v_b = new_v_b.reshape(b, keep, nh_b, dh).astype(cv_b.dtype)

    x1, *wo16 = _oproj(o_a.reshape(m, d_a), o_b.reshape(m, d_b), w["w_oa"], w["w_ob"], x.reshape(m, d),
                       gates(gate_a), tm, 1024 if prompt else 512)
    h2 = _modnorm(x1.reshape(b, t, d), scale_f, shift_f, w["g_ffn"], nb, tr).reshape(m, d)
    act, *wgu16 = _ffn_up(h2, w["w_g"], w["w_u"], f, 2 * tm, 256)
    y, *wd16 = _ffn_down(act, w["w_down"], x1, gates(gate_f), 512, 1024 if prompt else 256)
    if not prompt:
        w16.update(w_oa=(wo16[0], 0), w_ob=(wo16[1], 0), w_g=(wgu16[0], 0), w_u=(wgu16[1], 0), w_down=wd16[0])
    return (y.reshape(b, t, d), k_a32.reshape(b, t, nh_a, dh), v_a32.reshape(b, t, nh_a, dh), logf,
            new_k_b, new_v_b, w16)


def kernel(x_prompt, x_sample, cache_k_a, cache_v_a, cache_logf_a, cache_k_b, cache_v_b, c_prompt, c_sample,
           w_ada, b_ada, g_attn, g_ffn, w_in, b_f, g_q_a, g_k_a, g_q_b, g_k_b, rel_table, w_o, w_gu, w_down):
    depth = w_ada.shape[0]
    d = x_prompt.shape[-1]
    nb_p, nb_s = c_prompt.shape[0], c_sample.shape[0]
    nh_a = b_f.shape[1]
    n_qkv = w_in.shape[2] - nh_a
    d_a = n_qkv // 6
    f = w_down.shape[1]
    assert 2 * d_a == w_o.shape[1] and w_gu.shape[2] == 2 * f
    y_p, y_s = x_prompt, x_sample
    outs_p, outs_s = [], []
    for l in range(depth):
        w_in_t = jnp.swapaxes(w_in[l], 0, 1)
        small = dict(w_f=(w_in_t, n_qkv), b_f=b_f[l], g_attn=g_attn[l], g_ffn=g_ffn[l], g_q_a=g_q_a[l], g_k_a=g_k_a[l],
                     g_q_b=g_q_b[l], g_k_b=g_k_b[l], d_a=d_a, f=f)
        names = ("w_q_a", "w_k_a", "w_v_a", "w_q_b", "w_k_b", "w_v_b")
        w32 = dict(small, w_oa=(w_o[l], 0), w_ob=(w_o[l], 1), w_g=(w_gu[l], 0), w_u=(w_gu[l], f), w_down=w_down[l],
                   **{name: (w_in_t, i * d_a) for i, name in enumerate(names)})
        c_all = jnp.concatenate([c_prompt, c_sample], axis=0)
        pad = (-c_all.shape[0]) % 16
        mod = _ada(jnp.pad(c_all, ((0, pad), (0, 0))), w_ada[l], b_ada[l])
        mod_p = [mod[:nb_p, i * d:(i + 1) * d] for i in range(6)]
        mod_s = [mod[nb_p:nb_p + nb_s, i * d:(i + 1) * d] for i in range(6)]
        bias = (_band_bias(rel_table[l], x_sample.shape[1]), _band_bias_t(rel_table[l]))
        cache = (cache_k_a[l], cache_v_a[l], cache_logf_a[l], cache_k_b[l], cache_v_b[l])
        y_s, *rest_s, w16 = _layer(y_s, mod_s, cache, w32, bias, nb=nb_s, tr=x_sample.shape[1])
        y_p, *rest_p, _ = _layer(y_p, mod_p, None, dict(small, **w16), bias, nb=1, tr=min(512, x_prompt.shape[1]))
        outs_p.append(rest_p)
        outs_s.append(rest_s)
    stack = lambda outs, i: jnp.stack([o[i] for o in outs])
    return (y_p, y_s, *[stack(outs_p, i) for i in range(5)], *[stack(outs_s, i) for i in range(5)])
```

```python
import functools

import jax
import jax.numpy as jnp
from jax import lax
from jax.experimental import pallas as pl
from jax.experimental.pallas import tpu as pltpu

CHUNK = 64
LEFT_CHUNKS = 8
LEFT = LEFT_CHUNKS * CHUNK
MAX_REL = 128
EPS = 1e-6
NEG = -1e30
LOG2E = 1.4426950408889634

LANES = 128
BAND_TQ = 256
BAND_WIN = LEFT + BAND_TQ
BAND_EXT = BAND_WIN + LEFT
BAND_ROLL = 2048
CUM_BLK = 256
FOX_PAD = 16
VMEM_LIMIT_BYTES = 56 * 1024 * 1024

F32 = jnp.float32
BF16 = jnp.bfloat16


def _params(*semantics):
    return pltpu.CompilerParams(dimension_semantics=semantics, vmem_limit_bytes=VMEM_LIMIT_BYTES)


def _tile(n, want):
    if n <= want:
        return n
    t = (want // LANES) * LANES
    while t >= LANES:
        if n % t == 0:
            return t
        t -= LANES
    raise ValueError(f"no lane-aligned tile for {n}")


def _ada_kernel(c_ref, w_ref, b_ref, o_ref):
    c = c_ref[...]
    a = (c * jax.nn.sigmoid(c)).astype(BF16)
    o_ref[...] = jnp.dot(a, w_ref[...].astype(BF16), preferred_element_type=F32) + b_ref[...]


def _ada(c, w_ada, b_ada):
    m, d = c.shape
    n = w_ada.shape[1]
    tn = _tile(n, 512)
    return pl.pallas_call(
        _ada_kernel,
        grid=(n // tn,),
        in_specs=[pl.BlockSpec((m, d), lambda j: (0, 0)),
                  pl.BlockSpec((d, tn), lambda j: (0, j)),
                  pl.BlockSpec((1, tn), lambda j: (0, j))],
        out_specs=pl.BlockSpec((m, tn), lambda j: (0, j)),
        out_shape=jax.ShapeDtypeStruct((m, n), F32),
        compiler_params=_params("arbitrary"),
        name="ada_mod",
    )(c, w_ada, b_ada.reshape(1, n))


def _modnorm_kernel(x_ref, sc_ref, sh_ref, g_ref, o_ref):
    x = x_ref[...]
    ms = jnp.mean(x * x, axis=-1, keepdims=True)
    y = x * lax.rsqrt(ms + EPS) * g_ref[...]
    o_ref[...] = (y * (1.0 + sc_ref[...]) + sh_ref[...]).astype(o_ref.dtype)


def _modnorm(x, scale, shift, g, nb, tr):
    b, t, d = x.shape
    bs = pl.BlockSpec((nb, 1, d), lambda i, j: (i, 0, 0))
    return pl.pallas_call(
        _modnorm_kernel,
        grid=(b // nb, t // tr),
        in_specs=[pl.BlockSpec((nb, tr, d), lambda i, j: (i, j, 0)), bs, bs,
                  pl.BlockSpec((1, 1, d), lambda i, j: (0, 0, 0))],
        out_specs=pl.BlockSpec((nb, tr, d), lambda i, j: (i, j, 0)),
        out_shape=jax.ShapeDtypeStruct((b, t, d), BF16),
        compiler_params=_params("arbitrary", "arbitrary"),
        name="modnorm",
    )(x, scale.reshape(b, 1, d), shift.reshape(b, 1, d), g.reshape(1, 1, d))


def _bf16_weights(w_ref, copy_ref, transposed=False):
    if copy_ref is None:
        return w_ref[...]
    w = w_ref[...]
    w = (w.T if transposed else w).astype(BF16)
    copy_ref[...] = w
    return w


def _proj_kernel(a_ref, w_ref, g_ref, *o_refs, do_rms, dh, nh, rows32, tiles_per_seq, emit_w):
    if emit_w:
        *o_refs, wcopy_ref = o_refs
    w = _bf16_weights(w_ref, wcopy_ref if emit_w else None, transposed=True)
    acc = jnp.dot(a_ref[...], w, preferred_element_type=F32)
    o16_ref = o_refs[-1]
    tm, tn = acc.shape
    hpt = tn // dh
    g = g_ref[...]
    ys = []
    for hh in range(hpt):
        y = acc[:, hh * dh:(hh + 1) * dh]
        if do_rms:
            ms = jnp.mean(y * y, axis=-1, keepdims=True)
            y = y * lax.rsqrt(ms + EPS) * g
        o16_ref[:, hh * dh:(hh + 1) * dh] = y.astype(o16_ref.dtype)
        ys.append(y)
    if rows32:
        o32_ref = o_refs[0]
        h0 = pl.program_id(1) * hpt

        def store32():
            for hh in range(hpt):
                o32_ref[pl.ds(h0 + hh, rows32, stride=nh), :] = ys[hh][tm - rows32:, :]

        if rows32 == tm:
            store32()
        else:
            pl.when(pl.program_id(0) % tiles_per_seq == tiles_per_seq - 1)(store32)


def _proj(a, w, col0, n, g, *, do_rms, dh, tm, tn, f32_rows=0, seq_len=None):
    m, k = a.shape
    tm = min(tm, m)
    tn = _tile(n, tn)
    nh = n // dh
    j0 = col0 // tn
    emit_w = w.dtype != BF16
    assert col0 % tn == 0 and m % tm == 0 and (not emit_w or m == tm)
    col_outer = not f32_rows and not emit_w and n > tn and m > tm
    order = (lambda fn: (lambda j, i: fn(i, j))) if col_outer else (lambda fn: fn)
    w_mode = dict(pipeline_mode=pl.Buffered(1)) if n == tn or col_outer else {}
    out_specs = [pl.BlockSpec((tm, tn), order(lambda i, j: (i, j)))]
    out_shape = [jax.ShapeDtypeStruct((m, n), BF16)]
    rows32, tps = 0, 1
    if f32_rows:
        tps = seq_len // tm
        assert seq_len % tm == 0
        if f32_rows == seq_len:
            rows32 = tm
            out_specs.insert(0, pl.BlockSpec((tm * nh, dh), lambda i, j: (i, 0)))
            out_shape.insert(0, jax.ShapeDtypeStruct((m * nh, dh), F32))
        else:
            rows32 = f32_rows
            assert rows32 <= tm
            out_specs.insert(0, pl.BlockSpec((rows32 * nh, dh), lambda i, j: (i // tps, 0)))
            out_shape.insert(0, jax.ShapeDtypeStruct((m // seq_len * rows32 * nh, dh), F32))
    if emit_w:
        out_specs.append(pl.BlockSpec((k, tn), lambda i, j: (0, j)))
        out_shape.append(jax.ShapeDtypeStruct((k, n), BF16))
    return pl.pallas_call(
        functools.partial(_proj_kernel, do_rms=do_rms, dh=dh, nh=nh, rows32=rows32, tiles_per_seq=tps,
                          emit_w=emit_w),
        grid=(n // tn, m // tm) if col_outer else (m // tm, n // tn),
        in_specs=[pl.BlockSpec((tm, k), order(lambda i, j: (i, 0))),
                  pl.BlockSpec((tn, k), lambda i, j: (j0 + j, 0)) if emit_w else
                  pl.BlockSpec((k, tn), order(lambda i, j: (0, j0 + j)), **w_mode),
                  pl.BlockSpec((1, dh), order(lambda i, j: (0, 0)))],
        out_specs=out_specs,
        out_shape=out_shape,
        compiler_params=_params("arbitrary", "arbitrary"),
        name="proj",
    )(a, w, g.reshape(1, dh).astype(F32))


def _gate_kernel(a_ref, w_ref, b_ref, o_ref):
    z = _qk(a_ref[...], w_ref[...].astype(BF16)) + b_ref[...]
    o_ref[...] = jnp.minimum(z, 0.0) - jnp.log1p(jnp.exp(-jnp.abs(z)))


def _forget_gate(a, w_t, row0, b_f, tm):
    m, k = a.shape
    nh = b_f.shape[0]
    tm = min(tm, m)
    assert row0 % nh == 0
    return pl.pallas_call(
        _gate_kernel,
        grid=(m // tm,),
        in_specs=[pl.BlockSpec((tm, k), lambda i: (i, 0)),
                  pl.BlockSpec((nh, k), lambda i: (row0 // nh, 0)),
                  pl.BlockSpec((1, nh), lambda i: (0, 0))],
        out_specs=pl.BlockSpec((tm, nh), lambda i: (i, 0)),
        out_shape=jax.ShapeDtypeStruct((m, nh), F32),
        compiler_params=_params("arbitrary"),
        name="forget_gate",
    )(a, w_t, b_f.reshape(1, nh).astype(F32))


def _cumsum_kernel(x_ref, o_ref):
    h, tp = x_ref.shape[1], x_ref.shape[2]
    r = lax.broadcasted_iota(jnp.int32, (CUM_BLK, CUM_BLK), 0)
    c = lax.broadcasted_iota(jnp.int32, (CUM_BLK, CUM_BLK), 1)
    tri = (r <= c).astype(BF16)
    carry = jnp.zeros((h, 1), F32)
    for blk in range(tp // CUM_BLK):
        x = x_ref[0, :, blk * CUM_BLK:(blk + 1) * CUM_BLK]
        hi = x.astype(BF16)
        r1 = x - hi.astype(F32)
        mid = r1.astype(BF16)
        lo = (r1 - mid.astype(F32)).astype(BF16)
        cs = (jnp.dot(hi, tri, preferred_element_type=F32)
              + jnp.dot(mid, tri, preferred_element_type=F32)
              + jnp.dot(lo, tri, preferred_element_type=F32)) + carry
        o_ref[0, :, blk * CUM_BLK:(blk + 1) * CUM_BLK] = cs
        carry = cs[:, CUM_BLK - 1:CUM_BLK]


def _cumsum_t(logf_t):
    b, h, tp = logf_t.shape
    return pl.pallas_call(
        _cumsum_kernel,
        grid=(b,),
        in_specs=[pl.BlockSpec((1, h, tp), lambda i: (i, 0, 0))],
        out_specs=pl.BlockSpec((1, h, tp), lambda i: (i, 0, 0)),
        out_shape=jax.ShapeDtypeStruct((b, h, tp), F32),
        compiler_params=_params("arbitrary"),
        name="logf_cumsum",
    )(logf_t)


def _qk(q, k):
    return lax.dot_general(q, k, (((1,), (1,)), ((), ())), preferred_element_type=F32)


def _fox_kernel(q_ref, k_ref, v_ref, cc_ref, o_ref, vt_sc, ka_sc, *, tq, hb, dh, nh):
    hg = pl.program_id(1)
    i = pl.program_id(2)
    nk = vt_sc.shape[1]
    heads = range(hb)
    hcols = [slice(hh * dh, (hh + 1) * dh) for hh in heads]

    @pl.when(i == 0)
    def _():
        ones = jnp.ones((FOX_PAD, tq), BF16)
        r = lax.broadcasted_iota(jnp.int32, (LANES, dh), 0)
        lane = lax.broadcasted_iota(jnp.int32, (LANES, dh), 1)
        sels = [((r % nh == hg * hb + hh) & (r // nh == lane) & (r < 3 * nh)).astype(BF16) for hh in heads]
        for kb in range(nk):
            rows = slice(kb * tq, (kb + 1) * tq)
            ck = cc_ref[0, rows, :] * (-LOG2E)
            hi = ck.astype(BF16).astype(F32)
            mid = (ck - hi).astype(BF16).astype(F32)
            lo = ck - hi - mid
            packed = (hi + pltpu.roll(mid, nh, axis=1) + pltpu.roll(lo, 2 * nh, axis=1)).astype(BF16)
            for hh in heads:
                vt_sc[hh, kb, :dh, :] = v_ref[0, rows, hcols[hh]].T
                vt_sc[hh, kb, dh:, :] = ones
                ka_sc[hh, rows, :dh] = k_ref[0, rows, hcols[hh]]
                ka_sc[hh, rows, dh:] = jnp.dot(packed, sels[hh], preferred_element_type=F32).astype(BF16)

    ones_rows = (lax.broadcasted_iota(jnp.int32, (dh, tq), 0) < 3).astype(BF16)
    qts = [jnp.concatenate([q_ref[0, :, cols].T, ones_rows], axis=0) for cols in hcols]

    def scores(kb):
        off = pl.multiple_of(kb * tq, tq)
        out = []
        for hh in heads:
            out.append(jnp.dot(ka_sc[hh, pl.ds(off, tq), :], qts[hh], preferred_element_type=F32))
        return out

    def step(kb, state, mask=None):
        stats = []
        for s, (m, _) in zip(scores(kb), state):
            if mask is not None:
                s = jnp.where(mask, s, NEG)
            m_new = jnp.maximum(m, jnp.max(s, axis=0, keepdims=True))
            stats.append((m_new, jnp.exp2(m - m_new), jnp.exp2(s - m_new).astype(BF16)))
        return [(m_new, alpha * acc + jnp.dot(vt_sc[hh, kb], p, preferred_element_type=F32))
                for hh, ((m_new, alpha, p), (_, acc)) in enumerate(zip(stats, state))]

    state = [(jnp.full((1, tq), NEG, F32), jnp.zeros((dh + FOX_PAD, tq), F32)) for _ in heads]
    state = lax.fori_loop(0, i, step, state)
    key = lax.broadcasted_iota(jnp.int32, (tq, tq), 0)
    qry = lax.broadcasted_iota(jnp.int32, (tq, tq), 1)
    state = step(i, state, mask=key <= qry)
    for cols, (_, acc) in zip(hcols, state):
        o_ref[0, :, cols] = (acc[:dh] / acc[dh:dh + 1]).T.astype(o_ref.dtype)


def _fox_prompt(q, k, v, cum_col, nh, dh, tq, hb):
    b, t, _ = q.shape
    hb = min(hb, nh)
    assert t % tq == 0 and tq % LANES == 0 and 3 * nh <= LANES and dh == LANES
    cum_col = jnp.pad(cum_col, ((0, 0), (0, 0), (0, LANES - nh)))
    qspec = pl.BlockSpec((1, tq, hb * dh), lambda bb, h, i: (bb, i, h))
    kspec = pl.BlockSpec((1, t, hb * dh), lambda bb, h, i: (bb, 0, h))
    return pl.pallas_call(
        functools.partial(_fox_kernel, tq=tq, hb=hb, dh=dh, nh=nh),
        grid=(b, nh // hb, t // tq),
        in_specs=[qspec, kspec, kspec, pl.BlockSpec((1, t, LANES), lambda bb, h, i: (bb, 0, 0))],
        out_specs=qspec,
        out_shape=jax.ShapeDtypeStruct((b, t, nh * dh), BF16),
        scratch_shapes=[pltpu.VMEM((hb, t // tq, dh + FOX_PAD, tq), BF16), pltpu.VMEM((hb, t, 2 * dh), BF16)],
        compiler_params=_params("arbitrary", "arbitrary", "arbitrary"),
        name="fox_prompt",
    )(q, k, v, cum_col)


def _head_major(ref, n, nh):
    return pltpu.einshape("tjd->jtd", ref[0].astype(BF16).reshape(n, nh, ref.shape[2]))


def _fox_s_kernel(q_ref, kc_ref, vc_ref, kn_ref, vn_ref, cc_ref, crc_ref, crn_ref, o_ref, m_sc, l_sc, acc_sc,
                  *, nh, dh, tc, s_len):
    c = pl.program_id(1)
    cc = cc_ref[0][:s_len]
    r = lax.broadcasted_iota(jnp.int32, (s_len, s_len), 0)
    col = lax.broadcasted_iota(jnp.int32, (s_len, s_len), 1)

    @pl.when(c == 0)
    def _():
        hcols = [slice(h * dh, (h + 1) * dh) for h in range(nh)]
        ss = [_qk(q_ref[0, :, cols], kn_ref[0, :, cols]) + (cc[:, h:h + 1] - crn_ref[0, h:h + 1, :s_len])
              for h, cols in enumerate(hcols)]
        ps = []
        for h, s in enumerate(ss):
            s = jnp.where(col <= r, s, NEG)
            m = jnp.max(s, axis=1, keepdims=True)
            p = jnp.exp(s - m)
            m_sc[h] = m
            l_sc[h] = jnp.sum(p, axis=1, keepdims=True)
            ps.append(p.astype(BF16))
        for h, p in enumerate(ps):
            acc_sc[h] = jnp.dot(p, vn_ref[0, :, hcols[h]], preferred_element_type=F32)

    kc = _head_major(kc_ref, tc, nh)
    vc = _head_major(vc_ref, tc, nh)
    ss = [_qk(q_ref[0, :, h * dh:(h + 1) * dh], kc[h])
          + (cc[:, h:h + 1] - crc_ref[0, h:h + 1, :]) for h in range(nh)]
    ps = []
    for h, s in enumerate(ss):
        m_old = m_sc[h]
        m_new = jnp.maximum(m_old, jnp.max(s, axis=1, keepdims=True))
        alpha = jnp.exp(m_old - m_new)
        p = jnp.exp(s - m_new)
        m_sc[h] = m_new
        l_sc[h] = alpha * l_sc[h] + jnp.sum(p, axis=1, keepdims=True)
        ps.append((alpha, p.astype(BF16)))
    for h, (alpha, p) in enumerate(ps):
        acc_sc[h] = alpha * acc_sc[h] + jnp.dot(p, vc[h], preferred_element_type=F32)

    @pl.when(c == pl.num_programs(1) - 1)
    def _():
        for h in range(nh):
            o_ref[0, :, h * dh:(h + 1) * dh] = (acc_sc[h] / l_sc[h]).astype(o_ref.dtype)


def _fox_sample(q, kc, vc, kn, vn, cum_col, cum_t, nh, dh, tc):
    b, s_len, _ = q.shape
    past = kc.shape[1] // nh
    tp = cum_col.shape[1]
    tc = min(tc, past)
    assert past % CUM_BLK == 0 and tp - past == CUM_BLK and past % tc == 0 and s_len <= CUM_BLK
    new = pl.BlockSpec((1, s_len, nh * dh), lambda bb, c: (bb, 0, 0))
    cache = pl.BlockSpec((1, tc * nh, dh), lambda bb, c: (bb, c, 0))
    return pl.pallas_call(
        functools.partial(_fox_s_kernel, nh=nh, dh=dh, tc=tc, s_len=s_len),
        grid=(b, past // tc),
        in_specs=[new, cache, cache, new, new,
                  pl.BlockSpec((1, CUM_BLK, nh), lambda bb, c: (bb, past // CUM_BLK, 0)),
                  pl.BlockSpec((1, nh, tc), lambda bb, c: (bb, 0, c)),
                  pl.BlockSpec((1, nh, CUM_BLK), lambda bb, c: (bb, 0, past // CUM_BLK))],
        out_specs=new,
        out_shape=jax.ShapeDtypeStruct((b, s_len, nh * dh), BF16),
        scratch_shapes=[pltpu.VMEM((nh, s_len, 1), F32), pltpu.VMEM((nh, s_len, 1), F32),
                        pltpu.VMEM((nh, s_len, dh), F32)],
        compiler_params=_params("arbitrary", "arbitrary"),
        name="fox_sample",
    )(q, kc, vc, kn, vn, cum_col, cum_t, cum_t)


def _band_bias_kernel(gf_ref, o_ref):
    f8 = jnp.broadcast_to(gf_ref[0], (8, BAND_ROLL))
    sub = lax.broadcasted_iota(jnp.int32, (8, BAND_ROLL), 0)
    base = f8
    for bb in range(1, 8):
        base = jnp.where(sub == bb, pltpu.roll(f8, bb, axis=1), base)
    for a in range(o_ref.shape[1] // 8):
        rows = base if a == 0 else pltpu.roll(base, 8 * a, axis=1)
        o_ref[0, 8 * a:8 * a + 8, :] = rows[:, :BAND_WIN]


def _band_bias(rel_table, rows):
    nh = rel_table.shape[0]
    assert rows % 8 == 0 and rows <= CHUNK
    u = jnp.arange(BAND_ROLL)
    v = jnp.where(u < BAND_ROLL - BAND_TQ, u, u - BAND_ROLL)
    idx = jnp.clip(LEFT - v, -MAX_REL, MAX_REL) + MAX_REL
    gf = jnp.take(rel_table.astype(F32), idx, axis=1).reshape(nh, 1, BAND_ROLL)
    return pl.pallas_call(
        _band_bias_kernel,
        grid=(nh,),
        in_specs=[pl.BlockSpec((1, 1, BAND_ROLL), lambda h: (h, 0, 0))],
        out_specs=pl.BlockSpec((1, rows, BAND_WIN), lambda h: (h, 0, 0)),
        out_shape=jax.ShapeDtypeStruct((nh, rows, BAND_WIN), F32),
        compiler_params=_params("arbitrary"),
        name="band_bias",
    )(gf)


def _band_bias_t_kernel(gf_ref, o_ref):
    f8 = jnp.broadcast_to(gf_ref[0], (8, BAND_ROLL))
    sub = lax.broadcasted_iota(jnp.int32, (8, BAND_ROLL), 0)
    base = f8
    for bb in range(1, 8):
        base = jnp.where(sub == bb, pltpu.roll(f8, bb, axis=1), base)
    key8 = lax.broadcasted_iota(jnp.int32, (8, BAND_TQ), 0)
    qry_chunk = lax.broadcasted_iota(jnp.int32, (8, BAND_TQ), 1) // CHUNK
    for a in range(BAND_EXT // 8):
        rows = base if a == 0 else pltpu.roll(base, 8 * a, axis=1)
        cd = qry_chunk - ((key8 + 8 * a) // CHUNK - LEFT_CHUNKS)
        rows = jnp.where((cd >= 0) & (cd <= LEFT_CHUNKS), rows[:, :BAND_TQ], NEG)
        for w in range(3):
            r0 = 8 * a - (LEFT - w * BAND_TQ)
            if 0 <= r0 < BAND_WIN:
                o_ref[0, w, r0:r0 + 8, :] = rows


def _band_bias_t(rel_table):
    nh = rel_table.shape[0]
    u = jnp.arange(BAND_ROLL)
    v = jnp.where(u < 2 * BAND_TQ, u, u - BAND_ROLL)
    idx = jnp.clip(v + LEFT, -MAX_REL, MAX_REL) + MAX_REL
    gf = (jnp.take(rel_table.astype(F32), idx, axis=1) * LOG2E).reshape(nh, 1, BAND_ROLL)
    return pl.pallas_call(
        _band_bias_t_kernel,
        grid=(nh,),
        in_specs=[pl.BlockSpec((1, 1, BAND_ROLL), lambda h: (h, 0, 0))],
        out_specs=pl.BlockSpec((1, 3, BAND_WIN, BAND_TQ), lambda h: (h, 0, 0, 0)),
        out_shape=jax.ShapeDtypeStruct((nh, 3, BAND_WIN, BAND_TQ), F32),
        compiler_params=_params("arbitrary"),
        name="band_bias_t",
    )(gf)


def _softmax_pv(score_parts, value_parts):
    probs = []
    for parts in score_parts:
        m = functools.reduce(jnp.maximum, [jnp.max(s, axis=1, keepdims=True) for s in parts])
        ps = [jnp.exp(s - m) for s in parts]
        l = functools.reduce(jnp.add, [jnp.sum(p, axis=1, keepdims=True) for p in ps])
        probs.append(([p.astype(BF16) for p in ps], l))
    outs = []
    for (ps, l), vals in zip(probs, value_parts):
        o = functools.reduce(jnp.add, [jnp.dot(p, v(), preferred_element_type=F32) for p, v in zip(ps, vals)])
        outs.append(o / l)
    return outs


def _band_kernel(q_ref, k_ref, v_ref, e_ref, o_ref, vt_sc, *, hb, dh):
    g = pl.program_id(2)
    nk = vt_sc.shape[1]
    nwin = BAND_WIN // BAND_TQ
    hcols = [slice(hh * dh, (hh + 1) * dh) for hh in range(hb)]

    @pl.when(g == 0)
    def _():
        ones = jnp.ones((FOX_PAD, BAND_TQ), BF16)
        for kb in range(nk):
            for hh, cols in enumerate(hcols):
                vt_sc[hh, kb, :dh, :] = v_ref[0, kb * BAND_TQ:(kb + 1) * BAND_TQ, cols].T
                vt_sc[hh, kb, dh:, :] = ones

    wb = jnp.maximum(g - LEFT // BAND_TQ, 0)
    ws = pl.multiple_of(wb * BAND_TQ, BAND_TQ)
    w = jnp.minimum(g, LEFT // BAND_TQ)
    ss = [jnp.dot(k_ref[0, pl.ds(ws, BAND_WIN), cols], q_ref[0, :, cols].T, preferred_element_type=F32)
          + e_ref[hh, w] for hh, cols in enumerate(hcols)]
    ps = [jnp.exp2(s - jnp.max(s, axis=0, keepdims=True)).astype(BF16) for s in ss]
    for hh, (cols, p) in enumerate(zip(hcols, ps)):
        acc = functools.reduce(jnp.add, [
            jnp.dot(vt_sc[hh, wb + j], p[j * BAND_TQ:(j + 1) * BAND_TQ], preferred_element_type=F32)
            for j in range(nwin)])
        o_ref[0, :, cols] = (acc[:dh] / acc[dh:dh + 1]).T.astype(o_ref.dtype)


def _band_prompt(q, k, v, bias_t, nh, dh, hb):
    b, t, _ = q.shape
    assert t % BAND_TQ == 0 and t >= BAND_WIN
    hb = min(hb, nh)
    qspec = pl.BlockSpec((1, BAND_TQ, hb * dh), lambda h, bb, g: (bb, g, h))
    kspec = pl.BlockSpec((1, t, hb * dh), lambda h, bb, g: (bb, 0, h))
    return pl.pallas_call(
        functools.partial(_band_kernel, hb=hb, dh=dh),
        grid=(nh // hb, b, t // BAND_TQ),
        in_specs=[qspec, kspec, kspec,
                  pl.BlockSpec((hb, 3, BAND_WIN, BAND_TQ), lambda h, bb, g: (h, 0, 0, 0),
                               pipeline_mode=pl.Buffered(1))],
        out_specs=qspec,
        out_shape=jax.ShapeDtypeStruct((b, t, nh * dh), BF16),
        scratch_shapes=[pltpu.VMEM((hb, t // BAND_TQ, dh + FOX_PAD, BAND_TQ), BF16)],
        compiler_params=_params("arbitrary", "arbitrary", "arbitrary"),
        name="band_prompt",
    )(q, k, v, bias_t)


def _band_s_kernel(q_ref, kc_ref, vc_ref, kn_ref, vn_ref, kn32_ref, vn32_ref, e_ref, o_ref, nk_ref, nv_ref,
                   *, nh, dh, keep, s_len):
    for new_ref, cache_ref, n32_ref in ((nk_ref, kc_ref, kn32_ref), (nv_ref, vc_ref, vn32_ref)):
        new_ref[0, :(keep - s_len) * nh, :] = cache_ref[0, s_len * nh:, :]
        new_ref[0, (keep - s_len) * nh:, :] = n32_ref[0]
    hcols = [slice(h * dh, (h + 1) * dh) for h in range(nh)]
    scores, values = [], []
    kc = _head_major(kc_ref, keep, nh)
    vc = _head_major(vc_ref, keep, nh)
    for h, cols in enumerate(hcols):
        q = q_ref[0, :, cols]
        e = e_ref[h]
        scores.append([_qk(q, kc[h]) + e[:, :keep],
                       _qk(q, kn_ref[0, :, cols]) + e[:, keep:keep + s_len]])
        values.append([functools.partial(lambda h: vc[h], h),
                       functools.partial(lambda cols: vn_ref[0, :, cols], cols)])
    for cols, o in zip(hcols, _softmax_pv(scores, values)):
        o_ref[0, :, cols] = o.astype(o_ref.dtype)


def _band_sample(q, kc, vc, kn, vn, kn32, vn32, bias, nh, dh):
    b, s_len, _ = q.shape
    keep = kc.shape[1] // nh
    assert keep == LEFT and s_len <= CHUNK and s_len % 8 == 0
    new = pl.BlockSpec((1, s_len, nh * dh), lambda bb: (bb, 0, 0))
    new32 = pl.BlockSpec((1, s_len * nh, dh), lambda bb: (bb, 0, 0))
    cache = pl.BlockSpec((1, keep * nh, dh), lambda bb: (bb, 0, 0))
    return pl.pallas_call(
        functools.partial(_band_s_kernel, nh=nh, dh=dh, keep=keep, s_len=s_len),
        grid=(b,),
        in_specs=[new, cache, cache, new, new, new32, new32,
                  pl.BlockSpec((nh, s_len, BAND_WIN), lambda bb: (0, 0, 0))],
        out_specs=[new, cache, cache],
        out_shape=[jax.ShapeDtypeStruct((b, s_len, nh * dh), BF16)] + [jax.ShapeDtypeStruct(kc.shape, F32)] * 2,
        compiler_params=_params("arbitrary"),
        name="band_sample",
    )(q, kc, vc, kn, vn, kn32, vn32, bias)


def _oproj_kernel(oa_ref, ob_ref, wa_ref, wb_ref, x_ref, gate_ref, o_ref, *wcopy_refs):
    ca, cb = wcopy_refs if wcopy_refs else (None, None)
    acc = (jnp.dot(oa_ref[...], _bf16_weights(wa_ref, ca), preferred_element_type=F32)
           + jnp.dot(ob_ref[...], _bf16_weights(wb_ref, cb), preferred_element_type=F32))
    o_ref[...] = x_ref[...] + gate_ref[...] * acc


def _gate_spec(gate, m, tm, tn, order=lambda fn: fn):
    if gate.ndim == 2:
        return pl.BlockSpec((tm, tn), order(lambda i, j: (i, j)))
    tiles_per_seq = m // gate.shape[0] // tm
    return pl.BlockSpec((None, 1, tn), order(lambda i, j: (i // tiles_per_seq, 0, j)))


def _oproj(oa, ob, wa, wb, x, gate, tm, tn):
    m, da = oa.shape
    db = ob.shape[1]
    (wa, ra), (wb, rb) = wa, wb
    d = wa.shape[1]
    tm = min(tm, m)
    tn = _tile(d, tn)
    emit_w = wa.dtype != BF16
    assert da == db and (not emit_w or m == tm)
    out_specs = [pl.BlockSpec((tm, tn), lambda i, j: (i, j))]
    out_shape = [jax.ShapeDtypeStruct((m, d), F32)]
    if emit_w:
        out_specs += [pl.BlockSpec((da, tn), lambda i, j: (0, j))] * 2
        out_shape += [jax.ShapeDtypeStruct((da, d), BF16)] * 2
    return pl.pallas_call(
        _oproj_kernel,
        grid=(m // tm, d // tn),
        in_specs=[pl.BlockSpec((tm, da), lambda i, j: (i, 0)),
                  pl.BlockSpec((tm, db), lambda i, j: (i, 0)),
                  pl.BlockSpec((da, tn), lambda i, j: (ra, j)),
                  pl.BlockSpec((db, tn), lambda i, j: (rb, j)),
                  pl.BlockSpec((tm, tn), lambda i, j: (i, j)),
                  _gate_spec(gate, m, tm, tn)],
        out_specs=out_specs,
        out_shape=out_shape,
        compiler_params=_params("arbitrary", "arbitrary"),
        name="out_proj",
    )(oa, ob, wa, wb, x, gate)


def _ffn_up_kernel(a_ref, wg_ref, wu_ref, o_ref, *wcopy_refs):
    cg, cu = wcopy_refs if wcopy_refs else (None, None)
    a = a_ref[...]
    g = jnp.dot(a, _bf16_weights(wg_ref, cg), preferred_element_type=F32)
    u = jnp.dot(a, _bf16_weights(wu_ref, cu), preferred_element_type=F32)
    o_ref[...] = (g * jax.nn.sigmoid(g) * u).astype(o_ref.dtype)


def _ffn_up(a, wg, wu, f, tm, tn):
    m, k = a.shape
    (wg, cg), (wu, cu) = wg, wu
    tm = min(tm, m)
    tn = _tile(f, tn)
    jg, ju = cg // tn, cu // tn
    emit_w = wg.dtype != BF16
    assert cg % tn == 0 and cu % tn == 0 and (not emit_w or m == tm)
    out_specs = [pl.BlockSpec((tm, tn), lambda i, j: (i, j))]
    out_shape = [jax.ShapeDtypeStruct((m, f), BF16)]
    if emit_w:
        out_specs += [pl.BlockSpec((k, tn), lambda i, j: (0, j))] * 2
        out_shape += [jax.ShapeDtypeStruct((k, f), BF16)] * 2
    return pl.pallas_call(
        _ffn_up_kernel,
        grid=(m // tm, f // tn),
        in_specs=[pl.BlockSpec((tm, k), lambda i, j: (i, 0)),
                  pl.BlockSpec((k, tn), lambda i, j: (0, jg + j)),
                  pl.BlockSpec((k, tn), lambda i, j: (0, ju + j))],
        out_specs=out_specs,
        out_shape=out_shape,
        compiler_params=_params("arbitrary", "arbitrary"),
        name="ffn_up",
    )(a, wg, wu)


def _ffn_down_kernel(a_ref, w_ref, x_ref, gate_ref, o_ref, *wcopy_refs):
    w = _bf16_weights(w_ref, wcopy_refs[0] if wcopy_refs else None)
    acc = jnp.dot(a_ref[...], w, preferred_element_type=F32)
    o_ref[...] = x_ref[...] + gate_ref[...] * acc


def _ffn_down(a, w_down, x, gate, tm, tn):
    m, f = a.shape
    d = w_down.shape[1]
    tm = min(tm, m)
    tn = _tile(d, tn)
    emit_w = w_down.dtype != BF16
    assert not emit_w or m == tm
    order = lambda fn: (lambda j, i: fn(i, j))
    w_mode = dict(pipeline_mode=pl.Buffered(1)) if m > tm else {}
    a_mode = dict(pipeline_mode=pl.Buffered(1)) if m == tm else {}
    out_specs = [pl.BlockSpec((tm, tn), order(lambda i, j: (i, j)))]
    out_shape = [jax.ShapeDtypeStruct((m, d), F32)]
    if emit_w:
        out_specs.append(pl.BlockSpec((f, tn), order(lambda i, j: (0, j))))
        out_shape.append(jax.ShapeDtypeStruct((f, d), BF16))
    return pl.pallas_call(
        _ffn_down_kernel,
        grid=(d // tn, m // tm),
        in_specs=[pl.BlockSpec((tm, f), order(lambda i, j: (i, 0)), **a_mode),
                  pl.BlockSpec((f, tn), order(lambda i, j: (0, j)), **w_mode),
                  pl.BlockSpec((tm, tn), order(lambda i, j: (i, j))),
                  _gate_spec(gate, m, tm, tn, order)],
        out_specs=out_specs,
        out_shape=out_shape,
        compiler_params=_params("arbitrary", "arbitrary"),
        name="ffn_down",
    )(a, w_down, x, gate)


def _layer(x, mod, cache, w, bias, *, nb, tr):
    b, t, d = x.shape
    m = b * t
    shift_a, scale_a, gate_a, shift_f, scale_f, gate_f = mod
    dh = w["g_q_a"].shape[0]
    nh_a = w["b_f"].shape[0]
    d_a = d_b = w["d_a"]
    nh_b = d_b // dh
    f = w["f"]
    scale = dh ** -0.5
    prompt = cache is None
    tm = 1024
    w16 = {}

    def gates(gt):
        if prompt:
            return gt.reshape(b, 1, d)
        return jnp.broadcast_to(gt[:, None, :], (b, t, d)).reshape(m, d)

    h = _modnorm(x, scale_a, shift_a, w["g_attn"], nb, tr).reshape(m, d)
    keep_p = min(LEFT, t)
    full32 = dict(tm=512, tn=2048, f32_rows=t, seq_len=t) if prompt else dict(tm=tm, tn=512, f32_rows=m, seq_len=m)
    tail32 = dict(tm=tm, tn=1024, f32_rows=keep_p, seq_len=t) if prompt else full32

    def pj(name, n, g, **kw):
        arr, col0 = w[name]
        kw.setdefault("tn", 1024 if prompt else 512)
        outs = _proj(h, arr, col0, n, g, dh=dh, **kw)
        if not prompt:
            *outs, wcopy = outs
            w16[name] = (wcopy, 0)
        return outs

    (q_a,) = pj("w_q_a", d_a, w["g_q_a"] * (scale * LOG2E if prompt else scale), do_rms=True, tm=tm)
    k_a32, k_a = pj("w_k_a", d_a, w["g_k_a"], do_rms=True, **full32)
    v_a32, v_a = pj("w_v_a", d_a, w["g_k_a"], do_rms=False, **full32)
    (q_b,) = pj("w_q_b", d_b, w["g_q_b"] * (scale * LOG2E if prompt else scale), do_rms=True, tm=tm)
    k_b32, k_b = pj("w_k_b", d_b, w["g_k_b"], do_rms=True, **tail32)
    v_b32, v_b = pj("w_v_b", d_b, w["g_k_b"], do_rms=False, **tail32)
    logf = _forget_gate(h, *w["w_f"], w["b_f"], tm).reshape(b, t, nh_a)

    r3 = lambda z: z.reshape(b, t, -1)
    if prompt:
        cum_t = _cumsum_t(logf.transpose(0, 2, 1))
        tq = 256
        o_a = _fox_prompt(r3(q_a), r3(k_a), r3(v_a), cum_t.transpose(0, 2, 1), nh_a, dh, tq, 8)
        o_b = _band_prompt(r3(q_b), r3(k_b), r3(v_b), bias[1], nh_b, dh, 8)
        new_k_b, new_v_b = k_b32.reshape(b, keep_p, nh_b, dh), v_b32.reshape(b, keep_p, nh_b, dh)
    else:
        ck_a, cv_a, clogf_a, ck_b, cv_b = cache
        past = ck_a.shape[1]
        tp = past + CUM_BLK
        lf_all = jnp.concatenate([clogf_a.astype(F32), logf, jnp.zeros((b, tp - past - t, nh_a), F32)], axis=1)
        cum_t = _cumsum_t(lf_all.transpose(0, 2, 1))
        o_a = _fox_sample(r3(q_a), ck_a.reshape(b, past * nh_a, dh), cv_a.reshape(b, past * nh_a, dh),
                          r3(k_a), r3(v_a), cum_t.transpose(0, 2, 1), cum_t, nh_a, dh, 1024)
        keep = ck_b.shape[1]
        o_b, new_k_b, new_v_b = _band_sample(
            r3(q_b), ck_b.reshape(b, keep * nh_b, dh).astype(F32), cv_b.reshape(b, keep * nh_b, dh).astype(F32),
            r3(k_b), r3(v_b), k_b32.reshape(b, t * nh_b, dh), v_b32.reshape(b, t * nh_b, dh), bias[0], nh_b, dh)
        new_k_b = new_k_b.reshape(b, keep, nh_b, dh).astype(ck_b.dtype)
        new_v_b = new_v_b.reshape(b, keep, nh_b, dh).astype(cv_b.dtype)

    x1, *wo16 = _oproj(o_a.reshape(m, d_a), o_b.reshape(m, d_b), w["w_oa"], w["w_ob"], x.reshape(m, d),
                       gates(gate_a), tm, 1024 if prompt else 512)
    h2 = _modnorm(x1.reshape(b, t, d), scale_f, shift_f, w["g_ffn"], nb, tr).reshape(m, d)
    act, *wgu16 = _ffn_up(h2, w["w_g"], w["w_u"], f, 2 * tm, 256)
    y, *wd16 = _ffn_down(act, w["w_down"], x1, gates(gate_f), 512, 1024 if prompt else 256)
    if not prompt:
        w16.update(w_oa=(wo16[0], 0), w_ob=(wo16[1], 0), w_g=(wgu16[0], 0), w_u=(wgu16[1], 0), w_down=wd16[0])
    return (y.reshape(b, t, d), k_a32.reshape(b, t, nh_a, dh), v_a32.reshape(b, t, nh_a, dh), logf,
            new_k_b, new_v_b, w16)


def kernel(x_prompt, x_sample, cache_k_a, cache_v_a, cache_logf_a, cache_k_b, cache_v_b, c_prompt, c_sample,
           w_ada, b_ada, g_attn, g_ffn, w_in, b_f, g_q_a, g_k_a, g_q_b, g_k_b, rel_table, w_o, w_gu, w_down):
    depth = w_ada.shape[0]
    d = x_prompt.shape[-1]
    nb_p, nb_s = c_prompt.shape[0], c_sample.shape[0]
    nh_a = b_f.shape[1]
    n_qkv = w_in.shape[2] - nh_a
    d_a = n_qkv // 6
    f = w_down.shape[1]
    assert 2 * d_a == w_o.shape[1] and w_gu.shape[2] == 2 * f
    y_p, y_s = x_prompt, x_sample
    outs_p, outs_s = [], []
    for l in range(depth):
        w_in_t = jnp.swapaxes(w_in[l], 0, 1)
        small = dict(w_f=(w_in_t, n_qkv), b_f=b_f[l], g_attn=g_attn[l], g_ffn=g_ffn[l], g_q_a=g_q_a[l], g_k_a=g_k_a[l],
                     g_q_b=g_q_b[l], g_k_b=g_k_b[l], d_a=d_a, f=f)
        names = ("w_q_a", "w_k_a", "w_v_a", "w_q_b", "w_k_b", "w_v_b")
        w32 = dict(small, w_oa=(w_o[l], 0), w_ob=(w_o[l], 1), w_g=(w_gu[l], 0), w_u=(w_gu[l], f), w_down=w_down[l],
                   **{name: (w_in_t, i * d_a) for i, name in enumerate(names)})
        c_all = jnp.concatenate([c_prompt, c_sample], axis=0)
        pad = (-c_all.shape[0]) % 16
        mod = _ada(jnp.pad(c_all, ((0, pad), (0, 0))), w_ada[l], b_ada[l])
        mod_p = [mod[:nb_p, i * d:(i + 1) * d] for i in range(6)]
        mod_s = [mod[nb_p:nb_p + nb_s, i * d:(i + 1) * d] for i in range(6)]
        bias = (_band_bias(rel_table[l], x_sample.shape[1]), _band_bias_t(rel_table[l]))
        cache = (cache_k_a[l], cache_v_a[l], cache_logf_a[l], cache_k_b[l], cache_v_b[l])
        y_s, *rest_s, w16 = _layer(y_s, mod_s, cache, w32, bias, nb=nb_s, tr=x_sample.shape[1])
        y_p, *rest_p, _ = _layer(y_p, mod_p, None, dict(small, **w16), bias, nb=1, tr=min(512, x_prompt.shape[1]))
        outs_p.append(rest_p)
        outs_s.append(rest_s)
    stack = lambda outs, i: jnp.stack([o[i] for o in outs])
    return (y_p, y_s, *[stack(outs_p, i) for i in range(5)], *[stack(outs_s, i) for i in range(5)])
```
